```python
import math, functools
import jax, jax.numpy as jnp
from jax import lax
import numpy as np

D_MODEL = 1024
BATCH = 2
SEQ = 8192
DEPTH = 1
DEC_BATCH = 32
DEC_SEQ = 8
PAST_LEN = 8192
PAGE_SIZE = 128

D_RNN = D_MODEL
N_RNN_BLOCKS = 8
RNN_BLOCK = D_RNN // N_RNN_BLOCKS
CONV_W = 4
RG_C = 8.0
N_HEADS = 8
HEAD_DIM = 64
KV_DIM = 2 * HEAD_DIM
D_ATTN = N_HEADS * KV_DIM
D_FF = 3 * D_MODEL
FFN_CONV_W = 3
N_BUCKETS = 32
MAX_DISTANCE = 128
Q_BLOCK = 128
EPS = 1e-6
D_IN = D_RNN + 3 * D_ATTN + 2 * D_MODEL

kernel_name = 'hawk_diffattn_gated_merge_convffn_step'


def rms_norm(x, g):
    xf = x.astype(jnp.float32)
    y = xf * lax.rsqrt(jnp.mean(xf * xf, axis=-1, keepdims=True) + EPS)
    return (y * g.astype(jnp.float32)).astype(x.dtype)


def causal_dwconv(x, buf, w, b):
    width = w.shape[0]
    t = x.shape[1]
    xe = jnp.concatenate([buf.astype(x.dtype), x], axis=1)
    y = b
    for kk in range(width):
        y = y + w[kk] * xe[:, kk:kk + t]
    return y.astype(x.dtype), xe[:, -(width - 1):]


def _lin_combine(c1, c2):
    a1, b1 = c1
    a2, b2 = c2
    return a1 * a2, a2 * b1 + b2


def rg_lru(x, pos, h0, w_a, b_a, w_x, b_x, a_param):
    bsz, t, c = x.shape
    xb = x.reshape(bsz, t, N_RNN_BLOCKS, RNN_BLOCK)
    gate_a = jnp.einsum('btnc,ncd->btnd', xb, w_a).reshape(bsz, t, c) + b_a
    gate_x = jnp.einsum('btnc,ncd->btnd', xb, w_x).reshape(bsz, t, c) + b_x
    r = jax.nn.sigmoid(gate_a.astype(jnp.float32))
    i = jax.nn.sigmoid(gate_x.astype(jnp.float32))
    log_a = -RG_C * r * jax.nn.softplus(-a_param.astype(jnp.float32))
    reset = (pos == 0)[None, :, None]
    a = jnp.where(reset, 0.0, jnp.exp(log_a))
    mult = jnp.where(reset, 1.0, jnp.sqrt(-jnp.expm1(2.0 * log_a)))
    bterm = mult * i * x.astype(jnp.float32)
    a_cum, h = lax.associative_scan(_lin_combine, (a, bterm), axis=1)
    h = h + a_cum * h0.astype(jnp.float32)[:, None]
    return h.astype(x.dtype), h[:, -1].astype(x.dtype)


def rel_bucket(dist):
    n = jnp.maximum(dist, 0)
    max_exact = N_BUCKETS // 2
    nf = jnp.maximum(n, 1).astype(jnp.float32)
    large = max_exact + (jnp.log(nf / max_exact) / math.log(MAX_DISTANCE / max_exact)
                         * (N_BUCKETS - max_exact)).astype(jnp.int32)
    large = jnp.minimum(large, N_BUCKETS - 1)
    return jnp.where(n < max_exact, n, large)


def diff_logits(q, q_pos, k, k_pos, rel_bias):
    s = jnp.einsum('bqhmd,bkhmd->bhmqk', q, k).astype(jnp.float32) * (HEAD_DIM ** -0.5)
    dist = q_pos[:, None] - k_pos[None, :]
    bias = jnp.transpose(rel_bias[rel_bucket(dist)], (2, 0, 1)).astype(jnp.float32)
    s = s + bias[None, :, None]
    return jnp.where((dist >= 0)[None, None, None], s, -1e30)


def diff_weights(logits, lam):
    p = jax.nn.softmax(logits, axis=-1)
    return p[:, :, 0] - lam * p[:, :, 1]


def prompt_attn(q, k, v, lam, rel_bias):
    bsz, t = q.shape[:2]
    nb = t // Q_BLOCK
    k_pos = jnp.arange(t, dtype=jnp.int32)
    qb = q.reshape(bsz, nb, Q_BLOCK, N_HEADS, 2, HEAD_DIM).transpose(1, 0, 2, 3, 4, 5)
    qpos = k_pos.reshape(nb, Q_BLOCK)

    def one_block(args):
        qi, pi = args
        w = diff_weights(diff_logits(qi, pi, k, k_pos, rel_bias), lam)
        return jnp.einsum('bhqk,bkhe->bqhe', w.astype(v.dtype), v)

    o = lax.map(one_block, (qb, qpos))
    return o.transpose(1, 0, 2, 3, 4).reshape(bsz, t, N_HEADS, KV_DIM)


def sample_attn(q, k, v, lam, k_past, v_past, rel_bias):
    t = q.shape[1]
    past = k_past.shape[1]
    q_pos = past + jnp.arange(t, dtype=jnp.int32)
    s_past = diff_logits(q, q_pos, k_past, jnp.arange(past, dtype=jnp.int32), rel_bias)
    s_new = diff_logits(q, q_pos, k, q_pos, rel_bias)
    w = diff_weights(jnp.concatenate([s_past, s_new], axis=-1), lam).astype(v.dtype)
    return (jnp.einsum('bhqk,bkhe->bqhe', w[..., :past], v_past)
            + jnp.einsum('bhqk,bkhe->bqhe', w[..., past:], v))


def layer(x, pos, rnn_h0, rnn_buf, ffn_buf, attend, lp, lambda_init):
    bsz, t, _ = x.shape
    h = rms_norm(x, lp['norm_attn'])
    proj = jnp.einsum('btd,de->bte', h, lp['w_in'])
    o0 = D_RNN
    o1 = o0 + D_ATTN
    o2 = o1 + D_ATTN
    o3 = o2 + D_ATTN
    o4 = o3 + D_MODEL
    x_r = proj[..., :o0]
    q = proj[..., o0:o1].reshape(bsz, t, N_HEADS, 2, HEAD_DIM)
    k = proj[..., o1:o2].reshape(bsz, t, N_HEADS, 2, HEAD_DIM)
    v = proj[..., o2:o3].reshape(bsz, t, N_HEADS, KV_DIM)
    g_a = jax.nn.sigmoid(proj[..., o3:o4])
    g_b = jax.nn.sigmoid(proj[..., o4:])
    x_c, new_rnn_buf = causal_dwconv(x_r, rnn_buf, lp['conv_w'], lp['conv_b'])
    y_a, h_last = rg_lru(x_c, pos, rnn_h0, lp['rg_w_a'], lp['rg_b_a'], lp['rg_w_x'],
                         lp['rg_b_x'], lp['rg_a_param'])
    f32 = jnp.float32
    lam = (jnp.exp(jnp.sum(lp['lq1'].astype(f32) * lp['lk1'].astype(f32)))
           - jnp.exp(jnp.sum(lp['lq2'].astype(f32) * lp['lk2'].astype(f32))) + lambda_init)
    o = attend(q, k, v, lam)
    y_b = (rms_norm(o, lp['subln_g']) * (1.0 - lambda_init)).reshape(bsz, t, D_ATTN)
    x = x + jnp.einsum('btd,de->bte', g_a * y_a + g_b * y_b, lp['w_out'])
    h2 = rms_norm(x, lp['norm_ffn'])
    up = jnp.einsum('btd,df->btf', h2, lp['w_up'])
    up_c, new_ffn_buf = causal_dwconv(up, ffn_buf, lp['ffn_conv_w'], lp['ffn_conv_b'])
    act = jax.nn.gelu(up_c[..., :D_FF]) * up_c[..., D_FF:]
    x = x + jnp.einsum('btf,fd->btd', act, lp['w_down'])
    return x, k.reshape(bsz, t, N_HEADS, KV_DIM), v, h_last, new_rnn_buf, new_ffn_buf


def setup_inputs(seed: int = 0) -> dict:
    key = jax.random.key(seed)
    ks = jax.random.split(key, 32)
    f32 = jnp.float32
    n_pages = PAST_LEN // PAGE_SIZE
    n_used = DEC_BATCH * n_pages
    n_pool = (n_used * 5) // 4
    nrm = lambda k, shape, s: (jax.random.normal(k, shape, f32) * s)
    a0 = jax.random.uniform(ks[10], (DEPTH, D_RNN), f32, 0.9, 0.999)
    return {
        'x_prompt': nrm(ks[0], (BATCH, SEQ, D_MODEL), 1.0),
        'x_sample': nrm(ks[1], (DEC_BATCH, DEC_SEQ, D_MODEL), 1.0),
        'cache_k': nrm(ks[2], (DEPTH, n_pool, PAGE_SIZE, N_HEADS, KV_DIM), 1.0),
        'cache_v': nrm(ks[3], (DEPTH, n_pool, PAGE_SIZE, N_HEADS, KV_DIM), 1.0),
        'page_table': jax.random.permutation(ks[4], n_pool)[:n_used].reshape(DEC_BATCH, n_pages).astype(jnp.int32),
        'state_rglru_h': nrm(ks[5], (DEPTH, DEC_BATCH, D_RNN), 0.5),
        'state_rglru_conv': nrm(ks[6], (DEPTH, DEC_BATCH, CONV_W - 1, D_RNN), 1.0),
        'state_ffn_conv': nrm(ks[7], (DEPTH, DEC_BATCH, FFN_CONV_W - 1, 2 * D_FF), 1.0),
        'w_in': nrm(ks[8], (DEPTH, D_MODEL, D_IN), D_MODEL ** -0.5),
        'conv_w': nrm(ks[9], (DEPTH, CONV_W, D_RNN), CONV_W ** -0.5),
        'conv_b': nrm(ks[11], (DEPTH, D_RNN), 0.01),
        'rg_w_a': nrm(ks[12], (DEPTH, N_RNN_BLOCKS, RNN_BLOCK, RNN_BLOCK), RNN_BLOCK ** -0.5),
        'rg_b_a': nrm(ks[13], (DEPTH, D_RNN), 0.01),
        'rg_w_x': nrm(ks[14], (DEPTH, N_RNN_BLOCKS, RNN_BLOCK, RNN_BLOCK), RNN_BLOCK ** -0.5),
        'rg_b_x': nrm(ks[15], (DEPTH, D_RNN), 0.01),
        'rg_a_param': jnp.log(a0) - jnp.log1p(-a0),
        'lambda_q1': nrm(ks[16], (DEPTH, HEAD_DIM), 0.1),
        'lambda_k1': nrm(ks[17], (DEPTH, HEAD_DIM), 0.1),
        'lambda_q2': nrm(ks[18], (DEPTH, HEAD_DIM), 0.1),
        'lambda_k2': nrm(ks[19], (DEPTH, HEAD_DIM), 0.1),
        'subln_g': 1.0 + nrm(ks[20], (DEPTH, KV_DIM), 0.02),
        'rel_bias': nrm(ks[21], (N_BUCKETS, N_HEADS), 0.5),
        'w_out': nrm(ks[22], (DEPTH, D_MODEL, D_MODEL), D_MODEL ** -0.5),
        'norm_attn': 1.0 + nrm(ks[23], (DEPTH, D_MODEL), 0.02),
        'norm_ffn': 1.0 + nrm(ks[24], (DEPTH, D_MODEL), 0.02),
        'w_up': nrm(ks[25], (DEPTH, D_MODEL, 2 * D_FF), D_MODEL ** -0.5),
        'ffn_conv_w': nrm(ks[26], (DEPTH, FFN_CONV_W, 2 * D_FF), FFN_CONV_W ** -0.5),
        'ffn_conv_b': nrm(ks[27], (DEPTH, 2 * D_FF), 0.01),
        'w_down': nrm(ks[28], (DEPTH, D_FF, D_MODEL), D_FF ** -0.5),
        'norm_final': 1.0 + nrm(ks[29], (D_MODEL,), 0.02),
    }


def reference(x_prompt, x_sample, cache_k, cache_v, page_table, state_rglru_h, state_rglru_conv,
              state_ffn_conv, w_in, conv_w, conv_b, rg_w_a, rg_b_a, rg_w_x, rg_b_x, rg_a_param,
              lambda_q1, lambda_k1, lambda_q2, lambda_k2, subln_g, rel_bias, w_out, norm_attn,
              norm_ffn, w_up, ffn_conv_w, ffn_conv_b, w_down, norm_final):
    bsz, t = x_prompt.shape[:2]
    dbsz = x_sample.shape[0]
    tsz = x_sample.shape[1]
    past = page_table.shape[1] * cache_k.shape[2]
    pos_p = jnp.arange(t, dtype=jnp.int32)
    pos_s = past + jnp.arange(tsz, dtype=jnp.int32)
    yp, ys = x_prompt, x_sample
    kp_l, vp_l, ks_l, vs_l, hp_l, hs_l, rp_l, rs_l, fp_l, fs_l = ([] for _ in range(10))
    for l in range(DEPTH):
        lp = {'w_in': w_in[l], 'conv_w': conv_w[l], 'conv_b': conv_b[l], 'rg_w_a': rg_w_a[l],
              'rg_b_a': rg_b_a[l], 'rg_w_x': rg_w_x[l], 'rg_b_x': rg_b_x[l],
              'rg_a_param': rg_a_param[l], 'lq1': lambda_q1[l], 'lk1': lambda_k1[l],
              'lq2': lambda_q2[l], 'lk2': lambda_k2[l], 'subln_g': subln_g[l], 'w_out': w_out[l],
              'norm_attn': norm_attn[l], 'norm_ffn': norm_ffn[l], 'w_up': w_up[l],
              'ffn_conv_w': ffn_conv_w[l], 'ffn_conv_b': ffn_conv_b[l], 'w_down': w_down[l]}
        lambda_init = 0.8 - 0.6 * math.exp(-0.3 * l)
        yp, kp, vp, hp, rp, fp = layer(
            yp, pos_p, jnp.zeros((bsz, D_RNN), x_prompt.dtype),
            jnp.zeros((bsz, CONV_W - 1, D_RNN), x_prompt.dtype),
            jnp.zeros((bsz, FFN_CONV_W - 1, 2 * D_FF), x_prompt.dtype),
            functools.partial(prompt_attn, rel_bias=rel_bias), lp, lambda_init)
        k_past = cache_k[l][page_table].reshape(dbsz, past, N_HEADS, 2, HEAD_DIM)
        v_past = cache_v[l][page_table].reshape(dbsz, past, N_HEADS, KV_DIM)
        ys, ks_, vs_, hs, rs, fs = layer(
            ys, pos_s, state_rglru_h[l], state_rglru_conv[l], state_ffn_conv[l],
            functools.partial(sample_attn, k_past=k_past, v_past=v_past, rel_bias=rel_bias),
            lp, lambda_init)
        kp_l.append(kp); vp_l.append(vp); ks_l.append(ks_); vs_l.append(vs_)
        hp_l.append(hp); hs_l.append(hs); rp_l.append(rp); rs_l.append(rs)
        fp_l.append(fp); fs_l.append(fs)
    y_prompt = rms_norm(yp, norm_final)
    y_sample = rms_norm(ys, norm_final)
    return (y_prompt, y_sample, jnp.stack(kp_l), jnp.stack(vp_l), jnp.stack(ks_l), jnp.stack(vs_l),
            jnp.stack(hp_l), jnp.stack(hs_l), jnp.stack(rp_l), jnp.stack(rs_l),
            jnp.stack(fp_l), jnp.stack(fs_l))
```

```python
import functools
import math

import jax
import jax.numpy as jnp
from jax import lax
from jax.experimental import pallas as pl
from jax.experimental.pallas import tpu as pltpu

F32 = jnp.float32
BF16 = jnp.bfloat16

EPS = 1e-6
RG_C = 8.0
N_BUCKETS = 32
MAX_DISTANCE = 128
NEG = -1e30

SUBLANES = 8
LANES = 128
ROW_TILE = 256
ATTN_BLOCK = 256
FFN_CHUNK = 512
PAGES_PER_STEP = 8
VMEM_LIMIT = 52 * 1024 * 1024


def _params(n_axes, vmem=VMEM_LIMIT):
    return pltpu.CompilerParams(
        dimension_semantics=("arbitrary",) * n_axes, vmem_limit_bytes=vmem)


def _resident(shape):
    nd = len(shape)
    return pl.BlockSpec(shape, lambda *_: (0,) * nd, pipeline_mode=pl.Buffered(1))


def _sigmoid(x):
    return 1.0 / (1.0 + jnp.exp(-x))


def _rms(x, g):
    return x * lax.rsqrt(jnp.mean(x * x, axis=-1, keepdims=True) + EPS) * g


def _inproj_kernel(x_ref, g_ref, w_ref, xr_ref, q2_ref, k_ref, kb_ref, v_ref, vb_ref,
                   ga_ref, gb_ref, *, d, head_dim):
    h = _rms(x_ref[...], g_ref[...]).astype(BF16)

    def proj(j):
        return jnp.dot(h, w_ref[:, j * d:(j + 1) * d], preferred_element_type=F32)

    xr_ref[...] = proj(0)
    q = proj(1) * (head_dim ** -0.5)
    lane = lax.broadcasted_iota(jnp.int32, q.shape, 1)
    first = (lane % (2 * head_dim)) < head_dim
    q2_ref[0] = jnp.where(first, q, 0.0).astype(BF16)
    q2_ref[1] = jnp.where(first, 0.0, q).astype(BF16)
    k = proj(2)
    k_ref[...] = k
    kb_ref[...] = k.astype(BF16)
    v = proj(3)
    v_ref[...] = v
    vb_ref[...] = v.astype(BF16)
    ga_ref[...] = _sigmoid(proj(4))
    gb_ref[...] = _sigmoid(proj(5))


def _inproj(x2d, g, w_bf, head_dim):
    n, d = x2d.shape
    tm = min(ROW_TILE, n)
    row = lambda i: (i, 0)
    blk = pl.BlockSpec((tm, d), row)
    f32 = jax.ShapeDtypeStruct((n, d), F32)
    b16 = jax.ShapeDtypeStruct((n, d), BF16)
    return pl.pallas_call(
        functools.partial(_inproj_kernel, d=d, head_dim=head_dim),
        grid=(n // tm,),
        in_specs=[blk, _resident((1, d)), _resident(w_bf.shape)],
        out_specs=[blk, pl.BlockSpec((2, tm, d), lambda i: (0, i, 0)), blk, blk, blk, blk, blk, blk],
        out_shape=[f32, jax.ShapeDtypeStruct((2, n, d), BF16), f32, b16, f32, b16, f32, f32],
        compiler_params=_params(1),
        name="inproj",
    )(x2d, g.reshape(1, d), w_bf)


def _shifted(x3, prev3, d, t_idx):
    return jnp.where(t_idx >= d, pltpu.roll(x3, d, 1), pltpu.roll(prev3, d, 1))


def _causal_conv(x3, prev3, w, b):
    width = w.shape[0]
    t_idx = lax.broadcasted_iota(jnp.int32, x3.shape, 1)
    y = b + w[width - 1:width] * x3
    for dd in range(1, width):
        y = y + w[width - 1 - dd:width - dd] * _shifted(x3, prev3, dd, t_idx)
    return y


def _rglru_kernel(xr_ref, spad_ref, h0_ref, cw_ref, cb_ref, wa_ref, ba_ref, wx_ref, bx_ref,
                  ap_ref, ya_ref, hlast_ref, carry_x, carry_h, a_scr, b_scr,
                  *, group_mode, pos_base, n_blocks):
    ti = pl.program_id(1)
    tt, c = xr_ref.shape
    g = tt // SUBLANES
    blk = c // n_blocks
    x3 = xr_ref[...].reshape(g, SUBLANES, c)

    if group_mode:
        prev3 = spad_ref[...]
    else:
        @pl.when(ti == 0)
        def _():
            carry_x[...] = spad_ref[0]
            carry_h[...] = h0_ref[0]
        prev3 = jnp.concatenate([carry_x[...][None], x3[:g - 1]], axis=0) if g > 1 else carry_x[...][None]

    xc3 = _causal_conv(x3, prev3, cw_ref[...], cb_ref[...])
    if not group_mode:
        carry_x[...] = x3[g - 1]

    xc = xc3.reshape(tt, c)
    xcb = xc.astype(BF16)

    def gate(w_ref, b_ref):
        parts = [jnp.dot(xcb[:, n * blk:(n + 1) * blk], w_ref[n], preferred_element_type=F32)
                 for n in range(n_blocks)]
        return jnp.concatenate(parts, axis=1) + b_ref[...]

    r = _sigmoid(gate(wa_ref, ba_ref))
    i = _sigmoid(gate(wx_ref, bx_ref))
    z = -ap_ref[...]
    softplus = jnp.maximum(z, 0.0) + jnp.log(1.0 + jnp.exp(-jnp.abs(z)))
    log_a = -RG_C * r * softplus
    a = jnp.exp(log_a)
    th = jnp.tanh(log_a)
    mult = jnp.sqrt(-2.0 * th / (1.0 - th))
    row = lax.broadcasted_iota(jnp.int32, (tt, c), 0)
    pos = pos_base + ((row % SUBLANES) if group_mode else (ti * tt + row))
    reset = pos == 0
    a = jnp.where(reset, 0.0, a)
    mult = jnp.where(reset, 1.0, mult)
    b = mult * i * xc

    a3 = a.reshape(g, SUBLANES, c)
    b3 = b.reshape(g, SUBLANES, c)
    t_idx = lax.broadcasted_iota(jnp.int32, a3.shape, 1)
    for dd in (1, 2, 4):
        valid = t_idx >= dd
        b3 = jnp.where(valid, a3 * pltpu.roll(b3, dd, 1) + b3, b3)
        a3 = jnp.where(valid, a3 * pltpu.roll(a3, dd, 1), a3)

    if group_mode:
        h3 = a3 * h0_ref[...] + b3
        ya_ref[...] = h3.reshape(tt, c)
        hlast_ref[...] = h3
    else:
        a_scr[...] = a3
        b_scr[...] = b3

        def body(gi, carry):
            hg = a_scr[gi] * carry + b_scr[gi]
            ya_ref[pl.ds(pl.multiple_of(gi * SUBLANES, SUBLANES), SUBLANES), :] = hg
            return jnp.broadcast_to(hg[SUBLANES - 1:SUBLANES, :], (SUBLANES, c))

        last = lax.fori_loop(0, g, body, carry_h[...])
        carry_h[...] = last
        hlast_ref[0] = last


def _rglru(xr2d, spad, h0b, cw, cb, wa_bf, ba, wx_bf, bx, ap, *, n_seq, group_mode, pos_base):
    n, c = xr2d.shape
    n_blocks = wa_bf.shape[0]
    if group_mode:
        tt, grid = n, (1, 1)
        row_spec = pl.BlockSpec((n, c), lambda b, t: (0, 0))
        seq_spec = pl.BlockSpec((n_seq, SUBLANES, c), lambda b, t: (0, 0, 0))
    else:
        t_len = n // n_seq
        tt = min(ROW_TILE, t_len)
        nt = t_len // tt
        grid = (n_seq, nt)
        row_spec = pl.BlockSpec((tt, c), lambda b, t: (b * nt + t, 0))
        seq_spec = pl.BlockSpec((1, SUBLANES, c), lambda b, t: (b, 0, 0))
    g = tt // SUBLANES
    vec = lambda a: a.reshape(1, c)
    return pl.pallas_call(
        functools.partial(_rglru_kernel, group_mode=group_mode, pos_base=pos_base, n_blocks=n_blocks),
        grid=grid,
        in_specs=[row_spec, seq_spec, seq_spec, _resident(cw.shape), _resident((1, c)),
                  _resident(wa_bf.shape), _resident((1, c)), _resident(wx_bf.shape), _resident((1, c)),
                  _resident((1, c))],
        out_specs=[row_spec, seq_spec],
        out_shape=[jax.ShapeDtypeStruct((n, c), F32), jax.ShapeDtypeStruct((n_seq, SUBLANES, c), F32)],
        scratch_shapes=[pltpu.VMEM((SUBLANES, c), F32), pltpu.VMEM((SUBLANES, c), F32),
                        pltpu.VMEM((g, SUBLANES, c), F32), pltpu.VMEM((g, SUBLANES, c), F32)],
        compiler_params=_params(2),
        name="rglru",
    )(xr2d, spad, h0b, cw, vec(cb), wa_bf, vec(ba), wx_bf, vec(bx), vec(ap))


def _bias_of(dist, rb_ref, h, extra_mask=None):
    n = jnp.maximum(dist, 0)
    max_exact = N_BUCKETS // 2
    nf = jnp.maximum(n, 1).astype(F32)
    large = max_exact + (jnp.log(nf / max_exact) / math.log(MAX_DISTANCE / max_exact)
                         * (N_BUCKETS - max_exact)).astype(jnp.int32)
    large = jnp.minimum(large, N_BUCKETS - 1)
    bucket = jnp.where(n < max_exact, n, large)
    base = rb_ref[N_BUCKETS - 1, h]
    out = jnp.zeros(dist.shape, F32)
    for b in range(N_BUCKETS - 1):
        out = jnp.where(bucket == b, rb_ref[b, h] - base, out)
    ok = dist >= 0
    if extra_mask is not None:
        ok = jnp.logical_and(ok, extra_mask)
    return jnp.where(ok, out, NEG)


def _bias_kernel(rb_ref, bp_ref, bl_ref, bn_ref, *, tb, chunk, dec_seq):
    h = pl.program_id(0)
    i = lax.broadcasted_iota(jnp.int32, (tb, 2 * tb), 0)
    j = lax.broadcasted_iota(jnp.int32, (tb, 2 * tb), 1)
    bp_ref[0] = _bias_of(i + tb - j, rb_ref, h)
    rows = bl_ref.shape[0]
    t = lax.broadcasted_iota(jnp.int32, (rows, chunk), 0) % dec_seq
    j = lax.broadcasted_iota(jnp.int32, (rows, chunk), 1)
    bl_ref[...] = _bias_of(chunk + t - j, rb_ref, h)
    t = lax.broadcasted_iota(jnp.int32, (rows, LANES), 0) % dec_seq
    j = lax.broadcasted_iota(jnp.int32, (rows, LANES), 1)
    bn_ref[...] = _bias_of(t - j, rb_ref, h, extra_mask=j < dec_seq)


def _bias_tiles(rel_bias, n_heads, tb, chunk, dec_seq):
    rows = 2 * dec_seq
    return pl.pallas_call(
        functools.partial(_bias_kernel, tb=tb, chunk=chunk, dec_seq=dec_seq),
        grid=(n_heads,),
        in_specs=[pl.BlockSpec(memory_space=pltpu.SMEM)],
        out_specs=[pl.BlockSpec((1, tb, 2 * tb), lambda h: (h, 0, 0)),
                   pl.BlockSpec((rows, chunk), lambda h: (h, 0)),
                   pl.BlockSpec((rows, LANES), lambda h: (h, 0))],
        out_shape=[jax.ShapeDtypeStruct((n_heads, tb, 2 * tb), F32),
                   jax.ShapeDtypeStruct((n_heads * rows, chunk), F32),
                   jax.ShapeDtypeStruct((n_heads * rows, LANES), F32)],
        compiler_params=_params(1),
        name="bias_tiles",
    )(rel_bias)


def _lambda(lq1, lk1, lq2, lk2, lambda_init):
    s1 = jnp.sum(lq1 * lk1, axis=-1, keepdims=True)
    s2 = jnp.sum(lq2 * lk2, axis=-1, keepdims=True)
    return jnp.exp(s1) - jnp.exp(s2) + lambda_init


def _softmax_step(s, v, m_ref, l_ref, acc_ref):
    m_prev = m_ref[...]
    m_new = jnp.maximum(m_prev, jnp.max(s, axis=-1, keepdims=True))
    alpha = jnp.exp(m_prev - m_new)
    p = jnp.exp(s - m_new[:, :1])
    l_ref[...] = alpha * l_ref[...] + jnp.sum(p, axis=-1, keepdims=True)
    pv = jnp.dot(p.astype(BF16), v, preferred_element_type=F32)
    if acc_ref.shape[1] == alpha.shape[1]:
        acc_ref[...] = alpha * acc_ref[...] + pv
    else:
        acc_ref[...] = alpha[:, :1] * acc_ref[...] + pv
    m_ref[...] = m_new


def _qk(q, k):
    return lax.dot_general(q, k, (((1,), (1,)), ((), ())), preferred_element_type=F32)


def _prompt_attn_kernel(q2_ref, k_ref, v_ref, bias_ref, lq1_ref, lk1_ref, lq2_ref, lk2_ref, sg_ref,
                        o_ref, m_ref, l_ref, acc_ref, *, tb, lambda_init):
    qi = pl.program_id(2)
    q = q2_ref[...].reshape(2 * tb, q2_ref.shape[2])
    m_ref[...] = jnp.full(m_ref.shape, NEG, F32)
    l_ref[...] = jnp.zeros(l_ref.shape, F32)
    acc_ref[...] = jnp.zeros(acc_ref.shape, F32)

    def kv(kj):
        start = pl.multiple_of(kj * tb, tb)
        return k_ref[pl.ds(start, tb), :], v_ref[pl.ds(start, tb), :]

    def far(kj, carry):
        kb, vb = kv(kj)
        _softmax_step(_qk(q, kb), vb, m_ref, l_ref, acc_ref)
        return carry

    lax.fori_loop(0, jnp.maximum(qi - 1, 0), far, 0)

    @pl.when(qi > 0)
    def _():
        kb, vb = kv(qi - 1)
        bias = bias_ref[0, :, :tb]
        s = _qk(q, kb) + jnp.concatenate([bias, bias], axis=0)
        _softmax_step(s, vb, m_ref, l_ref, acc_ref)

    kb, vb = kv(qi)
    bias = bias_ref[0, :, tb:]
    s = _qk(q, kb) + jnp.concatenate([bias, bias], axis=0)
    _softmax_step(s, vb, m_ref, l_ref, acc_ref)

    lam = _lambda(lq1_ref[...], lk1_ref[...], lq2_ref[...], lk2_ref[...], lambda_init)
    acc = acc_ref[...]
    l = l_ref[...]
    o = acc[:tb] / l[:tb] - lam * (acc[tb:] / l[tb:])
    o_ref[...] = _rms(o, sg_ref[...]) * (1.0 - lambda_init)


def _prompt_attn(q2, kb, vb, bias_p, lams, subln_g, *, n_seq, n_heads, lambda_init):
    _, n, d = q2.shape
    e = d // n_heads
    t_len = n // n_seq
    tb = ATTN_BLOCK
    nq = t_len // tb
    hd = lams[0].shape[-1]
    small = pl.BlockSpec((1, hd), lambda b, h, i: (0, 0))
    return pl.pallas_call(
        functools.partial(_prompt_attn_kernel, tb=tb, lambda_init=lambda_init),
        grid=(n_seq, n_heads, nq),
        in_specs=[pl.BlockSpec((2, tb, e), lambda b, h, i: (0, b * nq + i, h)),
                  pl.BlockSpec((t_len, e), lambda b, h, i: (b, h)),
                  pl.BlockSpec((t_len, e), lambda b, h, i: (b, h)),
                  pl.BlockSpec((1, tb, 2 * tb), lambda b, h, i: (h, 0, 0)),
                  small, small, small, small,
                  pl.BlockSpec((1, e), lambda b, h, i: (0, 0))],
        out_specs=pl.BlockSpec((tb, e), lambda b, h, i: (b * nq + i, h)),
        out_shape=jax.ShapeDtypeStruct((n, d), F32),
        scratch_shapes=[pltpu.VMEM((2 * tb, LANES), F32), pltpu.VMEM((2 * tb, LANES), F32),
                        pltpu.VMEM((2 * tb, e), F32)],
        compiler_params=_params(3),
        name="prompt_attn",
    )(q2, kb, vb, bias_p, *[x.reshape(1, hd) for x in lams], subln_g.reshape(1, e))


def _sample_attn_kernel(pt_ref, q2_ref, *refs, n_heads, dec_seq, pps, lambda_init):
    kp = refs[:pps]
    vp = refs[pps:2 * pps]
    (kn_ref, vn_ref, bl_ref, bn_ref, lq1_ref, lk1_ref, lq2_ref, lk2_ref, sg_ref,
     o_ref, qbd_ref, kc_ref, vc_ref, m_ref, l_ref, acc_ref) = refs[2 * pps:]
    del pt_ref
    c = pl.program_id(1)
    nc = pl.num_programs(1)
    page, d = kp[0].shape
    e = d // n_heads
    rows = 2 * dec_seq

    @pl.when(c == 0)
    def _():
        lane_head = lax.broadcasted_iota(jnp.int32, (dec_seq, d), 1) // e
        pieces = []
        for h in range(n_heads):
            for mm in range(2):
                pieces.append(jnp.where(lane_head == h, q2_ref[mm], 0.0))
        qbd_ref[...] = jnp.concatenate(pieces, axis=0).astype(BF16)
        m_ref[...] = jnp.full(m_ref.shape, NEG, F32)
        l_ref[...] = jnp.zeros(l_ref.shape, F32)
        acc_ref[...] = jnp.zeros(acc_ref.shape, F32)

    for j in range(pps):
        kc_ref[j * page:(j + 1) * page, :] = kp[j][...].astype(BF16)
        vc_ref[j * page:(j + 1) * page, :] = vp[j][...].astype(BF16)
    q = qbd_ref[...]
    is_last = (c == nc - 1).astype(F32)
    s = _qk(q, kc_ref[...]) + bl_ref[...] * is_last
    _softmax_step(s, vc_ref[...], m_ref, l_ref, acc_ref)

    @pl.when(c == nc - 1)
    def _():
        pad = jnp.zeros((LANES - dec_seq, d), F32)
        kn = jnp.concatenate([kn_ref[...], pad], axis=0).astype(BF16)
        vn = jnp.concatenate([vn_ref[...], pad], axis=0).astype(BF16)
        _softmax_step(_qk(q, kn) + bn_ref[...], vn, m_ref, l_ref, acc_ref)

        lam = _lambda(lq1_ref[...], lk1_ref[...], lq2_ref[...], lk2_ref[...], lambda_init)
        acc = acc_ref[...]
        l = l_ref[...]
        outs = []
        for h in range(n_heads):
            r0 = h * rows
            cols = slice(h * e, (h + 1) * e)
            o0 = acc[r0:r0 + dec_seq, cols] / l[r0:r0 + dec_seq]
            o1 = acc[r0 + dec_seq:r0 + rows, cols] / l[r0 + dec_seq:r0 + rows]
            outs.append(_rms(o0 - lam * o1, sg_ref[...]) * (1.0 - lambda_init))
        o_ref[...] = jnp.concatenate(outs, axis=1)


def _sample_attn(page_table, q2f, cache_k, cache_v, k_new, v_new, bias_l, bias_n, lams, subln_g,
                 *, n_seq, n_heads, lambda_init):
    _, n, d = q2f.shape
    dec_seq = n // n_seq
    e = d // n_heads
    n_pool, page = cache_k.shape[:2]
    n_pages = page_table.shape[1]
    pps = PAGES_PER_STEP
    nc = n_pages // pps
    hd = lams[0].shape[-1]
    rows = n_heads * 2 * dec_seq

    def page_spec(j):
        return pl.BlockSpec((None, page, d), lambda b, c, pt: (pt[b, c * pps + j], 0, 0))

    const2 = lambda b, c, pt: (0, 0)
    small = pl.BlockSpec((1, hd), const2)
    seq = pl.BlockSpec((dec_seq, d), lambda b, c, pt: (b, 0))
    grid_spec = pltpu.PrefetchScalarGridSpec(
        num_scalar_prefetch=1,
        grid=(n_seq, nc),
        in_specs=[pl.BlockSpec((2, dec_seq, d), lambda b, c, pt: (0, b, 0))]
                 + [page_spec(j) for j in range(pps)] + [page_spec(j) for j in range(pps)]
                 + [seq, seq,
                    pl.BlockSpec((rows, pps * page), const2), pl.BlockSpec((rows, LANES), const2),
                    small, small, small, small, pl.BlockSpec((1, e), const2)],
        out_specs=seq,
        scratch_shapes=[pltpu.VMEM((rows, d), BF16),
                        pltpu.VMEM((pps * page, d), BF16), pltpu.VMEM((pps * page, d), BF16),
                        pltpu.VMEM((rows, LANES), F32), pltpu.VMEM((rows, LANES), F32),
                        pltpu.VMEM((rows, d), F32)],
    )
    ck = cache_k.reshape(n_pool, page, d)
    cv = cache_v.reshape(n_pool, page, d)
    return pl.pallas_call(
        functools.partial(_sample_attn_kernel, n_heads=n_heads, dec_seq=dec_seq, pps=pps,
                          lambda_init=lambda_init),
        grid_spec=grid_spec,
        out_shape=jax.ShapeDtypeStruct((n, d), F32),
        compiler_params=_params(2),
        name="sample_attn",
    )(page_table, q2f, *([ck] * pps), *([cv] * pps), k_new, v_new, bias_l, bias_n,
      *[x.reshape(1, hd) for x in lams], subln_g.reshape(1, e))


def _gelu_tanh(x):
    return 0.5 * x * (1.0 + jnp.tanh(math.sqrt(2.0 / math.pi) * (x + 0.044715 * (x * x * x))))


def _ffn_kernel(x_ref, ya_ref, yb_ref, ga_ref, gb_ref, spad_ref, wout_ref, nffn_ref, wup_ref,
                cw_ref, cb_ref, wdown_ref, nfin_ref, y_ref, tail_ref, carry_ref,
                *, group_mode, final_norm):
    ti = pl.program_id(1)
    tm, d = x_ref.shape
    g = tm // SUBLANES
    d_ff = wdown_ref.shape[0]

    merged = (ga_ref[...] * ya_ref[...] + gb_ref[...] * yb_ref[...]).astype(BF16)
    x2 = x_ref[...] + jnp.dot(merged, wout_ref[...], preferred_element_type=F32)
    h2 = _rms(x2, nffn_ref[...]).astype(BF16)

    if not group_mode:
        @pl.when(ti == 0)
        def _():
            carry_ref[...] = spad_ref[0]

    def up_conv(cols):
        up3 = jnp.dot(h2, wup_ref[:, cols], preferred_element_type=F32).reshape(g, SUBLANES, -1)
        if group_mode:
            prev3 = spad_ref[:, :, cols]
            tail_ref[:, :, cols] = up3
        else:
            first = carry_ref[:, cols][None]
            prev3 = jnp.concatenate([first, up3[:g - 1]], axis=0) if g > 1 else first
            carry_ref[:, cols] = up3[g - 1]
            tail_ref[0, :, cols] = up3[g - 1]
        return _causal_conv(up3, prev3, cw_ref[:, cols], cb_ref[:, cols]).reshape(tm, -1)

    down = jnp.zeros((tm, d), F32)
    for c0 in range(0, d_ff, FFN_CHUNK):
        gate = up_conv(slice(c0, c0 + FFN_CHUNK))
        val = up_conv(slice(d_ff + c0, d_ff + c0 + FFN_CHUNK))
        act = (_gelu_tanh(gate) * val).astype(BF16)
        down = down + jnp.dot(act, wdown_ref[c0:c0 + FFN_CHUNK, :], preferred_element_type=F32)
    x3 = x2 + down
    y_ref[...] = _rms(x3, nfin_ref[...]) if final_norm else x3


def _ffn(x2d, ya, yb, ga, gb, spad, wout_bf, nffn, wup_bf, cw, cb, wdown_bf, nfin,
         *, n_seq, group_mode, final_norm):
    n, d = x2d.shape
    f2 = wup_bf.shape[1]
    if group_mode:
        tm, grid = n, (1, 1)
        row = pl.BlockSpec((n, d), lambda b, t: (0, 0))
        seq = pl.BlockSpec((n_seq, SUBLANES, f2), lambda b, t: (0, 0, 0))
    else:
        t_len = n // n_seq
        tm = min(ROW_TILE, t_len)
        nt = t_len // tm
        grid = (n_seq, nt)
        row = pl.BlockSpec((tm, d), lambda b, t: (b * nt + t, 0))
        seq = pl.BlockSpec((1, SUBLANES, f2), lambda b, t: (b, 0, 0))
    return pl.pallas_call(
        functools.partial(_ffn_kernel, group_mode=group_mode, final_norm=final_norm),
        grid=grid,
        in_specs=[row, row, row, row, row, seq, _resident(wout_bf.shape), _resident((1, d)),
                  _resident(wup_bf.shape), _resident(cw.shape), _resident((1, f2)),
                  _resident(wdown_bf.shape), _resident((1, d))],
        out_specs=[row, seq],
        out_shape=[jax.ShapeDtypeStruct((n, d), F32), jax.ShapeDtypeStruct((n_seq, SUBLANES, f2), F32)],
        scratch_shapes=[pltpu.VMEM((SUBLANES, f2), F32)],
        compiler_params=_params(2),
        name="merge_ffn",
    )(x2d, ya, yb, ga, gb, spad, wout_bf, nffn.reshape(1, d), wup_bf, cw, cb.reshape(1, f2),
      wdown_bf, nfin.reshape(1, d))


def _pad_state(buf):
    b, w1, c = buf.shape
    return jnp.concatenate([jnp.zeros((b, SUBLANES - w1, c), buf.dtype), buf], axis=1)


def kernel(x_prompt, x_sample, cache_k, cache_v, page_table, state_rglru_h, state_rglru_conv, state_ffn_conv, w_in, conv_w, conv_b, rg_w_a, rg_b_a, rg_w_x, rg_b_x, rg_a_param, lambda_q1, lambda_k1, lambda_q2, lambda_k2, subln_g, rel_bias, w_out, norm_attn, norm_ffn, w_up, ffn_conv_w, ffn_conv_b, w_down, norm_final):
    bsz, t_len, d = x_prompt.shape
    dbsz, dec_seq, _ = x_sample.shape
    depth = w_in.shape[0]
    n_heads = cache_k.shape[3]
    kv_dim = cache_k.shape[4]
    head_dim = kv_dim // 2
    page = cache_k.shape[2]
    past = page_table.shape[1] * page
    cw_w = conv_w.shape[1]
    fw_w = ffn_conv_w.shape[1]
    assert w_in.shape[2] == 6 * d and n_heads * kv_dim == d and rg_a_param.shape[1] == d
    assert dec_seq == SUBLANES and t_len % ATTN_BLOCK == 0 and page_table.shape[1] % PAGES_PER_STEP == 0
    assert ATTN_BLOCK >= MAX_DISTANCE and PAGES_PER_STEP * page >= MAX_DISTANCE

    bias_p, bias_l, bias_n = _bias_tiles(rel_bias, n_heads, ATTN_BLOCK, PAGES_PER_STEP * page, dec_seq)

    xp = x_prompt.reshape(bsz * t_len, d)
    xs = x_sample.reshape(dbsz * dec_seq, d)
    outs = [[] for _ in range(10)]
    for l in range(depth):
        lambda_init = 0.8 - 0.6 * math.exp(-0.3 * l)
        last = l == depth - 1
        w_in_bf = w_in[l].astype(BF16)
        wa_bf = rg_w_a[l].astype(BF16)
        wx_bf = rg_w_x[l].astype(BF16)
        wout_bf = w_out[l].astype(BF16)
        wup_bf = w_up[l].astype(BF16)
        wdown_bf = w_down[l].astype(BF16)
        lams = (lambda_q1[l], lambda_k1[l], lambda_q2[l], lambda_k2[l])

        def branch_a(xr, spad, h0, n_seq, group_mode, pos_base):
            h0b = jnp.broadcast_to(h0[:, None, :], (n_seq, SUBLANES, d))
            return _rglru(xr, spad, h0b, conv_w[l], conv_b[l], wa_bf, rg_b_a[l], wx_bf, rg_b_x[l],
                          rg_a_param[l], n_seq=n_seq, group_mode=group_mode, pos_base=pos_base)

        def channel_mix(x2d, ya, yb, ga, gb, spad, n_seq, group_mode):
            return _ffn(x2d, ya, yb, ga, gb, spad, wout_bf, norm_ffn[l], wup_bf, ffn_conv_w[l],
                        ffn_conv_b[l], wdown_bf, norm_final, n_seq=n_seq, group_mode=group_mode,
                        final_norm=last)

        xr, q2, k, kb, v, vb, ga, gb = _inproj(xp, norm_attn[l], w_in_bf, head_dim)
        ya, hp = branch_a(xr, jnp.zeros((bsz, SUBLANES, d), F32), jnp.zeros((bsz, d), F32),
                          bsz, False, 0)
        yb = _prompt_attn(q2, kb, vb, bias_p, lams, subln_g[l], n_seq=bsz, n_heads=n_heads,
                          lambda_init=lambda_init)
        xp, fp = channel_mix(xp, ya, yb, ga, gb, jnp.zeros((bsz, SUBLANES, ffn_conv_w.shape[2]), F32),
                             bsz, False)
        outs[0].append(k.reshape(bsz, t_len, n_heads, kv_dim))
        outs[1].append(v.reshape(bsz, t_len, n_heads, kv_dim))
        outs[4].append(hp[:, SUBLANES - 1])
        outs[6].append(xr.reshape(bsz, t_len, d)[:, t_len - (cw_w - 1):])
        outs[8].append(fp[:, SUBLANES - (fw_w - 1):])

        xr, q2, k, kb, v, vb, ga, gb = _inproj(xs, norm_attn[l], w_in_bf, head_dim)
        ya, hs = branch_a(xr, _pad_state(state_rglru_conv[l]), state_rglru_h[l], dbsz, True, past)
        yb = _sample_attn(page_table, q2.astype(F32), cache_k[l], cache_v[l], k, v, bias_l, bias_n,
                          lams, subln_g[l], n_seq=dbsz, n_heads=n_heads, lambda_init=lambda_init)
        xs, fs = channel_mix(xs, ya, yb, ga, gb, _pad_state(state_ffn_conv[l]), dbsz, True)
        outs[2].append(k.reshape(dbsz, dec_seq, n_heads, kv_dim))
        outs[3].append(v.reshape(dbsz, dec_seq, n_heads, kv_dim))
        outs[5].append(hs[:, SUBLANES - 1])
        outs[7].append(xr.reshape(dbsz, dec_seq, d)[:, dec_seq - (cw_w - 1):])
        outs[9].append(fs[:, SUBLANES - (fw_w - 1):])

    return (xp.reshape(bsz, t_len, d), xs.reshape(dbsz, dec_seq, d),
            jnp.stack(outs[0]), jnp.stack(outs[1]), jnp.stack(outs[2]), jnp.stack(outs[3]),
            jnp.stack(outs[4]), jnp.stack(outs[5]), jnp.stack(outs[6]), jnp.stack(outs[7]),
            jnp.stack(outs[8]), jnp.stack(outs[9]))
```

```python
import functools
import math

import jax
import jax.numpy as jnp
from jax import lax
from jax.experimental import pallas as pl
from jax.experimental.pallas import tpu as pltpu

F32 = jnp.float32
BF16 = jnp.bfloat16

EPS = 1e-6
RG_C = 8.0
N_BUCKETS = 32
MAX_DISTANCE = 128
NEG = -1e30
LOG2E = math.log2(math.e)

SUBLANES = 8
LANES = 128
ROW_TILE = 256
ATTN_BLOCK = 512
FFN_CHUNK = 512
PAGES_PER_STEP = 8
VMEM_LIMIT = 52 * 1024 * 1024


def _params(n_axes, vmem=VMEM_LIMIT):
    return pltpu.CompilerParams(
        dimension_semantics=("arbitrary",) * n_axes, vmem_limit_bytes=vmem)


def _resident(shape):
    nd = len(shape)
    return pl.BlockSpec(shape, lambda *_: (0,) * nd, pipeline_mode=pl.Buffered(1))


def _sigmoid(x):
    return 1.0 / (1.0 + jnp.exp(-x))


def _rms(x, g):
    return x * lax.rsqrt(jnp.mean(x * x, axis=-1, keepdims=True) + EPS) * g


def _inproj_kernel(x_ref, g_ref, w_ref, xr_ref, q2_ref, k_ref, kb_ref, v_ref, vt_ref,
                   ga_ref, gb_ref, *, d, head_dim):
    h = _rms(x_ref[...], g_ref[...]).astype(BF16)

    def proj(j):
        return jnp.dot(h, w_ref[:, j * d:(j + 1) * d], preferred_element_type=F32)

    xr_ref[...] = proj(0)
    q = proj(1) * (head_dim ** -0.5 * LOG2E)
    lane = lax.broadcasted_iota(jnp.int32, q.shape, 1)
    first = (lane % (2 * head_dim)) < head_dim
    q2_ref[0] = jnp.where(first, q, 0.0).astype(BF16)
    q2_ref[1] = jnp.where(first, 0.0, q).astype(BF16)
    k = proj(2)
    k_ref[...] = k
    kb_ref[...] = k.astype(BF16)
    v = proj(3)
    v_ref[...] = v
    vt_ref[...] = v.T.astype(BF16)
    ga_ref[...] = _sigmoid(proj(4))
    gb_ref[...] = _sigmoid(proj(5))


def _inproj(x2d, g, w_bf, head_dim):
    n, d = x2d.shape
    tm = min(ROW_TILE, n)
    row = lambda i: (i, 0)
    blk = pl.BlockSpec((tm, d), row)
    f32 = jax.ShapeDtypeStruct((n, d), F32)
    return pl.pallas_call(
        functools.partial(_inproj_kernel, d=d, head_dim=head_dim),
        grid=(n // tm,),
        in_specs=[blk, _resident((1, d)), _resident(w_bf.shape)],
        out_specs=[blk, pl.BlockSpec((2, tm, d), lambda i: (0, i, 0)), blk, blk, blk,
                   pl.BlockSpec((d, tm), lambda i: (0, i)), blk, blk],
        out_shape=[f32, jax.ShapeDtypeStruct((2, n, d), BF16), f32,
                   jax.ShapeDtypeStruct((n, d), BF16), f32,
                   jax.ShapeDtypeStruct((d, n), BF16), f32, f32],
        compiler_params=_params(1),
        name="inproj",
    )(x2d, g.reshape(1, d), w_bf)


def _shifted(x3, prev3, d, t_idx):
    return jnp.where(t_idx >= d, pltpu.roll(x3, d, 1), pltpu.roll(prev3, d, 1))


def _causal_conv(x3, prev3, w, b):
    width = w.shape[0]
    t_idx = lax.broadcasted_iota(jnp.int32, x3.shape, 1)
    y = b + w[width - 1:width] * x3
    for dd in range(1, width):
        y = y + w[width - 1 - dd:width - dd] * _shifted(x3, prev3, dd, t_idx)
    return y


def _rglru_kernel(xr_ref, spad_ref, h0_ref, cw_ref, cb_ref, wa_ref, ba_ref, wx_ref, bx_ref,
                  ap_ref, ya_ref, hlast_ref, carry_x, carry_h, a_scr, b_scr,
                  *, group_mode, pos_base, n_blocks):
    ti = pl.program_id(1)
    tt, c = xr_ref.shape
    g = tt // SUBLANES
    blk = c // n_blocks
    x3 = xr_ref[...].reshape(g, SUBLANES, c)

    if group_mode:
        prev3 = spad_ref[...]
    else:
        @pl.when(ti == 0)
        def _():
            carry_x[...] = spad_ref[0]
            carry_h[...] = h0_ref[0]
        prev3 = jnp.concatenate([carry_x[...][None], x3[:g - 1]], axis=0) if g > 1 else carry_x[...][None]

    xc3 = _causal_conv(x3, prev3, cw_ref[...], cb_ref[...])
    if not group_mode:
        carry_x[...] = x3[g - 1]

    xc = xc3.reshape(tt, c)
    xcb = xc.astype(BF16)

    def gate(w_ref, b_ref):
        parts = [jnp.dot(xcb[:, n * blk:(n + 1) * blk], w_ref[n], preferred_element_type=F32)
                 for n in range(n_blocks)]
        return jnp.concatenate(parts, axis=1) + b_ref[...]

    r = _sigmoid(gate(wa_ref, ba_ref))
    i = _sigmoid(gate(wx_ref, bx_ref))
    z = -ap_ref[...]
    softplus = jnp.maximum(z, 0.0) + jnp.log(1.0 + jnp.exp(-jnp.abs(z)))
    log_a = -RG_C * r * softplus
    a = jnp.exp(log_a)
    th = jnp.tanh(log_a)
    mult = jnp.sqrt(-2.0 * th / (1.0 - th))
    row = lax.broadcasted_iota(jnp.int32, (tt, c), 0)
    pos = pos_base + ((row % SUBLANES) if group_mode else (ti * tt + row))
    reset = pos == 0
    a = jnp.where(reset, 0.0, a)
    mult = jnp.where(reset, 1.0, mult)
    b = mult * i * xc

    a3 = a.reshape(g, SUBLANES, c)
    b3 = b.reshape(g, SUBLANES, c)
    t_idx = lax.broadcasted_iota(jnp.int32, a3.shape, 1)
    for dd in (1, 2, 4):
        valid = t_idx >= dd
        b3 = jnp.where(valid, a3 * pltpu.roll(b3, dd, 1) + b3, b3)
        a3 = jnp.where(valid, a3 * pltpu.roll(a3, dd, 1), a3)

    if group_mode:
        h3 = a3 * h0_ref[...] + b3
        ya_ref[...] = h3.reshape(tt, c)
        hlast_ref[...] = h3
    else:
        a_scr[...] = a3
        b_scr[...] = b3

        def body(gi, carry):
            hg = a_scr[gi] * carry + b_scr[gi]
            ya_ref[pl.ds(pl.multiple_of(gi * SUBLANES, SUBLANES), SUBLANES), :] = hg
            return jnp.broadcast_to(hg[SUBLANES - 1:SUBLANES, :], (SUBLANES, c))

        last = lax.fori_loop(0, g, body, carry_h[...])
        carry_h[...] = last
        hlast_ref[0] = last


def _rglru(xr2d, spad, h0b, cw, cb, wa_bf, ba, wx_bf, bx, ap, *, n_seq, group_mode, pos_base):
    n, c = xr2d.shape
    n_blocks = wa_bf.shape[0]
    if group_mode:
        tt, grid = n, (1, 1)
        row_spec = pl.BlockSpec((n, c), lambda b, t: (0, 0))
        seq_spec = pl.BlockSpec((n_seq, SUBLANES, c), lambda b, t: (0, 0, 0))
    else:
        t_len = n // n_seq
        tt = min(ROW_TILE, t_len)
        nt = t_len // tt
        grid = (n_seq, nt)
        row_spec = pl.BlockSpec((tt, c), lambda b, t: (b * nt + t, 0))
        seq_spec = pl.BlockSpec((1, SUBLANES, c), lambda b, t: (b, 0, 0))
    g = tt // SUBLANES
    vec = lambda a: a.reshape(1, c)
    return pl.pallas_call(
        functools.partial(_rglru_kernel, group_mode=group_mode, pos_base=pos_base, n_blocks=n_blocks),
        grid=grid,
        in_specs=[row_spec, seq_spec, seq_spec, _resident(cw.shape), _resident((1, c)),
                  _resident(wa_bf.shape), _resident((1, c)), _resident(wx_bf.shape), _resident((1, c)),
                  _resident((1, c))],
        out_specs=[row_spec, seq_spec],
        out_shape=[jax.ShapeDtypeStruct((n, c), F32), jax.ShapeDtypeStruct((n_seq, SUBLANES, c), F32)],
        scratch_shapes=[pltpu.VMEM((SUBLANES, c), F32), pltpu.VMEM((SUBLANES, c), F32),
                        pltpu.VMEM((g, SUBLANES, c), F32), pltpu.VMEM((g, SUBLANES, c), F32)],
        compiler_params=_params(2),
        name="rglru",
    )(xr2d, spad, h0b, cw, vec(cb), wa_bf, vec(ba), wx_bf, vec(bx), vec(ap))


def _bias_of(dist, valid, rb_ref, h):
    n = jnp.maximum(dist, 0)
    max_exact = N_BUCKETS // 2
    nf = jnp.maximum(n, 1).astype(F32)
    large = max_exact + (jnp.log(nf / max_exact) / math.log(MAX_DISTANCE / max_exact)
                         * (N_BUCKETS - max_exact)).astype(jnp.int32)
    large = jnp.minimum(large, N_BUCKETS - 1)
    bucket = jnp.where(n < max_exact, n, large)
    base = rb_ref[N_BUCKETS - 1, h]
    out = jnp.zeros(dist.shape, F32)
    for b in range(N_BUCKETS - 1):
        out = jnp.where(bucket == b, (rb_ref[b, h] - base) * LOG2E, out)
    return jnp.where(valid, out, NEG)


def _bias_kernel(rb_ref, bp_ref, mask_ref, blp_ref, bn_ref, *, tb, page, n_heads, dec_seq):
    h = pl.program_id(0)
    j = lax.broadcasted_iota(jnp.int32, (2 * tb, tb), 0)
    i = lax.broadcasted_iota(jnp.int32, (2 * tb, tb), 1)
    dist = i + tb - j
    bp_ref[0] = _bias_of(dist, dist >= 0, rb_ref, h)
    rows, width = mask_ref.shape
    t = lax.broadcasted_iota(jnp.int32, (rows, width), 0) % dec_seq
    lane = lax.broadcasted_iota(jnp.int32, (rows, width), 1)
    own = (lane % n_heads) == h
    mask_ref[...] = jnp.where(own, 0.0, NEG)
    blp_ref[...] = _bias_of(page + t - lane // n_heads, own, rb_ref, h)
    t = lax.broadcasted_iota(jnp.int32, (rows, LANES), 0) % dec_seq
    lane = lax.broadcasted_iota(jnp.int32, (rows, LANES), 1)
    dist = t - lane // n_heads
    bn_ref[...] = _bias_of(dist, jnp.logical_and((lane % n_heads) == h, dist >= 0), rb_ref, h)


def _bias_tiles(rel_bias, n_heads, tb, page, dec_seq):
    rows = 2 * dec_seq
    width = page * n_heads
    return pl.pallas_call(
        functools.partial(_bias_kernel, tb=tb, page=page, n_heads=n_heads, dec_seq=dec_seq),
        grid=(n_heads,),
        in_specs=[pl.BlockSpec(memory_space=pltpu.SMEM)],
        out_specs=[pl.BlockSpec((1, 2 * tb, tb), lambda h: (h, 0, 0)),
                   pl.BlockSpec((rows, width), lambda h: (h, 0)),
                   pl.BlockSpec((rows, width), lambda h: (h, 0)),
                   pl.BlockSpec((rows, LANES), lambda h: (h, 0))],
        out_shape=[jax.ShapeDtypeStruct((n_heads, 2 * tb, tb), F32),
                   jax.ShapeDtypeStruct((n_heads * rows, width), F32),
                   jax.ShapeDtypeStruct((n_heads * rows, width), F32),
                   jax.ShapeDtypeStruct((n_heads * rows, LANES), F32)],
        compiler_params=_params(1),
        name="bias_tiles",
    )(rel_bias)


def _lambda(lq1, lk1, lq2, lk2, lambda_init):
    s1 = jnp.sum(lq1 * lk1, axis=-1, keepdims=True)
    s2 = jnp.sum(lq2 * lk2, axis=-1, keepdims=True)
    return jnp.exp(s1) - jnp.exp(s2) + lambda_init


def _qk(a, b):
    return lax.dot_general(a, b, (((1,), (1,)), ((), ())), preferred_element_type=F32)


def _all_sublanes(x, op):
    for dd in (1, 2, 4):
        x = op(x, pltpu.roll(x, dd, 0))
    return x


def _prompt_attn_kernel(q2_ref, k_ref, vt_ref, bias_ref, lq1_ref, lk1_ref, lq2_ref, lk2_ref, sg_ref,
                        o_ref, s0_ref, s1_ref, m0_ref, m1_ref, l0_ref, l1_ref, a0_ref, a1_ref,
                        *, tb, lambda_init):
    qi = pl.program_id(2)
    e = vt_ref.shape[0]
    s_refs = (s0_ref, s1_ref)
    m_refs = (m0_ref, m1_ref)
    l_refs = (l0_ref, l1_ref)
    acc_refs = (a0_ref, a1_ref)
    for mm in range(2):
        m_refs[mm][...] = jnp.full(m_refs[mm].shape, NEG, F32)
        l_refs[mm][...] = jnp.zeros(l_refs[mm].shape, F32)
        acc_refs[mm][...] = jnp.zeros(acc_refs[mm].shape, F32)

    def scores(mm, kj):
        start = pl.multiple_of(kj * tb, tb)
        return _qk(k_ref[pl.ds(start, tb), :], q2_ref[mm])

    def consume(mm, kj, bias):
        s = s_refs[mm][...]
        if bias is not None:
            s = s + bias
        s3 = s.reshape(tb // SUBLANES, SUBLANES, tb)
        m_prev = m_refs[mm][...]
        m_new = jnp.maximum(m_prev, _all_sublanes(jnp.max(s3, axis=0), jnp.maximum))
        alpha = jnp.exp2(m_prev - m_new)
        p3 = jnp.exp2(s3 - m_new[None])
        l_refs[mm][...] = alpha * l_refs[mm][...] + _all_sublanes(jnp.sum(p3, axis=0), jnp.add)
        start = pl.multiple_of(kj * tb, tb)
        pv = jnp.dot(vt_ref[:, pl.ds(start, tb)], p3.reshape(tb, tb).astype(BF16),
                     preferred_element_type=F32)
        acc = acc_refs[mm][...].reshape(e // SUBLANES, SUBLANES, tb) * alpha[None]
        acc_refs[mm][...] = acc.reshape(e, tb) + pv
        m_refs[mm][...] = m_new

    s0_ref[...] = scores(0, 0)

    def far(j, carry):
        s1_ref[...] = scores(1, j)
        consume(0, j, None)
        s0_ref[...] = scores(0, j + 1)
        consume(1, j, None)
        return carry

    lax.fori_loop(0, jnp.maximum(qi - 1, 0), far, 0)

    @pl.when(qi > 0)
    def _():
        bias = bias_ref[0, :tb, :]
        s1_ref[...] = scores(1, qi - 1)
        consume(0, qi - 1, bias)
        s0_ref[...] = scores(0, qi)
        consume(1, qi - 1, bias)

    bias = bias_ref[0, tb:, :]
    s1_ref[...] = scores(1, qi)
    consume(0, qi, bias)
    consume(1, qi, bias)

    lam = _lambda(lq1_ref[...], lk1_ref[...], lq2_ref[...], lk2_ref[...], lambda_init)

    def normalised(mm):
        acc = acc_refs[mm][...].reshape(e // SUBLANES, SUBLANES, tb)
        return (acc / l_refs[mm][...][None]).reshape(e, tb)

    o = (normalised(0) - lam * normalised(1)).T
    o_ref[...] = _rms(o, sg_ref[...]) * (1.0 - lambda_init)


def _prompt_attn(q2, kb, vt, bias_p, lams, subln_g, *, n_seq, n_heads, lambda_init):
    _, n, d = q2.shape
    e = d // n_heads
    t_len = n // n_seq
    tb = ATTN_BLOCK
    nq = t_len // tb
    hd = lams[0].shape[-1]
    small = pl.BlockSpec((1, hd), lambda b, h, i: (0, 0))
    stat = pltpu.VMEM((SUBLANES, tb), F32)
    return pl.pallas_call(
        functools.partial(_prompt_attn_kernel, tb=tb, lambda_init=lambda_init),
        grid=(n_seq, n_heads, nq),
        in_specs=[pl.BlockSpec((2, tb, e), lambda b, h, i: (0, b * nq + i, h)),
                  pl.BlockSpec((t_len, e), lambda b, h, i: (b, h)),
                  pl.BlockSpec((e, t_len), lambda b, h, i: (h, b)),
                  pl.BlockSpec((1, 2 * tb, tb), lambda b, h, i: (h, 0, 0)),
                  small, small, small, small,
                  pl.BlockSpec((1, e), lambda b, h, i: (0, 0))],
        out_specs=pl.BlockSpec((tb, e), lambda b, h, i: (b * nq + i, h)),
        out_shape=jax.ShapeDtypeStruct((n, d), F32),
        scratch_shapes=[pltpu.VMEM((tb, tb), F32), pltpu.VMEM((tb, tb), F32),
                        stat, stat, stat, stat,
                        pltpu.VMEM((e, tb), F32), pltpu.VMEM((e, tb), F32)],
        compiler_params=_params(3),
        name="prompt_attn",
    )(q2, kb, vt, bias_p, *[x.reshape(1, hd) for x in lams], subln_g.reshape(1, e))


def _online_update(s_parts, v_parts, m_ref, l_ref, acc_ref):
    m_prev = m_ref[...]
    m_new = m_prev
    for s in s_parts:
        m_new = jnp.maximum(m_new, jnp.max(s, axis=-1, keepdims=True))
    alpha = jnp.exp2(m_prev - m_new)
    l_new = alpha * l_ref[...]
    acc = alpha * acc_ref[...]
    for s, v in zip(s_parts, v_parts):
        p = jnp.exp2(s - m_new[:, :1])
        l_new = l_new + jnp.sum(p, axis=-1, keepdims=True)
        acc = acc + jnp.dot(p, v, preferred_element_type=F32)
    l_ref[...] = l_new
    acc_ref[...] = acc
    m_ref[...] = m_new


def _sample_attn_kernel(pt_ref, q2_ref, *refs, n_heads, dec_seq, pps, lambda_init):
    kp = refs[:pps]
    vp = refs[pps:2 * pps]
    (kn_ref, vn_ref, mask_ref, blp_ref, bn_ref, lq1_ref, lk1_ref, lq2_ref, lk2_ref, sg_ref,
     o_ref, q_ref, m_ref, l_ref, acc_ref) = refs[2 * pps:]
    del pt_ref
    c = pl.program_id(1)
    nc = pl.num_programs(1)
    e = kp[0].shape[1]
    rows = 2 * dec_seq

    @pl.when(c == 0)
    def _():
        pieces = [q2_ref[mm][:, h * e:(h + 1) * e] for h in range(n_heads) for mm in range(2)]
        q_ref[...] = jnp.concatenate(pieces, axis=0)
        m_ref[...] = jnp.full(m_ref.shape, NEG, F32)
        l_ref[...] = jnp.zeros(l_ref.shape, F32)
        acc_ref[...] = jnp.zeros(acc_ref.shape, F32)

    q = q_ref[...]
    mask = mask_ref[...]
    is_last = (c == nc - 1).astype(F32)
    s_parts = []
    for j in range(pps):
        bias = mask + (blp_ref[...] - mask) * is_last if j == pps - 1 else mask
        s_parts.append(_qk(q, kp[j][...]) + bias)
    _online_update(s_parts, [vp[j][...] for j in range(pps)], m_ref, l_ref, acc_ref)

    @pl.when(c == nc - 1)
    def _():
        _online_update([_qk(q, kn_ref[...]) + bn_ref[...]], [vn_ref[...]], m_ref, l_ref, acc_ref)
        lam = _lambda(lq1_ref[...], lk1_ref[...], lq2_ref[...], lk2_ref[...], lambda_init)
        o_all = acc_ref[...] / l_ref[...]
        outs = []
        for h in range(n_heads):
            r0 = h * rows
            o = o_all[r0:r0 + dec_seq] - lam * o_all[r0 + dec_seq:r0 + rows]
            outs.append(_rms(o, sg_ref[...]) * (1.0 - lambda_init))
        o_ref[...] = jnp.concatenate(outs, axis=1)


def _sample_attn(page_table, q2f, cache_k, cache_v, kn_rows, vn_rows, mask, bias_lp, bias_n, lams,
                 subln_g, *, n_seq, n_heads, lambda_init):
    _, n, d = q2f.shape
    dec_seq = n // n_seq
    e = d // n_heads
    n_pool, page = cache_k.shape[:2]
    n_pages = page_table.shape[1]
    pps = PAGES_PER_STEP
    nc = n_pages // pps
    hd = lams[0].shape[-1]
    rows = n_heads * 2 * dec_seq
    width = page * n_heads

    def page_spec(j):
        return pl.BlockSpec((None, width, e), lambda b, c, pt: (pt[b, c * pps + j], 0, 0))

    const2 = lambda b, c, pt: (0, 0)
    small = pl.BlockSpec((1, hd), const2)
    new_rows = pl.BlockSpec((LANES, e), lambda b, c, pt: (b, 0))
    grid_spec = pltpu.PrefetchScalarGridSpec(
        num_scalar_prefetch=1,
        grid=(n_seq, nc),
        in_specs=[pl.BlockSpec((2, dec_seq, d), lambda b, c, pt: (0, b, 0))]
                 + [page_spec(j) for j in range(pps)] + [page_spec(j) for j in range(pps)]
                 + [new_rows, new_rows,
                    pl.BlockSpec((rows, width), const2), pl.BlockSpec((rows, width), const2),
                    pl.BlockSpec((rows, LANES), const2),
                    small, small, small, small, pl.BlockSpec((1, e), const2)],
        out_specs=pl.BlockSpec((dec_seq, d), lambda b, c, pt: (b, 0)),
        scratch_shapes=[pltpu.VMEM((rows, e), F32), pltpu.VMEM((rows, LANES), F32),
                        pltpu.VMEM((rows, LANES), F32), pltpu.VMEM((rows, e), F32)],
    )
    ck = cache_k.reshape(n_pool, width, e)
    cv = cache_v.reshape(n_pool, width, e)
    return pl.pallas_call(
        functools.partial(_sample_attn_kernel, n_heads=n_heads, dec_seq=dec_seq, pps=pps,
                          lambda_init=lambda_init),
        grid_spec=grid_spec,
        out_shape=jax.ShapeDtypeStruct((n, d), F32),
        compiler_params=_params(2),
        name="sample_attn",
    )(page_table, q2f, *([ck] * pps), *([cv] * pps), kn_rows, vn_rows, mask, bias_lp, bias_n,
      *[x.reshape(1, hd) for x in lams], subln_g.reshape(1, e))


def _new_token_rows(x2d, n_seq, n_heads):
    n, d = x2d.shape
    e = d // n_heads
    per_seq = (n // n_seq) * n_heads
    x = x2d.reshape(n_seq, per_seq, e)
    x = jnp.pad(x, ((0, 0), (0, LANES - per_seq), (0, 0)))
    return x.reshape(n_seq * LANES, e)


def _gelu_tanh(x):
    return 0.5 * x * (1.0 + jnp.tanh(math.sqrt(2.0 / math.pi) * (x + 0.044715 * (x * x * x))))


def _ffn_kernel(x_ref, ya_ref, yb_ref, ga_ref, gb_ref, spad_ref, wout_ref, nffn_ref, wup_ref,
                cw_ref, cb_ref, wdown_ref, nfin_ref, y_ref, tail_ref, carry_ref,
                *, group_mode, final_norm):
    ti = pl.program_id(1)
    tm, d = x_ref.shape
    g = tm // SUBLANES
    d_ff = wdown_ref.shape[0]

    merged = (ga_ref[...] * ya_ref[...] + gb_ref[...] * yb_ref[...]).astype(BF16)
    x2 = x_ref[...] + jnp.dot(merged, wout_ref[...], preferred_element_type=F32)
    h2 = _rms(x2, nffn_ref[...]).astype(BF16)

    if not group_mode:
        @pl.when(ti == 0)
        def _():
            carry_ref[...] = spad_ref[0]

    def up_conv(cols):
        up3 = jnp.dot(h2, wup_ref[:, cols], preferred_element_type=F32).reshape(g, SUBLANES, -1)
        if group_mode:
            prev3 = spad_ref[:, :, cols]
            tail_ref[:, :, cols] = up3
        else:
            first = carry_ref[:, cols][None]
            prev3 = jnp.concatenate([first, up3[:g - 1]], axis=0) if g > 1 else first
            carry_ref[:, cols] = up3[g - 1]
            tail_ref[0, :, cols] = up3[g - 1]
        return _causal_conv(up3, prev3, cw_ref[:, cols], cb_ref[:, cols]).reshape(tm, -1)

    down = jnp.zeros((tm, d), F32)
    for c0 in range(0, d_ff, FFN_CHUNK):
        gate = up_conv(slice(c0, c0 + FFN_CHUNK))
        val = up_conv(slice(d_ff + c0, d_ff + c0 + FFN_CHUNK))
        act = (_gelu_tanh(gate) * val).astype(BF16)
        down = down + jnp.dot(act, wdown_ref[c0:c0 + FFN_CHUNK, :], preferred_element_type=F32)
    x3 = x2 + down
    y_ref[...] = _rms(x3, nfin_ref[...]) if final_norm else x3


def _ffn(x2d, ya, yb, ga, gb, spad, wout_bf, nffn, wup_bf, cw, cb, wdown_bf, nfin,
         *, n_seq, group_mode, final_norm):
    n, d = x2d.shape
    f2 = wup_bf.shape[1]
    if group_mode:
        tm, grid = n, (1, 1)
        row = pl.BlockSpec((n, d), lambda b, t: (0, 0))
        seq = pl.BlockSpec((n_seq, SUBLANES, f2), lambda b, t: (0, 0, 0))
    else:
        t_len = n // n_seq
        tm = min(ROW_TILE, t_len)
        nt = t_len // tm
        grid = (n_seq, nt)
        row = pl.BlockSpec((tm, d), lambda b, t: (b * nt + t, 0))
        seq = pl.BlockSpec((1, SUBLANES, f2), lambda b, t: (b, 0, 0))
    return pl.pallas_call(
        functools.partial(_ffn_kernel, group_mode=group_mode, final_norm=final_norm),
        grid=grid,
        in_specs=[row, row, row, row, row, seq, _resident(wout_bf.shape), _resident((1, d)),
                  _resident(wup_bf.shape), _resident(cw.shape), _resident((1, f2)),
                  _resident(wdown_bf.shape), _resident((1, d))],
        out_specs=[row, seq],
        out_shape=[jax.ShapeDtypeStruct((n, d), F32), jax.ShapeDtypeStruct((n_seq, SUBLANES, f2), F32)],
        scratch_shapes=[pltpu.VMEM((SUBLANES, f2), F32)],
        compiler_params=_params(2),
        name="merge_ffn",
    )(x2d, ya, yb, ga, gb, spad, wout_bf, nffn.reshape(1, d), wup_bf, cw, cb.reshape(1, f2),
      wdown_bf, nfin.reshape(1, d))


def _pad_state(buf):
    b, w1, c = buf.shape
    return jnp.concatenate([jnp.zeros((b, SUBLANES - w1, c), buf.dtype), buf], axis=1)


def kernel(x_prompt, x_sample, cache_k, cache_v, page_table, state_rglru_h, state_rglru_conv, state_ffn_conv, w_in, conv_w, conv_b, rg_w_a, rg_b_a, rg_w_x, rg_b_x, rg_a_param, lambda_q1, lambda_k1, lambda_q2, lambda_k2, subln_g, rel_bias, w_out, norm_attn, norm_ffn, w_up, ffn_conv_w, ffn_conv_b, w_down, norm_final):
    bsz, t_len, d = x_prompt.shape
    dbsz, dec_seq, _ = x_sample.shape
    depth = w_in.shape[0]
    n_heads = cache_k.shape[3]
    kv_dim = cache_k.shape[4]
    head_dim = kv_dim // 2
    page = cache_k.shape[2]
    past = page_table.shape[1] * page
    cw_w = conv_w.shape[1]
    fw_w = ffn_conv_w.shape[1]
    assert w_in.shape[2] == 6 * d and n_heads * kv_dim == d and rg_a_param.shape[1] == d
    assert dec_seq == SUBLANES and t_len % ATTN_BLOCK == 0 and page_table.shape[1] % PAGES_PER_STEP == 0
    assert ATTN_BLOCK >= MAX_DISTANCE and page >= MAX_DISTANCE and dec_seq * n_heads <= LANES
    assert kv_dim == LANES and n_heads == SUBLANES

    bias_p, mask_s, bias_lp, bias_n = _bias_tiles(rel_bias, n_heads, ATTN_BLOCK, page, dec_seq)

    xp = x_prompt.reshape(bsz * t_len, d)
    xs = x_sample.reshape(dbsz * dec_seq, d)
    outs = [[] for _ in range(10)]
    for l in range(depth):
        lambda_init = 0.8 - 0.6 * math.exp(-0.3 * l)
        last = l == depth - 1
        w_in_bf = w_in[l].astype(BF16)
        wa_bf = rg_w_a[l].astype(BF16)
        wx_bf = rg_w_x[l].astype(BF16)
        wout_bf = w_out[l].astype(BF16)
        wup_bf = w_up[l].astype(BF16)
        wdown_bf = w_down[l].astype(BF16)
        lams = (lambda_q1[l], lambda_k1[l], lambda_q2[l], lambda_k2[l])

        def branch_a(xr, spad, h0, n_seq, group_mode, pos_base):
            h0b = jnp.broadcast_to(h0[:, None, :], (n_seq, SUBLANES, d))
            return _rglru(xr, spad, h0b, conv_w[l], conv_b[l], wa_bf, rg_b_a[l], wx_bf, rg_b_x[l],
                          rg_a_param[l], n_seq=n_seq, group_mode=group_mode, pos_base=pos_base)

        def channel_mix(x2d, ya, yb, ga, gb, spad, n_seq, group_mode):
            return _ffn(x2d, ya, yb, ga, gb, spad, wout_bf, norm_ffn[l], wup_bf, ffn_conv_w[l],
                        ffn_conv_b[l], wdown_bf, norm_final, n_seq=n_seq, group_mode=group_mode,
                        final_norm=last)

        xr, q2, k, kb, v, vt, ga, gb = _inproj(xp, norm_attn[l], w_in_bf, head_dim)
        ya, hp = branch_a(xr, jnp.zeros((bsz, SUBLANES, d), F32), jnp.zeros((bsz, d), F32),
                          bsz, False, 0)
        yb = _prompt_attn(q2, kb, vt, bias_p, lams, subln_g[l], n_seq=bsz, n_heads=n_heads,
                          lambda_init=lambda_init)
        xp, fp = channel_mix(xp, ya, yb, ga, gb, jnp.zeros((bsz, SUBLANES, ffn_conv_w.shape[2]), F32),
                             bsz, False)
        outs[0].append(k.reshape(bsz, t_len, n_heads, kv_dim))
        outs[1].append(v.reshape(bsz, t_len, n_heads, kv_dim))
        outs[4].append(hp[:, SUBLANES - 1])
        outs[6].append(xr.reshape(bsz, t_len, d)[:, t_len - (cw_w - 1):])
        outs[8].append(fp[:, SUBLANES - (fw_w - 1):])

        xr, q2, k, _, v, _, ga, gb = _inproj(xs, norm_attn[l], w_in_bf, head_dim)
        ya, hs = branch_a(xr, _pad_state(state_rglru_conv[l]), state_rglru_h[l], dbsz, True, past)
        yb = _sample_attn(page_table, q2.astype(F32), cache_k[l], cache_v[l],
                          _new_token_rows(k, dbsz, n_heads), _new_token_rows(v, dbsz, n_heads),
                          mask_s, bias_lp, bias_n, lams, subln_g[l], n_seq=dbsz, n_heads=n_heads,
                          lambda_init=lambda_init)
        xs, fs = channel_mix(xs, ya, yb, ga, gb, _pad_state(state_ffn_conv[l]), dbsz, True)
        outs[2].append(k.reshape(dbsz, dec_seq, n_heads, kv_dim))
        outs[3].append(v.reshape(dbsz, dec_seq, n_heads, kv_dim))
        outs[5].append(hs[:, SUBLANES - 1])
        outs[7].append(xr.reshape(dbsz, dec_seq, d)[:, dec_seq - (cw_w - 1):])
        outs[9].append(fs[:, SUBLANES - (fw_w - 1):])

    return (xp.reshape(bsz, t_len, d), xs.reshape(dbsz, dec_seq, d),
            jnp.stack(outs[0]), jnp.stack(outs[1]), jnp.stack(outs[2]), jnp.stack(outs[3]),
            jnp.stack(outs[4]), jnp.stack(outs[5]), jnp.stack(outs[6]), jnp.stack(outs[7]),
            jnp.stack(outs[8]), jnp.stack(outs[9]))
```

```python
import functools
import math

import jax
import jax.numpy as jnp
from jax import lax
from jax.experimental import pallas as pl
from jax.experimental.pallas import tpu as pltpu

F32 = jnp.float32
BF16 = jnp.bfloat16

EPS = 1e-6
RG_C = 8.0
N_BUCKETS = 32
MAX_DISTANCE = 128
NEG = -1e30
LOG2E = math.log2(math.e)

SUBLANES = 8
LANES = 128
ROW_TILE = 256
FFN_ROW_TILE = 512
ATTN_BLOCK = 512
FFN_CHUNK = 512
PAGES_PER_STEP = 8
VMEM_LIMIT = 52 * 1024 * 1024


def _params(n_axes, vmem=VMEM_LIMIT):
    return pltpu.CompilerParams(
        dimension_semantics=("arbitrary",) * n_axes, vmem_limit_bytes=vmem)


def _resident(shape):
    nd = len(shape)
    return pl.BlockSpec(shape, lambda *_: (0,) * nd, pipeline_mode=pl.Buffered(1))


def _sigmoid(x):
    return 0.5 * (1.0 + jnp.tanh(0.5 * x))


def _rms(x, g):
    return x * lax.rsqrt(jnp.mean(x * x, axis=-1, keepdims=True) + EPS) * g


def _inproj_kernel(x_ref, g_ref, w_ref, xr_ref, q2_ref, k_ref, kb_ref, v_ref, vt_ref,
                   ga_ref, gb_ref, *, d, head_dim):
    h = _rms(x_ref[...], g_ref[...]).astype(BF16)

    def proj(j):
        return jnp.dot(h, w_ref[:, j * d:(j + 1) * d], preferred_element_type=F32)

    xr_ref[...] = proj(0)
    q = proj(1) * (head_dim ** -0.5 * LOG2E)
    lane = lax.broadcasted_iota(jnp.int32, q.shape, 1)
    first = (lane % (2 * head_dim)) < head_dim
    q2_ref[0] = jnp.where(first, q, 0.0).astype(BF16)
    q2_ref[1] = jnp.where(first, 0.0, q).astype(BF16)
    k = proj(2)
    k_ref[...] = k
    kb_ref[...] = k.astype(BF16)
    v = proj(3)
    v_ref[...] = v
    vt_ref[...] = v.T.astype(BF16)
    ga_ref[...] = _sigmoid(proj(4))
    gb_ref[...] = _sigmoid(proj(5))


def _inproj(x2d, g, w_bf, head_dim):
    n, d = x2d.shape
    tm = min(ROW_TILE, n)
    row = lambda i: (i, 0)
    blk = pl.BlockSpec((tm, d), row)
    f32 = jax.ShapeDtypeStruct((n, d), F32)
    return pl.pallas_call(
        functools.partial(_inproj_kernel, d=d, head_dim=head_dim),
        grid=(n // tm,),
        in_specs=[blk, _resident((1, d)), _resident(w_bf.shape)],
        out_specs=[blk, pl.BlockSpec((2, tm, d), lambda i: (0, i, 0)), blk, blk, blk,
                   pl.BlockSpec((d, tm), lambda i: (0, i)), blk, blk],
        out_shape=[f32, jax.ShapeDtypeStruct((2, n, d), BF16), f32,
                   jax.ShapeDtypeStruct((n, d), BF16), f32,
                   jax.ShapeDtypeStruct((d, n), BF16), f32, f32],
        compiler_params=_params(1),
        name="inproj",
    )(x2d, g.reshape(1, d), w_bf)


def _shifted(x3, prev3, d, t_idx):
    return jnp.where(t_idx >= d, pltpu.roll(x3, d, 1), pltpu.roll(prev3, d, 1))


def _causal_conv(x3, prev3, w, b):
    width = w.shape[0]
    t_idx = lax.broadcasted_iota(jnp.int32, x3.shape, 1)
    y = b + w[width - 1:width] * x3
    for dd in range(1, width):
        y = y + w[width - 1 - dd:width - dd] * _shifted(x3, prev3, dd, t_idx)
    return y


def _rglru_kernel(xr_ref, ga_ref, spad_ref, h0_ref, cw_ref, cb_ref, wa_ref, ba_ref, wx_ref, bx_ref,
                  ap_ref, za_ref, hlast_ref, carry_x, carry_h, a_scr, b_scr,
                  *, group_mode, pos_base, n_blocks):
    ti = pl.program_id(1)
    tt, c = xr_ref.shape
    g = tt // SUBLANES
    blk = c // n_blocks
    x3 = xr_ref[...].reshape(g, SUBLANES, c)

    if group_mode:
        prev3 = spad_ref[...]
    else:
        @pl.when(ti == 0)
        def _():
            carry_x[...] = spad_ref[0]
            carry_h[...] = h0_ref[0]
        prev3 = jnp.concatenate([carry_x[...][None], x3[:g - 1]], axis=0) if g > 1 else carry_x[...][None]

    xc3 = _causal_conv(x3, prev3, cw_ref[...], cb_ref[...])
    if not group_mode:
        carry_x[...] = x3[g - 1]

    xc = xc3.reshape(tt, c)
    xcb = xc.astype(BF16)

    def gate(w_ref, b_ref):
        parts = [jnp.dot(xcb[:, n * blk:(n + 1) * blk], w_ref[n], preferred_element_type=F32)
                 for n in range(n_blocks)]
        return jnp.concatenate(parts, axis=1) + b_ref[...]

    r = _sigmoid(gate(wa_ref, ba_ref))
    i = _sigmoid(gate(wx_ref, bx_ref))
    z = -ap_ref[...]
    softplus = jnp.maximum(z, 0.0) + jnp.log(1.0 + jnp.exp(-jnp.abs(z)))
    log_a = -RG_C * r * softplus
    a = jnp.exp(log_a)
    th = jnp.tanh(log_a)
    mult = jnp.sqrt(-2.0 * th / (1.0 - th))
    row = lax.broadcasted_iota(jnp.int32, (tt, c), 0)
    pos = pos_base + ((row % SUBLANES) if group_mode else (ti * tt + row))
    reset = pos == 0
    a = jnp.where(reset, 0.0, a)
    mult = jnp.where(reset, 1.0, mult)
    b = mult * i * xc

    a3 = a.reshape(g, SUBLANES, c)
    b3 = b.reshape(g, SUBLANES, c)
    t_idx = lax.broadcasted_iota(jnp.int32, a3.shape, 1)
    for dd in (1, 2, 4):
        valid = t_idx >= dd
        b3 = jnp.where(valid, a3 * pltpu.roll(b3, dd, 1) + b3, b3)
        a3 = jnp.where(valid, a3 * pltpu.roll(a3, dd, 1), a3)

    if group_mode:
        h3 = a3 * h0_ref[...] + b3
        hlast_ref[...] = h3
    else:
        a_scr[...] = a3
        b_scr[...] = b3

        def body(gi, carry):
            hg = a_scr[gi] * carry + b_scr[gi]
            b_scr[gi] = hg
            return jnp.broadcast_to(hg[SUBLANES - 1:SUBLANES, :], (SUBLANES, c))

        last = lax.fori_loop(0, g, body, carry_h[...])
        carry_h[...] = last
        hlast_ref[0] = last
        h3 = b_scr[...]
    za_ref[...] = (ga_ref[...] * h3.reshape(tt, c)).astype(za_ref.dtype)


def _rglru(xr2d, ga, spad, h0b, cw, cb, wa_bf, ba, wx_bf, bx, ap, *, n_seq, group_mode, pos_base):
    n, c = xr2d.shape
    n_blocks = wa_bf.shape[0]
    if group_mode:
        tt, grid = n, (1, 1)
        row_spec = pl.BlockSpec((n, c), lambda b, t: (0, 0))
        seq_spec = pl.BlockSpec((n_seq, SUBLANES, c), lambda b, t: (0, 0, 0))
    else:
        t_len = n // n_seq
        tt = min(ROW_TILE, t_len)
        nt = t_len // tt
        grid = (n_seq, nt)
        row_spec = pl.BlockSpec((tt, c), lambda b, t: (b * nt + t, 0))
        seq_spec = pl.BlockSpec((1, SUBLANES, c), lambda b, t: (b, 0, 0))
    g = tt // SUBLANES
    vec = lambda a: a.reshape(1, c)
    return pl.pallas_call(
        functools.partial(_rglru_kernel, group_mode=group_mode, pos_base=pos_base, n_blocks=n_blocks),
        grid=grid,
        in_specs=[row_spec, row_spec, seq_spec, seq_spec, _resident(cw.shape), _resident((1, c)),
                  _resident(wa_bf.shape), _resident((1, c)), _resident(wx_bf.shape), _resident((1, c)),
                  _resident((1, c))],
        out_specs=[row_spec, seq_spec],
        out_shape=[jax.ShapeDtypeStruct((n, c), BF16), jax.ShapeDtypeStruct((n_seq, SUBLANES, c), F32)],
        scratch_shapes=[pltpu.VMEM((SUBLANES, c), F32), pltpu.VMEM((SUBLANES, c), F32),
                        pltpu.VMEM((g, SUBLANES, c), F32), pltpu.VMEM((g, SUBLANES, c), F32)],
        compiler_params=_params(2),
        name="rglru",
    )(xr2d, ga, spad, h0b, cw, vec(cb), wa_bf, vec(ba), wx_bf, vec(bx), vec(ap))


def _bias_of(dist, valid, rb_ref, h):
    n = jnp.maximum(dist, 0)
    max_exact = N_BUCKETS // 2
    nf = jnp.maximum(n, 1).astype(F32)
    large = max_exact + (jnp.log(nf / max_exact) / math.log(MAX_DISTANCE / max_exact)
                         * (N_BUCKETS - max_exact)).astype(jnp.int32)
    large = jnp.minimum(large, N_BUCKETS - 1)
    bucket = jnp.where(n < max_exact, n, large)
    base = rb_ref[N_BUCKETS - 1, h]
    out = jnp.zeros(dist.shape, F32)
    for b in range(N_BUCKETS - 1):
        out = jnp.where(bucket == b, (rb_ref[b, h] - base) * LOG2E, out)
    return jnp.where(valid, out, NEG)


def _bias_kernel(rb_ref, bp_ref, blp_ref, bn_ref, *, tb, n_heads, dec_seq):
    h = pl.program_id(0)
    j = lax.broadcasted_iota(jnp.int32, (2 * tb, tb), 0)
    i = lax.broadcasted_iota(jnp.int32, (2 * tb, tb), 1)
    dist = i + tb - j
    bp_ref[0] = _bias_of(dist, dist >= 0, rb_ref, h)
    rows, page = blp_ref.shape
    t = lax.broadcasted_iota(jnp.int32, (rows, page), 0) % dec_seq
    key = lax.broadcasted_iota(jnp.int32, (rows, page), 1)
    dist = page + t - key
    blp_ref[...] = _bias_of(dist, dist >= 0, rb_ref, h)
    t = lax.broadcasted_iota(jnp.int32, (rows, LANES), 0) % dec_seq
    lane = lax.broadcasted_iota(jnp.int32, (rows, LANES), 1)
    dist = t - lane // n_heads
    bn_ref[...] = _bias_of(dist, jnp.logical_and((lane % n_heads) == h, dist >= 0), rb_ref, h)


def _bias_tiles(rel_bias, n_heads, tb, page, dec_seq):
    rows = 2 * dec_seq
    return pl.pallas_call(
        functools.partial(_bias_kernel, tb=tb, n_heads=n_heads, dec_seq=dec_seq),
        grid=(n_heads,),
        in_specs=[pl.BlockSpec(memory_space=pltpu.SMEM)],
        out_specs=[pl.BlockSpec((1, 2 * tb, tb), lambda h: (h, 0, 0)),
                   pl.BlockSpec((rows, page), lambda h: (h, 0)),
                   pl.BlockSpec((rows, LANES), lambda h: (h, 0))],
        out_shape=[jax.ShapeDtypeStruct((n_heads, 2 * tb, tb), F32),
                   jax.ShapeDtypeStruct((n_heads * rows, page), F32),
                   jax.ShapeDtypeStruct((n_heads * rows, LANES), F32)],
        compiler_params=_params(1),
        name="bias_tiles",
    )(rel_bias)


def _lambda(lq1, lk1, lq2, lk2, lambda_init):
    s1 = jnp.sum(lq1 * lk1, axis=-1, keepdims=True)
    s2 = jnp.sum(lq2 * lk2, axis=-1, keepdims=True)
    return jnp.exp(s1) - jnp.exp(s2) + lambda_init


def _qk(a, b):
    return lax.dot_general(a, b, (((1,), (1,)), ((), ())), preferred_element_type=F32)


def _all_sublanes(x, op):
    for dd in (1, 2, 4):
        x = op(x, pltpu.roll(x, dd, 0))
    return x


def _prompt_attn_kernel(q2_ref, k_ref, vt_ref, gb_ref, bias_ref, lq1_ref, lk1_ref, lq2_ref, lk2_ref,
                        sg_ref, o_ref, s0_ref, s1_ref, m0_ref, m1_ref, l0_ref, l1_ref, a0_ref, a1_ref,
                        *, tb, lambda_init):
    qi = pl.program_id(2)
    e = vt_ref.shape[0]
    s_refs = (s0_ref, s1_ref)
    m_refs = (m0_ref, m1_ref)
    l_refs = (l0_ref, l1_ref)
    acc_refs = (a0_ref, a1_ref)
    for mm in range(2):
        m_refs[mm][...] = jnp.full(m_refs[mm].shape, NEG, F32)
        l_refs[mm][...] = jnp.zeros(l_refs[mm].shape, F32)
        acc_refs[mm][...] = jnp.zeros(acc_refs[mm].shape, F32)

    def scores(mm, kj):
        start = pl.multiple_of(kj * tb, tb)
        return _qk(k_ref[pl.ds(start, tb), :], q2_ref[mm])

    def consume(mm, kj, bias):
        s = s_refs[mm][...]
        if bias is not None:
            s = s + bias
        s3 = s.reshape(tb // SUBLANES, SUBLANES, tb)
        m_prev = m_refs[mm][...]
        m_new = jnp.maximum(m_prev, _all_sublanes(jnp.max(s3, axis=0), jnp.maximum))
        alpha = jnp.exp2(m_prev - m_new)
        p3 = jnp.exp2(s3 - m_new[None])
        l_refs[mm][...] = alpha * l_refs[mm][...] + _all_sublanes(jnp.sum(p3, axis=0), jnp.add)
        start = pl.multiple_of(kj * tb, tb)
        pv = jnp.dot(vt_ref[:, pl.ds(start, tb)], p3.reshape(tb, tb).astype(BF16),
                     preferred_element_type=F32)
        acc = acc_refs[mm][...].reshape(e // SUBLANES, SUBLANES, tb) * alpha[None]
        acc_refs[mm][...] = acc.reshape(e, tb) + pv
        m_refs[mm][...] = m_new

    s0_ref[...] = scores(0, 0)

    def far(j, carry):
        s1_ref[...] = scores(1, j)
        consume(0, j, None)
        s0_ref[...] = scores(0, j + 1)
        consume(1, j, None)
        return carry

    lax.fori_loop(0, jnp.maximum(qi - 1, 0), far, 0)

    @pl.when(qi > 0)
    def _():
        bias = bias_ref[0, :tb, :]
        s1_ref[...] = scores(1, qi - 1)
        consume(0, qi - 1, bias)
        s0_ref[...] = scores(0, qi)
        consume(1, qi - 1, bias)

    bias = bias_ref[0, tb:, :]
    s1_ref[...] = scores(1, qi)
    consume(0, qi, bias)
    consume(1, qi, bias)

    lam = _lambda(lq1_ref[...], lk1_ref[...], lq2_ref[...], lk2_ref[...], lambda_init)

    def normalised(mm):
        acc = acc_refs[mm][...].reshape(e // SUBLANES, SUBLANES, tb)
        return (acc / l_refs[mm][...][None]).reshape(e, tb)

    o = (normalised(0) - lam * normalised(1)).T
    o_ref[...] = (gb_ref[...] * (_rms(o, sg_ref[...]) * (1.0 - lambda_init))).astype(o_ref.dtype)


def _prompt_attn(q2, kb, vt, gb, bias_p, lams, subln_g, *, n_seq, n_heads, lambda_init):
    _, n, d = q2.shape
    e = d // n_heads
    t_len = n // n_seq
    tb = ATTN_BLOCK
    nq = t_len // tb
    hd = lams[0].shape[-1]
    small = pl.BlockSpec((1, hd), lambda b, h, i: (0, 0))
    stat = pltpu.VMEM((SUBLANES, tb), F32)
    return pl.pallas_call(
        functools.partial(_prompt_attn_kernel, tb=tb, lambda_init=lambda_init),
        grid=(n_seq, n_heads, nq),
        in_specs=[pl.BlockSpec((2, tb, e), lambda b, h, i: (0, b * nq + i, h)),
                  pl.BlockSpec((t_len, e), lambda b, h, i: (b, h)),
                  pl.BlockSpec((e, t_len), lambda b, h, i: (h, b)),
                  pl.BlockSpec((tb, e), lambda b, h, i: (b * nq + i, h)),
                  pl.BlockSpec((1, 2 * tb, tb), lambda b, h, i: (h, 0, 0)),
                  small, small, small, small,
                  pl.BlockSpec((1, e), lambda b, h, i: (0, 0))],
        out_specs=pl.BlockSpec((tb, e), lambda b, h, i: (b * nq + i, h)),
        out_shape=jax.ShapeDtypeStruct((n, d), BF16),
        scratch_shapes=[pltpu.VMEM((tb, tb), F32), pltpu.VMEM((tb, tb), F32),
                        stat, stat, stat, stat,
                        pltpu.VMEM((e, tb), F32), pltpu.VMEM((e, tb), F32)],
        compiler_params=_params(3),
        name="prompt_attn",
    )(q2, kb, vt, gb, bias_p, *[x.reshape(1, hd) for x in lams], subln_g.reshape(1, e))


def _online_update(s, pv_of, m_ref, l_ref, acc_ref):
    m_prev = m_ref[...]
    m_new = jnp.maximum(m_prev, jnp.max(s, axis=-1, keepdims=True))
    alpha = jnp.exp2(m_prev - m_new)
    p = jnp.exp2(s - m_new[:, :1])
    l_ref[...] = alpha * l_ref[...] + jnp.sum(p, axis=-1, keepdims=True)
    acc_ref[...] = alpha * acc_ref[...] + pv_of(p)
    m_ref[...] = m_new


def _sample_attn_kernel(pt_ref, q2_ref, *refs, n_heads, dec_seq, pps, lambda_init):
    kp = refs[:pps]
    vp = refs[pps:2 * pps]
    (kn_ref, vn_ref, gb_ref, blp_ref, bn_ref, lq1_ref, lk1_ref, lq2_ref, lk2_ref, sg_ref,
     o_ref, q_ref, m_ref, l_ref, acc_ref) = refs[2 * pps:]
    del pt_ref
    c = pl.program_id(1)
    nc = pl.num_programs(1)
    e = kp[0].shape[1]
    page = kp[0].shape[0] // n_heads
    rows = 2 * dec_seq

    @pl.when(c == 0)
    def _():
        pieces = [q2_ref[mm][:, h * e:(h + 1) * e] for h in range(n_heads) for mm in range(2)]
        q_ref[...] = jnp.concatenate(pieces, axis=0)
        m_ref[...] = jnp.full(m_ref.shape, NEG, F32)
        l_ref[...] = jnp.zeros(l_ref.shape, F32)
        acc_ref[...] = jnp.zeros(acc_ref.shape, F32)

    def head_rows(ref, h):
        return ref[pl.ds(h, page, stride=n_heads), :]

    q = q_ref[...]
    is_last = (c == nc - 1).astype(F32)
    s_heads = []
    for h in range(n_heads):
        qh = q[h * rows:(h + 1) * rows]
        parts = [_qk(qh, head_rows(kp[j], h)) for j in range(pps)]
        parts[-1] = parts[-1] + blp_ref[h * rows:(h + 1) * rows, :] * is_last
        s_heads.append(jnp.concatenate(parts, axis=1))

    def past_pv(p):
        out = []
        for h in range(n_heads):
            ph = p[h * rows:(h + 1) * rows]
            acc = jnp.dot(ph[:, :page], head_rows(vp[0], h), preferred_element_type=F32)
            for j in range(1, pps):
                acc = acc + jnp.dot(ph[:, j * page:(j + 1) * page], head_rows(vp[j], h),
                                    preferred_element_type=F32)
            out.append(acc)
        return jnp.concatenate(out, axis=0)

    _online_update(jnp.concatenate(s_heads, axis=0), past_pv, m_ref, l_ref, acc_ref)

    @pl.when(c == nc - 1)
    def _():
        _online_update(_qk(q, kn_ref[...]) + bn_ref[...],
                       lambda p: jnp.dot(p, vn_ref[...], preferred_element_type=F32),
                       m_ref, l_ref, acc_ref)
        lam = _lambda(lq1_ref[...], lk1_ref[...], lq2_ref[...], lk2_ref[...], lambda_init)
        o_all = acc_ref[...] / l_ref[...]
        outs = []
        for h in range(n_heads):
            r0 = h * rows
            o = o_all[r0:r0 + dec_seq] - lam * o_all[r0 + dec_seq:r0 + rows]
            outs.append(_rms(o, sg_ref[...]) * (1.0 - lambda_init))
        o_ref[...] = gb_ref[...] * jnp.concatenate(outs, axis=1)


def _sample_attn(page_table, q2f, cache_k, cache_v, kn_rows, vn_rows, gb, bias_lp, bias_n, lams,
                 subln_g, *, n_seq, n_heads, lambda_init):
    _, n, d = q2f.shape
    dec_seq = n // n_seq
    e = d // n_heads
    n_pool, page = cache_k.shape[:2]
    n_pages = page_table.shape[1]
    pps = PAGES_PER_STEP
    nc = n_pages // pps
    hd = lams[0].shape[-1]
    rows = n_heads * 2 * dec_seq
    width = page * n_heads

    def page_spec(j):
        return pl.BlockSpec((None, width, e), lambda b, c, pt: (pt[b, c * pps + j], 0, 0))

    const2 = lambda b, c, pt: (0, 0)
    small = pl.BlockSpec((1, hd), const2)
    new_rows = pl.BlockSpec((LANES, e), lambda b, c, pt: (b, 0))
    seq = pl.BlockSpec((dec_seq, d), lambda b, c, pt: (b, 0))
    grid_spec = pltpu.PrefetchScalarGridSpec(
        num_scalar_prefetch=1,
        grid=(n_seq, nc),
        in_specs=[pl.BlockSpec((2, dec_seq, d), lambda b, c, pt: (0, b, 0))]
                 + [page_spec(j) for j in range(pps)] + [page_spec(j) for j in range(pps)]
                 + [new_rows, new_rows, seq,
                    pl.BlockSpec((rows, page), const2), pl.BlockSpec((rows, LANES), const2),
                    small, small, small, small, pl.BlockSpec((1, e), const2)],
        out_specs=seq,
        scratch_shapes=[pltpu.VMEM((rows, e), F32), pltpu.VMEM((rows, LANES), F32),
                        pltpu.VMEM((rows, LANES), F32), pltpu.VMEM((rows, e), F32)],
    )
    ck = cache_k.reshape(n_pool, width, e)
    cv = cache_v.reshape(n_pool, width, e)
    return pl.pallas_call(
        functools.partial(_sample_attn_kernel, n_heads=n_heads, dec_seq=dec_seq, pps=pps,
                          lambda_init=lambda_init),
        grid_spec=grid_spec,
        out_shape=jax.ShapeDtypeStruct((n, d), F32),
        compiler_params=_params(2),
        name="sample_attn",
    )(page_table, q2f, *([ck] * pps), *([cv] * pps), kn_rows, vn_rows, gb, bias_lp, bias_n,
      *[x.reshape(1, hd) for x in lams], subln_g.reshape(1, e))


def _new_token_rows(x2d, n_seq, n_heads):
    n, d = x2d.shape
    e = d // n_heads
    per_seq = (n // n_seq) * n_heads
    x = x2d.reshape(n_seq, per_seq, e)
    x = jnp.pad(x, ((0, 0), (0, LANES - per_seq), (0, 0)))
    return x.reshape(n_seq * LANES, e)


def _gelu_tanh(x):
    return 0.5 * x * (1.0 + jnp.tanh(math.sqrt(2.0 / math.pi) * (x + 0.044715 * (x * x * x))))


def _ffn_kernel(x_ref, za_ref, zb_ref, spad_ref, wout_ref, nffn_ref, wup_ref,
                cw_ref, cb_ref, wdown_ref, nfin_ref, y_ref, tail_ref, carry_ref, act_ref,
                *, group_mode, final_norm):
    ti = pl.program_id(1)
    tm, d = x_ref.shape
    g = tm // SUBLANES
    d_ff = wdown_ref.shape[0]

    merged = (za_ref[...].astype(F32) + zb_ref[...].astype(F32)).astype(BF16)
    x2 = x_ref[...] + jnp.dot(merged, wout_ref[...], preferred_element_type=F32)
    h2 = _rms(x2, nffn_ref[...]).astype(BF16)

    if not group_mode:
        @pl.when(ti == 0)
        def _():
            carry_ref[...] = spad_ref[0]

    def up_conv(cols):
        up3 = jnp.dot(h2, wup_ref[:, cols], preferred_element_type=F32).reshape(g, SUBLANES, -1)
        if group_mode:
            prev3 = spad_ref[:, :, cols]
            tail_ref[:, :, cols] = up3
        else:
            first = carry_ref[:, cols][None]
            prev3 = jnp.concatenate([first, up3[:g - 1]], axis=0) if g > 1 else first
            carry_ref[:, cols] = up3[g - 1]
            tail_ref[0, :, cols] = up3[g - 1]
        return _causal_conv(up3, prev3, cw_ref[:, cols], cb_ref[:, cols]).reshape(tm, -1)

    for c0 in range(0, d_ff, FFN_CHUNK):
        gate = up_conv(slice(c0, c0 + FFN_CHUNK))
        val = up_conv(slice(d_ff + c0, d_ff + c0 + FFN_CHUNK))
        act_ref[:, c0:c0 + FFN_CHUNK] = (_gelu_tanh(gate) * val).astype(BF16)
    x3 = x2 + jnp.dot(act_ref[...], wdown_ref[...], preferred_element_type=F32)
    y_ref[...] = _rms(x3, nfin_ref[...]) if final_norm else x3


def _ffn(x2d, za, zb, spad, wout_bf, nffn, wup_bf, cw, cb, wdown_bf, nfin,
         *, n_seq, group_mode, final_norm):
    n, d = x2d.shape
    f2 = wup_bf.shape[1]
    if group_mode:
        tm, grid = n, (1, 1)
        row = pl.BlockSpec((n, d), lambda b, t: (0, 0))
        seq = pl.BlockSpec((n_seq, SUBLANES, f2), lambda b, t: (0, 0, 0))
    else:
        t_len = n // n_seq
        tm = min(FFN_ROW_TILE, t_len)
        nt = t_len // tm
        grid = (n_seq, nt)
        row = pl.BlockSpec((tm, d), lambda b, t: (b * nt + t, 0))
        seq = pl.BlockSpec((1, SUBLANES, f2), lambda b, t: (b, 0, 0))
    return pl.pallas_call(
        functools.partial(_ffn_kernel, group_mode=group_mode, final_norm=final_norm),
        grid=grid,
        in_specs=[row, row, row, seq, _resident(wout_bf.shape), _resident((1, d)),
                  _resident(wup_bf.shape), _resident(cw.shape), _resident((1, f2)),
                  _resident(wdown_bf.shape), _resident((1, d))],
        out_specs=[row, seq],
        out_shape=[jax.ShapeDtypeStruct((n, d), F32), jax.ShapeDtypeStruct((n_seq, SUBLANES, f2), F32)],
        scratch_shapes=[pltpu.VMEM((SUBLANES, f2), F32), pltpu.VMEM((tm, f2 // 2), BF16)],
        compiler_params=_params(2),
        name="merge_ffn",
    )(x2d, za, zb, spad, wout_bf, nffn.reshape(1, d), wup_bf, cw, cb.reshape(1, f2),
      wdown_bf, nfin.reshape(1, d))


def _pad_state(buf):
    b, w1, c = buf.shape
    return jnp.concatenate([jnp.zeros((b, SUBLANES - w1, c), buf.dtype), buf], axis=1)


def kernel(x_prompt, x_sample, cache_k, cache_v, page_table, state_rglru_h, state_rglru_conv, state_ffn_conv, w_in, conv_w, conv_b, rg_w_a, rg_b_a, rg_w_x, rg_b_x, rg_a_param, lambda_q1, lambda_k1, lambda_q2, lambda_k2, subln_g, rel_bias, w_out, norm_attn, norm_ffn, w_up, ffn_conv_w, ffn_conv_b, w_down, norm_final):
    bsz, t_len, d = x_prompt.shape
    dbsz, dec_seq, _ = x_sample.shape
    depth = w_in.shape[0]
    n_heads = cache_k.shape[3]
    kv_dim = cache_k.shape[4]
    head_dim = kv_dim // 2
    page = cache_k.shape[2]
    past = page_table.shape[1] * page
    cw_w = conv_w.shape[1]
    fw_w = ffn_conv_w.shape[1]
    assert w_in.shape[2] == 6 * d and n_heads * kv_dim == d and rg_a_param.shape[1] == d
    assert dec_seq == SUBLANES and t_len % ATTN_BLOCK == 0 and page_table.shape[1] % PAGES_PER_STEP == 0
    assert ATTN_BLOCK >= MAX_DISTANCE and page >= MAX_DISTANCE and dec_seq * n_heads <= LANES
    assert kv_dim == LANES and n_heads == SUBLANES

    bias_p, bias_lp, bias_n = _bias_tiles(rel_bias, n_heads, ATTN_BLOCK, page, dec_seq)

    xp = x_prompt.reshape(bsz * t_len, d)
    xs = x_sample.reshape(dbsz * dec_seq, d)
    outs = [[] for _ in range(10)]
    for l in range(depth):
        lambda_init = 0.8 - 0.6 * math.exp(-0.3 * l)
        last = l == depth - 1
        w_in_bf = w_in[l].astype(BF16)
        wa_bf = rg_w_a[l].astype(BF16)
        wx_bf = rg_w_x[l].astype(BF16)
        wout_bf = w_out[l].astype(BF16)
        wup_bf = w_up[l].astype(BF16)
        wdown_bf = w_down[l].astype(BF16)
        lams = (lambda_q1[l], lambda_k1[l], lambda_q2[l], lambda_k2[l])

        def branch_a(xr, ga, spad, h0, n_seq, group_mode, pos_base):
            h0b = jnp.broadcast_to(h0[:, None, :], (n_seq, SUBLANES, d))
            return _rglru(xr, ga, spad, h0b, conv_w[l], conv_b[l], wa_bf, rg_b_a[l], wx_bf, rg_b_x[l],
                          rg_a_param[l], n_seq=n_seq, group_mode=group_mode, pos_base=pos_base)

        def channel_mix(x2d, za, zb, spad, n_seq, group_mode):
            return _ffn(x2d, za, zb, spad, wout_bf, norm_ffn[l], wup_bf, ffn_conv_w[l],
                        ffn_conv_b[l], wdown_bf, norm_final, n_seq=n_seq, group_mode=group_mode,
                        final_norm=last)

        xr, q2, k, kb, v, vt, ga, gb = _inproj(xp, norm_attn[l], w_in_bf, head_dim)
        za, hp = branch_a(xr, ga, jnp.zeros((bsz, SUBLANES, d), F32), jnp.zeros((bsz, d), F32),
                          bsz, False, 0)
        zb = _prompt_attn(q2, kb, vt, gb, bias_p, lams, subln_g[l], n_seq=bsz, n_heads=n_heads,
                          lambda_init=lambda_init)
        xp, fp = channel_mix(xp, za, zb, jnp.zeros((bsz, SUBLANES, ffn_conv_w.shape[2]), F32),
                             bsz, False)
        outs[0].append(k.reshape(bsz, t_len, n_heads, kv_dim))
        outs[1].append(v.reshape(bsz, t_len, n_heads, kv_dim))
        outs[4].append(hp[:, SUBLANES - 1])
        outs[6].append(xr.reshape(bsz, t_len, d)[:, t_len - (cw_w - 1):])
        outs[8].append(fp[:, SUBLANES - (fw_w - 1):])

        xr, q2, k, _, v, _, ga, gb = _inproj(xs, norm_attn[l], w_in_bf, head_dim)
        za, hs = branch_a(xr, ga, _pad_state(state_rglru_conv[l]), state_rglru_h[l], dbsz, True, past)
        zb = _sample_attn(page_table, q2.astype(F32), cache_k[l], cache_v[l],
                          _new_token_rows(k, dbsz, n_heads), _new_token_rows(v, dbsz, n_heads),
                          gb, bias_lp, bias_n, lams, subln_g[l], n_seq=dbsz, n_heads=n_heads,
                          lambda_init=lambda_init)
        xs, fs = channel_mix(xs, za, zb, _pad_state(state_ffn_conv[l]), dbsz, True)
        outs[2].append(k.reshape(dbsz, dec_seq, n_heads, kv_dim))
        outs[3].append(v.reshape(dbsz, dec_seq, n_heads, kv_dim))
        outs[5].append(hs[:, SUBLANES - 1])
        outs[7].append(xr.reshape(dbsz, dec_seq, d)[:, dec_seq - (cw_w - 1):])
        outs[9].append(fs[:, SUBLANES - (fw_w - 1):])

    return (xp.reshape(bsz, t_len, d), xs.reshape(dbsz, dec_seq, d),
            jnp.stack(outs[0]), jnp.stack(outs[1]), jnp.stack(outs[2]), jnp.stack(outs[3]),
            jnp.stack(outs[4]), jnp.stack(outs[5]), jnp.stack(outs[6]), jnp.stack(outs[7]),
            jnp.stack(outs[8]), jnp.stack(outs[9]))
```

```python
import functools
import math

import jax
import jax.numpy as jnp
from jax import lax
from jax.experimental import pallas as pl
from jax.experimental.pallas import tpu as pltpu

F32 = jnp.float32
BF16 = jnp.bfloat16

EPS = 1e-6
RG_C = 8.0
N_BUCKETS = 32
MAX_DISTANCE = 128
NEG = -1e30
LOG2E = math.log2(math.e)

SUBLANES = 8
LANES = 128
ROW_TILE = 256
FFN_ROW_TILE = 512
ATTN_BLOCK = 512
FFN_CHUNK = 512
PAGES_PER_STEP = 8
VMEM_LIMIT = 52 * 1024 * 1024


def _params(n_axes, vmem=VMEM_LIMIT):
    return pltpu.CompilerParams(
        dimension_semantics=("arbitrary",) * n_axes, vmem_limit_bytes=vmem)


def _resident(shape):
    nd = len(shape)
    return pl.BlockSpec(shape, lambda *_: (0,) * nd, pipeline_mode=pl.Buffered(1))


def _sigmoid(x):
    return 0.5 * (1.0 + jnp.tanh(0.5 * x))


def _rms(x, g):
    return x * lax.rsqrt(jnp.mean(x * x, axis=-1, keepdims=True) + EPS) * g


def _inproj_kernel(x_ref, g_ref, w_ref, xr_ref, q2_ref, k_ref, kb_ref, v_ref, vt_ref,
                   ga_ref, gb_ref, *, d, head_dim):
    h = _rms(x_ref[...], g_ref[...]).astype(BF16)

    def proj(j):
        return jnp.dot(h, w_ref[:, j * d:(j + 1) * d], preferred_element_type=F32)

    xr_ref[...] = proj(0)
    q = proj(1) * (head_dim ** -0.5 * LOG2E)
    lane = lax.broadcasted_iota(jnp.int32, q.shape, 1)
    first = (lane % (2 * head_dim)) < head_dim
    q2_ref[0] = jnp.where(first, q, 0.0).astype(BF16)
    q2_ref[1] = jnp.where(first, 0.0, q).astype(BF16)
    k = proj(2)
    k_ref[...] = k
    kb_ref[...] = k.astype(BF16)
    v = proj(3)
    v_ref[...] = v
    vt_ref[...] = v.T.astype(BF16)
    ga_ref[...] = _sigmoid(proj(4))
    gb_ref[...] = _sigmoid(proj(5))


def _inproj(x2d, g, w_bf, head_dim):
    n, d = x2d.shape
    tm = min(ROW_TILE, n)
    row = lambda i: (i, 0)
    blk = pl.BlockSpec((tm, d), row)
    f32 = jax.ShapeDtypeStruct((n, d), F32)
    return pl.pallas_call(
        functools.partial(_inproj_kernel, d=d, head_dim=head_dim),
        grid=(n // tm,),
        in_specs=[blk, _resident((1, d)), _resident(w_bf.shape)],
        out_specs=[blk, pl.BlockSpec((2, tm, d), lambda i: (0, i, 0)), blk, blk, blk,
                   pl.BlockSpec((d, tm), lambda i: (0, i)), blk, blk],
        out_shape=[f32, jax.ShapeDtypeStruct((2, n, d), BF16), f32,
                   jax.ShapeDtypeStruct((n, d), BF16), f32,
                   jax.ShapeDtypeStruct((d, n), BF16), f32, f32],
        compiler_params=_params(1),
        name="inproj",
    )(x2d, g.reshape(1, d), w_bf)


def _shifted(x3, prev3, d, t_idx):
    return jnp.where(t_idx >= d, pltpu.roll(x3, d, 1), pltpu.roll(prev3, d, 1))


def _causal_conv(x3, prev3, w, b):
    width = w.shape[0]
    t_idx = lax.broadcasted_iota(jnp.int32, x3.shape, 1)
    y = b + w[width - 1:width] * x3
    for dd in range(1, width):
        y = y + w[width - 1 - dd:width - dd] * _shifted(x3, prev3, dd, t_idx)
    return y


def _rglru_kernel(xr_ref, ga_ref, spad_ref, h0_ref, cw_ref, cb_ref, wa_ref, ba_ref, wx_ref, bx_ref,
                  ap_ref, za_ref, hlast_ref, carry_x, carry_h, a_scr, b_scr,
                  *, group_mode, pos_base, n_blocks):
    ti = pl.program_id(1)
    tt, c = xr_ref.shape
    g = tt // SUBLANES
    blk = c // n_blocks
    x3 = xr_ref[...].reshape(g, SUBLANES, c)

    if group_mode:
        prev3 = spad_ref[...]
    else:
        @pl.when(ti == 0)
        def _():
            carry_x[...] = spad_ref[0]
            carry_h[...] = h0_ref[0]
        prev3 = jnp.concatenate([carry_x[...][None], x3[:g - 1]], axis=0) if g > 1 else carry_x[...][None]

    xc3 = _causal_conv(x3, prev3, cw_ref[...], cb_ref[...])
    if not group_mode:
        carry_x[...] = x3[g - 1]

    xc = xc3.reshape(tt, c)
    xcb = xc.astype(BF16)

    def gate(w_ref, b_ref):
        parts = [jnp.dot(xcb[:, n * blk:(n + 1) * blk], w_ref[n], preferred_element_type=F32)
                 for n in range(n_blocks)]
        return jnp.concatenate(parts, axis=1) + b_ref[...]

    r = _sigmoid(gate(wa_ref, ba_ref))
    i = _sigmoid(gate(wx_ref, bx_ref))
    z = -ap_ref[...]
    softplus = jnp.maximum(z, 0.0) + jnp.log(1.0 + jnp.exp(-jnp.abs(z)))
    log_a = -RG_C * r * softplus
    a = jnp.exp(log_a)
    th = jnp.tanh(log_a)
    mult = jnp.sqrt(-2.0 * th / (1.0 - th))
    row = lax.broadcasted_iota(jnp.int32, (tt, c), 0)
    pos = pos_base + ((row % SUBLANES) if group_mode else (ti * tt + row))
    reset = pos == 0
    a = jnp.where(reset, 0.0, a)
    mult = jnp.where(reset, 1.0, mult)
    b = mult * i * xc

    a3 = a.reshape(g, SUBLANES, c)
    b3 = b.reshape(g, SUBLANES, c)
    t_idx = lax.broadcasted_iota(jnp.int32, a3.shape, 1)
    for dd in (1, 2, 4):
        valid = t_idx >= dd
        b3 = jnp.where(valid, a3 * pltpu.roll(b3, dd, 1) + b3, b3)
        a3 = jnp.where(valid, a3 * pltpu.roll(a3, dd, 1), a3)

    if group_mode:
        h3 = a3 * h0_ref[...] + b3
        hlast_ref[...] = h3
    else:
        a_scr[...] = a3
        b_scr[...] = b3

        def body(gi, carry):
            hg = a_scr[gi] * carry + b_scr[gi]
            b_scr[gi] = hg
            return jnp.broadcast_to(hg[SUBLANES - 1:SUBLANES, :], (SUBLANES, c))

        last = lax.fori_loop(0, g, body, carry_h[...])
        carry_h[...] = last
        hlast_ref[0] = last
        h3 = b_scr[...]
    za_ref[...] = (ga_ref[...] * h3.reshape(tt, c)).astype(za_ref.dtype)


def _rglru(xr2d, ga, spad, h0b, cw, cb, wa_bf, ba, wx_bf, bx, ap, *, n_seq, group_mode, pos_base):
    n, c = xr2d.shape
    n_blocks = wa_bf.shape[0]
    if group_mode:
        tt, grid = n, (1, 1)
        row_spec = pl.BlockSpec((n, c), lambda b, t: (0, 0))
        seq_spec = pl.BlockSpec((n_seq, SUBLANES, c), lambda b, t: (0, 0, 0))
    else:
        t_len = n // n_seq
        tt = min(ROW_TILE, t_len)
        nt = t_len // tt
        grid = (n_seq, nt)
        row_spec = pl.BlockSpec((tt, c), lambda b, t: (b * nt + t, 0))
        seq_spec = pl.BlockSpec((1, SUBLANES, c), lambda b, t: (b, 0, 0))
    g = tt // SUBLANES
    vec = lambda a: a.reshape(1, c)
    return pl.pallas_call(
        functools.partial(_rglru_kernel, group_mode=group_mode, pos_base=pos_base, n_blocks=n_blocks),
        grid=grid,
        in_specs=[row_spec, row_spec, seq_spec, seq_spec, _resident(cw.shape), _resident((1, c)),
                  _resident(wa_bf.shape), _resident((1, c)), _resident(wx_bf.shape), _resident((1, c)),
                  _resident((1, c))],
        out_specs=[row_spec, seq_spec],
        out_shape=[jax.ShapeDtypeStruct((n, c), BF16), jax.ShapeDtypeStruct((n_seq, SUBLANES, c), F32)],
        scratch_shapes=[pltpu.VMEM((SUBLANES, c), F32), pltpu.VMEM((SUBLANES, c), F32),
                        pltpu.VMEM((g, SUBLANES, c), F32), pltpu.VMEM((g, SUBLANES, c), F32)],
        compiler_params=_params(2),
        name="rglru",
    )(xr2d, ga, spad, h0b, cw, vec(cb), wa_bf, vec(ba), wx_bf, vec(bx), vec(ap))


def _bias_of(dist, valid, rb_ref, h):
    n = jnp.maximum(dist, 0)
    max_exact = N_BUCKETS // 2
    nf = jnp.maximum(n, 1).astype(F32)
    large = max_exact + (jnp.log(nf / max_exact) / math.log(MAX_DISTANCE / max_exact)
                         * (N_BUCKETS - max_exact)).astype(jnp.int32)
    large = jnp.minimum(large, N_BUCKETS - 1)
    bucket = jnp.where(n < max_exact, n, large)
    base = rb_ref[N_BUCKETS - 1, h]
    out = jnp.zeros(dist.shape, F32)
    for b in range(N_BUCKETS - 1):
        out = jnp.where(bucket == b, (rb_ref[b, h] - base) * LOG2E, out)
    return jnp.where(valid, out, NEG)


def _bias_kernel(rb_ref, bp_ref, blp_ref, bn_ref, *, tb, n_heads, dec_seq):
    h = pl.program_id(0)
    j = lax.broadcasted_iota(jnp.int32, (2 * tb, tb), 0)
    i = lax.broadcasted_iota(jnp.int32, (2 * tb, tb), 1)
    dist = i + tb - j
    bp_ref[0] = _bias_of(dist, dist >= 0, rb_ref, h)
    rows, page = blp_ref.shape
    t = lax.broadcasted_iota(jnp.int32, (rows, page), 0) % dec_seq
    key = lax.broadcasted_iota(jnp.int32, (rows, page), 1)
    dist = page + t - key
    blp_ref[...] = _bias_of(dist, dist >= 0, rb_ref, h)
    t = lax.broadcasted_iota(jnp.int32, (rows, LANES), 0) % dec_seq
    lane = lax.broadcasted_iota(jnp.int32, (rows, LANES), 1)
    dist = t - lane // n_heads
    bn_ref[...] = _bias_of(dist, jnp.logical_and((lane % n_heads) == h, dist >= 0), rb_ref, h)


def _bias_tiles(rel_bias, n_heads, tb, page, dec_seq):
    rows = 2 * dec_seq
    return pl.pallas_call(
        functools.partial(_bias_kernel, tb=tb, n_heads=n_heads, dec_seq=dec_seq),
        grid=(n_heads,),
        in_specs=[pl.BlockSpec(memory_space=pltpu.SMEM)],
        out_specs=[pl.BlockSpec((1, 2 * tb, tb), lambda h: (h, 0, 0)),
                   pl.BlockSpec((rows, page), lambda h: (h, 0)),
                   pl.BlockSpec((rows, LANES), lambda h: (h, 0))],
        out_shape=[jax.ShapeDtypeStruct((n_heads, 2 * tb, tb), F32),
                   jax.ShapeDtypeStruct((n_heads * rows, page), F32),
                   jax.ShapeDtypeStruct((n_heads * rows, LANES), F32)],
        compiler_params=_params(1),
        name="bias_tiles",
    )(rel_bias)


def _lambda(lq1, lk1, lq2, lk2, lambda_init):
    s1 = jnp.sum(lq1 * lk1, axis=-1, keepdims=True)
    s2 = jnp.sum(lq2 * lk2, axis=-1, keepdims=True)
    return jnp.exp(s1) - jnp.exp(s2) + lambda_init


def _qk(a, b):
    return lax.dot_general(a, b, (((1,), (1,)), ((), ())), preferred_element_type=F32)


def _all_sublanes(x, op):
    for dd in (1, 2, 4):
        x = op(x, pltpu.roll(x, dd, 0))
    return x


def _prompt_attn_kernel(q2_ref, k_ref, vt_ref, gb_ref, bias_ref, lq1_ref, lk1_ref, lq2_ref, lk2_ref,
                        sg_ref, o_ref, s0_ref, s1_ref, m0_ref, m1_ref, l0_ref, l1_ref, a0_ref, a1_ref,
                        *, tb, lambda_init):
    e, t_len = vt_ref.shape
    nq = t_len // tb
    s_refs = (s0_ref, s1_ref)
    m_refs = (m0_ref, m1_ref)
    l_refs = (l0_ref, l1_ref)
    acc_refs = (a0_ref, a1_ref)

    def block(i):
        return pl.ds(i * tb if isinstance(i, int) else pl.multiple_of(i * tb, tb), tb)

    def init():
        for mm in range(2):
            m_refs[mm][...] = jnp.full(m_refs[mm].shape, NEG, F32)
            l_refs[mm][...] = jnp.zeros(l_refs[mm].shape, F32)
            acc_refs[mm][...] = jnp.zeros(acc_refs[mm].shape, F32)

    def scores(mm, qi, kj):
        return _qk(k_ref[block(kj), :], q2_ref[mm, block(qi), :])

    def consume(mm, kj, bias):
        s = s_refs[mm][...]
        if bias is not None:
            s = s + bias
        s3 = s.reshape(tb // SUBLANES, SUBLANES, tb)
        m_prev = m_refs[mm][...]
        m_new = jnp.maximum(m_prev, _all_sublanes(jnp.max(s3, axis=0), jnp.maximum))
        alpha = jnp.exp2(m_prev - m_new)
        p3 = jnp.exp2(s3 - m_new[None])
        l_refs[mm][...] = alpha * l_refs[mm][...] + _all_sublanes(jnp.sum(p3, axis=0), jnp.add)
        pv = jnp.dot(vt_ref[:, block(kj)], p3.reshape(tb, tb).astype(BF16),
                     preferred_element_type=F32)
        acc = acc_refs[mm][...].reshape(e // SUBLANES, SUBLANES, tb) * alpha[None]
        acc_refs[mm][...] = acc.reshape(e, tb) + pv
        m_refs[mm][...] = m_new

    def step(qi, kj, bias, next_q, next_k):
        s1_ref[...] = scores(1, qi, kj)
        consume(0, kj, bias)
        s0_ref[...] = scores(0, next_q, next_k)
        consume(1, kj, bias)

    lam = _lambda(lq1_ref[...], lk1_ref[...], lq2_ref[...], lk2_ref[...], lambda_init)

    def finalize(qi):
        def normalised(mm):
            acc = acc_refs[mm][...].reshape(e // SUBLANES, SUBLANES, tb)
            return (acc / l_refs[mm][...][None]).reshape(e, tb)

        o = (normalised(0) - lam * normalised(1)).T
        y = _rms(o, sg_ref[...]) * (1.0 - lambda_init)
        o_ref[block(qi), :] = (gb_ref[block(qi), :] * y).astype(o_ref.dtype)

    bias_prev = lambda: bias_ref[0, :tb, :]
    bias_diag = lambda: bias_ref[0, tb:, :]

    init()
    s0_ref[...] = scores(0, 0, 0)
    step(0, 0, bias_diag(), 1 if nq > 1 else 0, 0)
    finalize(0)

    def tile(qi, carry):
        init()
        n_far = qi - 1
        odd = n_far & 1

        @pl.when(odd == 1)
        def _():
            step(qi, 0, None, qi, 1)

        def far_pair(jj, c):
            j = odd + 2 * jj
            step(qi, j, None, qi, j + 1)
            step(qi, j + 1, None, qi, j + 2)
            return c

        lax.fori_loop(0, lax.shift_right_logical(n_far, 1), far_pair, 0)
        step(qi, qi - 1, bias_prev(), qi, qi)
        step(qi, qi, bias_diag(), jnp.minimum(qi + 1, nq - 1), 0)
        finalize(qi)
        return carry

    lax.fori_loop(1, nq, tile, 0)


def _prompt_attn(q2, kb, vt, gb, bias_p, lams, subln_g, *, n_seq, n_heads, lambda_init):
    _, n, d = q2.shape
    e = d // n_heads
    t_len = n // n_seq
    tb = ATTN_BLOCK
    hd = lams[0].shape[-1]
    small = pl.BlockSpec((1, hd), lambda b, h: (0, 0))
    rows = pl.BlockSpec((t_len, e), lambda b, h: (b, h))
    stat = pltpu.VMEM((SUBLANES, tb), F32)
    return pl.pallas_call(
        functools.partial(_prompt_attn_kernel, tb=tb, lambda_init=lambda_init),
        grid=(n_seq, n_heads),
        in_specs=[pl.BlockSpec((2, t_len, e), lambda b, h: (0, b, h)),
                  rows,
                  pl.BlockSpec((e, t_len), lambda b, h: (h, b)),
                  rows,
                  pl.BlockSpec((1, 2 * tb, tb), lambda b, h: (h, 0, 0)),
                  small, small, small, small,
                  pl.BlockSpec((1, e), lambda b, h: (0, 0))],
        out_specs=rows,
        out_shape=jax.ShapeDtypeStruct((n, d), BF16),
        scratch_shapes=[pltpu.VMEM((tb, tb), F32), pltpu.VMEM((tb, tb), F32),
                        stat, stat, stat, stat,
                        pltpu.VMEM((e, tb), F32), pltpu.VMEM((e, tb), F32)],
        compiler_params=_params(2),
        name="prompt_attn",
    )(q2, kb, vt, gb, bias_p, *[x.reshape(1, hd) for x in lams], subln_g.reshape(1, e))


def _online_update(s, pv_of, m_ref, l_ref, acc_ref):
    m_prev = m_ref[...]
    m_new = jnp.maximum(m_prev, jnp.max(s, axis=-1, keepdims=True))
    alpha = jnp.exp2(m_prev - m_new)
    p = jnp.exp2(s - m_new[:, :1])
    l_ref[...] = alpha * l_ref[...] + jnp.sum(p, axis=-1, keepdims=True)
    acc_ref[...] = alpha * acc_ref[...] + pv_of(p)
    m_ref[...] = m_new


def _sample_attn_kernel(pt_ref, q2_ref, *refs, n_heads, dec_seq, pps, lambda_init):
    kp = refs[:pps]
    vp = refs[pps:2 * pps]
    (kn_ref, vn_ref, gb_ref, blp_ref, bn_ref, lq1_ref, lk1_ref, lq2_ref, lk2_ref, sg_ref,
     o_ref, q_ref, m_ref, l_ref, acc_ref) = refs[2 * pps:]
    del pt_ref
    c = pl.program_id(1)
    nc = pl.num_programs(1)
    e = kp[0].shape[1]
    page = kp[0].shape[0] // n_heads
    rows = 2 * dec_seq

    @pl.when(c == 0)
    def _():
        pieces = [q2_ref[mm][:, h * e:(h + 1) * e] for h in range(n_heads) for mm in range(2)]
        q_ref[...] = jnp.concatenate(pieces, axis=0)
        m_ref[...] = jnp.full(m_ref.shape, NEG, F32)
        l_ref[...] = jnp.zeros(l_ref.shape, F32)
        acc_ref[...] = jnp.zeros(acc_ref.shape, F32)

    def head_rows(ref, h):
        return ref[pl.ds(h, page, stride=n_heads), :]

    q = q_ref[...]
    is_last = (c == nc - 1).astype(F32)
    s_heads = []
    for h in range(n_heads):
        qh = q[h * rows:(h + 1) * rows]
        parts = [_qk(qh, head_rows(kp[j], h)) for j in range(pps)]
        parts[-1] = parts[-1] + blp_ref[h * rows:(h + 1) * rows, :] * is_last
        s_heads.append(jnp.concatenate(parts, axis=1))

    def past_pv(p):
        out = []
        for h in range(n_heads):
            ph = p[h * rows:(h + 1) * rows]
            acc = jnp.dot(ph[:, :page], head_rows(vp[0], h), preferred_element_type=F32)
            for j in range(1, pps):
                acc = acc + jnp.dot(ph[:, j * page:(j + 1) * page], head_rows(vp[j], h),
                                    preferred_element_type=F32)
            out.append(acc)
        return jnp.concatenate(out, axis=0)

    _online_update(jnp.concatenate(s_heads, axis=0), past_pv, m_ref, l_ref, acc_ref)

    @pl.when(c == nc - 1)
    def _():
        _online_update(_qk(q, kn_ref[...]) + bn_ref[...],
                       lambda p: jnp.dot(p, vn_ref[...], preferred_element_type=F32),
                       m_ref, l_ref, acc_ref)
        lam = _lambda(lq1_ref[...], lk1_ref[...], lq2_ref[...], lk2_ref[...], lambda_init)
        o_all = acc_ref[...] / l_ref[...]
        outs = []
        for h in range(n_heads):
            r0 = h * rows
            o = o_all[r0:r0 + dec_seq] - lam * o_all[r0 + dec_seq:r0 + rows]
            outs.append(_rms(o, sg_ref[...]) * (1.0 - lambda_init))
        o_ref[...] = gb_ref[...] * jnp.concatenate(outs, axis=1)


def _sample_attn(page_table, q2f, cache_k, cache_v, kn_rows, vn_rows, gb, bias_lp, bias_n, lams,
                 subln_g, *, n_seq, n_heads, lambda_init):
    _, n, d = q2f.shape
    dec_seq = n // n_seq
    e = d // n_heads
    n_pool, page = cache_k.shape[:2]
    n_pages = page_table.shape[1]
    pps = PAGES_PER_STEP
    nc = n_pages // pps
    hd = lams[0].shape[-1]
    rows = n_heads * 2 * dec_seq
    width = page * n_heads

    def page_spec(j):
        return pl.BlockSpec((None, width, e), lambda b, c, pt: (pt[b, c * pps + j], 0, 0))

    const2 = lambda b, c, pt: (0, 0)
    small = pl.BlockSpec((1, hd), const2)
    new_rows = pl.BlockSpec((LANES, e), lambda b, c, pt: (b, 0))
    seq = pl.BlockSpec((dec_seq, d), lambda b, c, pt: (b, 0))
    grid_spec = pltpu.PrefetchScalarGridSpec(
        num_scalar_prefetch=1,
        grid=(n_seq, nc),
        in_specs=[pl.BlockSpec((2, dec_seq, d), lambda b, c, pt: (0, b, 0))]
                 + [page_spec(j) for j in range(pps)] + [page_spec(j) for j in range(pps)]
                 + [new_rows, new_rows, seq,
                    pl.BlockSpec((rows, page), const2), pl.BlockSpec((rows, LANES), const2),
                    small, small, small, small, pl.BlockSpec((1, e), const2)],
        out_specs=seq,
        scratch_shapes=[pltpu.VMEM((rows, e), F32), pltpu.VMEM((rows, LANES), F32),
                        pltpu.VMEM((rows, LANES), F32), pltpu.VMEM((rows, e), F32)],
    )
    ck = cache_k.reshape(n_pool, width, e)
    cv = cache_v.reshape(n_pool, width, e)
    return pl.pallas_call(
        functools.partial(_sample_attn_kernel, n_heads=n_heads, dec_seq=dec_seq, pps=pps,
                          lambda_init=lambda_init),
        grid_spec=grid_spec,
        out_shape=jax.ShapeDtypeStruct((n, d), F32),
        compiler_params=_params(2),
        name="sample_attn",
    )(page_table, q2f, *([ck] * pps), *([cv] * pps), kn_rows, vn_rows, gb, bias_lp, bias_n,
      *[x.reshape(1, hd) for x in lams], subln_g.reshape(1, e))


def _new_token_rows(x2d, n_seq, n_heads):
    n, d = x2d.shape
    e = d // n_heads
    per_seq = (n // n_seq) * n_heads
    x = x2d.reshape(n_seq, per_seq, e)
    x = jnp.pad(x, ((0, 0), (0, LANES - per_seq), (0, 0)))
    return x.reshape(n_seq * LANES, e)


def _gelu_tanh(x):
    return 0.5 * x * (1.0 + jnp.tanh(math.sqrt(2.0 / math.pi) * (x + 0.044715 * (x * x * x))))


def _ffn_kernel(x_ref, za_ref, zb_ref, spad_ref, wout_ref, nffn_ref, wup_ref,
                cw_ref, cb_ref, wdown_ref, nfin_ref, y_ref, tail_ref, carry_ref, act_ref,
                *, group_mode, final_norm):
    ti = pl.program_id(1)
    tm, d = x_ref.shape
    g = tm // SUBLANES
    d_ff = wdown_ref.shape[0]

    merged = (za_ref[...].astype(F32) + zb_ref[...].astype(F32)).astype(BF16)
    x2 = x_ref[...] + jnp.dot(merged, wout_ref[...], preferred_element_type=F32)
    h2 = _rms(x2, nffn_ref[...]).astype(BF16)

    if not group_mode:
        @pl.when(ti == 0)
        def _():
            carry_ref[...] = spad_ref[0]

    def up_conv(cols):
        up3 = jnp.dot(h2, wup_ref[:, cols], preferred_element_type=F32).reshape(g, SUBLANES, -1)
        if group_mode:
            prev3 = spad_ref[:, :, cols]
            tail_ref[:, :, cols] = up3
        else:
            first = carry_ref[:, cols][None]
            prev3 = jnp.concatenate([first, up3[:g - 1]], axis=0) if g > 1 else first
            carry_ref[:, cols] = up3[g - 1]
            tail_ref[0, :, cols] = up3[g - 1]
        return _causal_conv(up3, prev3, cw_ref[:, cols], cb_ref[:, cols]).reshape(tm, -1)

    for c0 in range(0, d_ff, FFN_CHUNK):
        gate = up_conv(slice(c0, c0 + FFN_CHUNK))
        val = up_conv(slice(d_ff + c0, d_ff + c0 + FFN_CHUNK))
        act_ref[:, c0:c0 + FFN_CHUNK] = (_gelu_tanh(gate) * val).astype(BF16)
    x3 = x2 + jnp.dot(act_ref[...], wdown_ref[...], preferred_element_type=F32)
    y_ref[...] = _rms(x3, nfin_ref[...]) if final_norm else x3


def _ffn(x2d, za, zb, spad, wout_bf, nffn, wup_bf, cw, cb, wdown_bf, nfin,
         *, n_seq, group_mode, final_norm):
    n, d = x2d.shape
    f2 = wup_bf.shape[1]
    if group_mode:
        tm, grid = n, (1, 1)
        row = pl.BlockSpec((n, d), lambda b, t: (0, 0))
        seq = pl.BlockSpec((n_seq, SUBLANES, f2), lambda b, t: (0, 0, 0))
    else:
        t_len = n // n_seq
        tm = min(FFN_ROW_TILE, t_len)
        nt = t_len // tm
        grid = (n_seq, nt)
        row = pl.BlockSpec((tm, d), lambda b, t: (b * nt + t, 0))
        seq = pl.BlockSpec((1, SUBLANES, f2), lambda b, t: (b, 0, 0))
    return pl.pallas_call(
        functools.partial(_ffn_kernel, group_mode=group_mode, final_norm=final_norm),
        grid=grid,
        in_specs=[row, row, row, seq, _resident(wout_bf.shape), _resident((1, d)),
                  _resident(wup_bf.shape), _resident(cw.shape), _resident((1, f2)),
                  _resident(wdown_bf.shape), _resident((1, d))],
        out_specs=[row, seq],
        out_shape=[jax.ShapeDtypeStruct((n, d), F32), jax.ShapeDtypeStruct((n_seq, SUBLANES, f2), F32)],
        scratch_shapes=[pltpu.VMEM((SUBLANES, f2), F32), pltpu.VMEM((tm, f2 // 2), BF16)],
        compiler_params=_params(2),
        name="merge_ffn",
    )(x2d, za, zb, spad, wout_bf, nffn.reshape(1, d), wup_bf, cw, cb.reshape(1, f2),
      wdown_bf, nfin.reshape(1, d))


def _pad_state(buf):
    b, w1, c = buf.shape
    return jnp.concatenate([jnp.zeros((b, SUBLANES - w1, c), buf.dtype), buf], axis=1)


def kernel(x_prompt, x_sample, cache_k, cache_v, page_table, state_rglru_h, state_rglru_conv, state_ffn_conv, w_in, conv_w, conv_b, rg_w_a, rg_b_a, rg_w_x, rg_b_x, rg_a_param, lambda_q1, lambda_k1, lambda_q2, lambda_k2, subln_g, rel_bias, w_out, norm_attn, norm_ffn, w_up, ffn_conv_w, ffn_conv_b, w_down, norm_final):
    bsz, t_len, d = x_prompt.shape
    dbsz, dec_seq, _ = x_sample.shape
    depth = w_in.shape[0]
    n_heads = cache_k.shape[3]
    kv_dim = cache_k.shape[4]
    head_dim = kv_dim // 2
    page = cache_k.shape[2]
    past = page_table.shape[1] * page
    cw_w = conv_w.shape[1]
    fw_w = ffn_conv_w.shape[1]
    assert w_in.shape[2] == 6 * d and n_heads * kv_dim == d and rg_a_param.shape[1] == d
    assert dec_seq == SUBLANES and t_len % ATTN_BLOCK == 0 and page_table.shape[1] % PAGES_PER_STEP == 0
    assert ATTN_BLOCK >= MAX_DISTANCE and page >= MAX_DISTANCE and dec_seq * n_heads <= LANES
    assert kv_dim == LANES and n_heads == SUBLANES

    bias_p, bias_lp, bias_n = _bias_tiles(rel_bias, n_heads, ATTN_BLOCK, page, dec_seq)

    xp = x_prompt.reshape(bsz * t_len, d)
    xs = x_sample.reshape(dbsz * dec_seq, d)
    outs = [[] for _ in range(10)]
    for l in range(depth):
        lambda_init = 0.8 - 0.6 * math.exp(-0.3 * l)
        last = l == depth - 1
        w_in_bf = w_in[l].astype(BF16)
        wa_bf = rg_w_a[l].astype(BF16)
        wx_bf = rg_w_x[l].astype(BF16)
        wout_bf = w_out[l].astype(BF16)
        wup_bf = w_up[l].astype(BF16)
        wdown_bf = w_down[l].astype(BF16)
        lams = (lambda_q1[l], lambda_k1[l], lambda_q2[l], lambda_k2[l])

        def branch_a(xr, ga, spad, h0, n_seq, group_mode, pos_base):
            h0b = jnp.broadcast_to(h0[:, None, :], (n_seq, SUBLANES, d))
            return _rglru(xr, ga, spad, h0b, conv_w[l], conv_b[l], wa_bf, rg_b_a[l], wx_bf, rg_b_x[l],
                          rg_a_param[l], n_seq=n_seq, group_mode=group_mode, pos_base=pos_base)

        def channel_mix(x2d, za, zb, spad, n_seq, group_mode):
            return _ffn(x2d, za, zb, spad, wout_bf, norm_ffn[l], wup_bf, ffn_conv_w[l],
                        ffn_conv_b[l], wdown_bf, norm_final, n_seq=n_seq, group_mode=group_mode,
                        final_norm=last)

        xr, q2, k, kb, v, vt, ga, gb = _inproj(xp, norm_attn[l], w_in_bf, head_dim)
        za, hp = branch_a(xr, ga, jnp.zeros((bsz, SUBLANES, d), F32), jnp.zeros((bsz, d), F32),
                          bsz, False, 0)
        zb = _prompt_attn(q2, kb, vt, gb, bias_p, lams, subln_g[l], n_seq=bsz, n_heads=n_heads,
                          lambda_init=lambda_init)
        xp, fp = channel_mix(xp, za, zb, jnp.zeros((bsz, SUBLANES, ffn_conv_w.shape[2]), F32),
                             bsz, False)
        outs[0].append(k.reshape(bsz, t_len, n_heads, kv_dim))
        outs[1].append(v.reshape(bsz, t_len, n_heads, kv_dim))
        outs[4].append(hp[:, SUBLANES - 1])
        outs[6].append(xr.reshape(bsz, t_len, d)[:, t_len - (cw_w - 1):])
        outs[8].append(fp[:, SUBLANES - (fw_w - 1):])

        xr, q2, k, _, v, _, ga, gb = _inproj(xs, norm_attn[l], w_in_bf, head_dim)
        za, hs = branch_a(xr, ga, _pad_state(state_rglru_conv[l]), state_rglru_h[l], dbsz, True, past)
        zb = _sample_attn(page_table, q2.astype(F32), cache_k[l], cache_v[l],
                          _new_token_rows(k, dbsz, n_heads), _new_token_rows(v, dbsz, n_heads),
                          gb, bias_lp, bias_n, lams, subln_g[l], n_seq=dbsz, n_heads=n_heads,
                          lambda_init=lambda_init)
        xs, fs = channel_mix(xs, za, zb, _pad_state(state_ffn_conv[l]), dbsz, True)
        outs[2].append(k.reshape(dbsz, dec_seq, n_heads, kv_dim))
        outs[3].append(v.reshape(dbsz, dec_seq, n_heads, kv_dim))
        outs[5].append(hs[:, SUBLANES - 1])
        outs[7].append(xr.reshape(dbsz, dec_seq, d)[:, dec_seq - (cw_w - 1):])
        outs[9].append(fs[:, SUBLANES - (fw_w - 1):])

    return (xp.reshape(bsz, t_len, d), xs.reshape(dbsz, dec_seq, d),
            jnp.stack(outs[0]), jnp.stack(outs[1]), jnp.stack(outs[2]), jnp.stack(outs[3]),
            jnp.stack(outs[4]), jnp.stack(outs[5]), jnp.stack(outs[6]), jnp.stack(outs[7]),
            jnp.stack(outs[8]), jnp.stack(outs[9]))
```

```python
import functools
import math

import jax
import jax.numpy as jnp
from jax import lax
from jax.experimental import pallas as pl
from jax.experimental.pallas import tpu as pltpu

F32 = jnp.float32
BF16 = jnp.bfloat16

EPS = 1e-6
RG_C = 8.0
N_BUCKETS = 32
MAX_DISTANCE = 128
NEG = -1e30
LOG2E = math.log2(math.e)

SUBLANES = 8
LANES = 128
ROW_TILE = 256
FFN_ROW_TILE = 512
ATTN_BLOCK = 512
FFN_CHUNK = 512
PAGES_PER_STEP = 16
VMEM_LIMIT = 52 * 1024 * 1024


def _params(n_axes, vmem=VMEM_LIMIT):
    return pltpu.CompilerParams(
        dimension_semantics=("arbitrary",) * n_axes, vmem_limit_bytes=vmem)


def _resident(shape):
    nd = len(shape)
    return pl.BlockSpec(shape, lambda *_: (0,) * nd, pipeline_mode=pl.Buffered(1))


def _sigmoid(x):
    return 0.5 * (1.0 + jnp.tanh(0.5 * x))


def _rms(x, g):
    return x * lax.rsqrt(jnp.mean(x * x, axis=-1, keepdims=True) + EPS) * g


def _inproj_kernel(x_ref, g_ref, w_ref, xr_ref, q2_ref, k_ref, kb_ref, v_ref, vt_ref,
                   ga_ref, gb_ref, *, d, head_dim):
    h = _rms(x_ref[...], g_ref[...]).astype(BF16)

    def proj(j):
        return jnp.dot(h, w_ref[:, j * d:(j + 1) * d], preferred_element_type=F32)

    xr_ref[...] = proj(0)
    q = proj(1) * (head_dim ** -0.5 * LOG2E)
    lane = lax.broadcasted_iota(jnp.int32, q.shape, 1)
    first = (lane % (2 * head_dim)) < head_dim
    q2_ref[0] = jnp.where(first, q, 0.0).astype(BF16)
    q2_ref[1] = jnp.where(first, 0.0, q).astype(BF16)
    k = proj(2)
    k_ref[...] = k
    kb_ref[...] = k.astype(BF16)
    v = proj(3)
    v_ref[...] = v
    vt_ref[...] = v.T.astype(BF16)
    ga_ref[...] = _sigmoid(proj(4))
    gb_ref[...] = _sigmoid(proj(5))


def _inproj(x2d, g, w_bf, head_dim):
    n, d = x2d.shape
    tm = min(ROW_TILE, n)
    row = lambda i: (i, 0)
    blk = pl.BlockSpec((tm, d), row)
    f32 = jax.ShapeDtypeStruct((n, d), F32)
    return pl.pallas_call(
        functools.partial(_inproj_kernel, d=d, head_dim=head_dim),
        grid=(n // tm,),
        in_specs=[blk, _resident((1, d)), _resident(w_bf.shape)],
        out_specs=[blk, pl.BlockSpec((2, tm, d), lambda i: (0, i, 0)), blk, blk, blk,
                   pl.BlockSpec((d, tm), lambda i: (0, i)), blk, blk],
        out_shape=[f32, jax.ShapeDtypeStruct((2, n, d), BF16), f32,
                   jax.ShapeDtypeStruct((n, d), BF16), f32,
                   jax.ShapeDtypeStruct((d, n), BF16), f32, f32],
        compiler_params=_params(1),
        name="inproj",
    )(x2d, g.reshape(1, d), w_bf)


def _shifted(x3, prev3, d, t_idx):
    return jnp.where(t_idx >= d, pltpu.roll(x3, d, 1), pltpu.roll(prev3, d, 1))


def _causal_conv(x3, prev3, w, b):
    width = w.shape[0]
    t_idx = lax.broadcasted_iota(jnp.int32, x3.shape, 1)
    y = b + w[width - 1:width] * x3
    for dd in range(1, width):
        y = y + w[width - 1 - dd:width - dd] * _shifted(x3, prev3, dd, t_idx)
    return y


def _rglru_kernel(xr_ref, ga_ref, spad_ref, h0_ref, cw_ref, cb_ref, wa_ref, ba_ref, wx_ref, bx_ref,
                  ap_ref, za_ref, hlast_ref, carry_x, carry_h, a_scr, b_scr,
                  *, group_mode, pos_base, n_blocks):
    ti = pl.program_id(1)
    tt, c = xr_ref.shape
    g = tt // SUBLANES
    blk = c // n_blocks
    x3 = xr_ref[...].reshape(g, SUBLANES, c)

    if group_mode:
        prev3 = spad_ref[...]
    else:
        @pl.when(ti == 0)
        def _():
            carry_x[...] = spad_ref[0]
            carry_h[...] = h0_ref[0]
        prev3 = jnp.concatenate([carry_x[...][None], x3[:g - 1]], axis=0) if g > 1 else carry_x[...][None]

    xc3 = _causal_conv(x3, prev3, cw_ref[...], cb_ref[...])
    if not group_mode:
        carry_x[...] = x3[g - 1]

    xc = xc3.reshape(tt, c)
    xcb = xc.astype(BF16)

    def gate(w_ref, b_ref):
        parts = [jnp.dot(xcb[:, n * blk:(n + 1) * blk], w_ref[n], preferred_element_type=F32)
                 for n in range(n_blocks)]
        return jnp.concatenate(parts, axis=1) + b_ref[...]

    r = _sigmoid(gate(wa_ref, ba_ref))
    i = _sigmoid(gate(wx_ref, bx_ref))
    z = -ap_ref[...]
    softplus = jnp.maximum(z, 0.0) + jnp.log(1.0 + jnp.exp(-jnp.abs(z)))
    log_a = -RG_C * r * softplus
    a = jnp.exp(log_a)
    th = jnp.tanh(log_a)
    mult = jnp.sqrt(-2.0 * th / (1.0 - th))
    ix = i * xc
    a3 = a.reshape(g, SUBLANES, c)
    b3 = (mult * ix).reshape(g, SUBLANES, c)
    t_idx = lax.broadcasted_iota(jnp.int32, a3.shape, 1)
    if pos_base == 0:
        ix3 = ix.reshape(g, SUBLANES, c)
        if group_mode:
            reset = t_idx == 0
            a3 = jnp.where(reset, 0.0, a3)
            b3 = jnp.where(reset, ix3, b3)
        else:
            reset = (lax.broadcasted_iota(jnp.int32, (SUBLANES, c), 0) + ti) == 0
            rest = lambda x3: [x3[1:]] if g > 1 else []
            a3 = jnp.concatenate([jnp.where(reset, 0.0, a3[0])[None]] + rest(a3), axis=0)
            b3 = jnp.concatenate([jnp.where(reset, ix3[0], b3[0])[None]] + rest(b3), axis=0)

    for dd in (1, 2, 4):
        valid = t_idx >= dd
        b3 = jnp.where(valid, a3 * pltpu.roll(b3, dd, 1) + b3, b3)
        a3 = jnp.where(valid, a3 * pltpu.roll(a3, dd, 1), a3)

    if group_mode:
        h3 = a3 * h0_ref[...] + b3
        hlast_ref[...] = h3
    else:
        a_scr[...] = a3
        b_scr[...] = b3

        def body(gi, carry):
            hg = a_scr[gi] * carry + b_scr[gi]
            b_scr[gi] = hg
            return jnp.broadcast_to(hg[SUBLANES - 1:SUBLANES, :], (SUBLANES, c))

        last = lax.fori_loop(0, g, body, carry_h[...])
        carry_h[...] = last
        hlast_ref[0] = last
        h3 = b_scr[...]
    za_ref[...] = (ga_ref[...] * h3.reshape(tt, c)).astype(za_ref.dtype)


def _rglru(xr2d, ga, spad, h0b, cw, cb, wa_bf, ba, wx_bf, bx, ap, *, n_seq, group_mode, pos_base):
    n, c = xr2d.shape
    n_blocks = wa_bf.shape[0]
    if group_mode:
        tt, grid = n, (1, 1)
        row_spec = pl.BlockSpec((n, c), lambda b, t: (0, 0))
        seq_spec = pl.BlockSpec((n_seq, SUBLANES, c), lambda b, t: (0, 0, 0))
    else:
        t_len = n // n_seq
        tt = min(ROW_TILE, t_len)
        nt = t_len // tt
        grid = (n_seq, nt)
        row_spec = pl.BlockSpec((tt, c), lambda b, t: (b * nt + t, 0))
        seq_spec = pl.BlockSpec((1, SUBLANES, c), lambda b, t: (b, 0, 0))
    g = tt // SUBLANES
    vec = lambda a: a.reshape(1, c)
    return pl.pallas_call(
        functools.partial(_rglru_kernel, group_mode=group_mode, pos_base=pos_base, n_blocks=n_blocks),
        grid=grid,
        in_specs=[row_spec, row_spec, seq_spec, seq_spec, _resident(cw.shape), _resident((1, c)),
                  _resident(wa_bf.shape), _resident((1, c)), _resident(wx_bf.shape), _resident((1, c)),
                  _resident((1, c))],
        out_specs=[row_spec, seq_spec],
        out_shape=[jax.ShapeDtypeStruct((n, c), BF16), jax.ShapeDtypeStruct((n_seq, SUBLANES, c), F32)],
        scratch_shapes=[pltpu.VMEM((SUBLANES, c), F32), pltpu.VMEM((SUBLANES, c), F32),
                        pltpu.VMEM((g, SUBLANES, c), F32), pltpu.VMEM((g, SUBLANES, c), F32)],
        compiler_params=_params(2),
        name="rglru",
    )(xr2d, ga, spad, h0b, cw, vec(cb), wa_bf, vec(ba), wx_bf, vec(bx), vec(ap))


def _bias_of(dist, valid, rb_ref, h):
    n = jnp.maximum(dist, 0)
    max_exact = N_BUCKETS // 2
    nf = jnp.maximum(n, 1).astype(F32)
    large = max_exact + (jnp.log(nf / max_exact) / math.log(MAX_DISTANCE / max_exact)
                         * (N_BUCKETS - max_exact)).astype(jnp.int32)
    large = jnp.minimum(large, N_BUCKETS - 1)
    bucket = jnp.where(n < max_exact, n, large)
    base = rb_ref[N_BUCKETS - 1, h]
    out = jnp.zeros(dist.shape, F32)
    for b in range(N_BUCKETS - 1):
        out = jnp.where(bucket == b, (rb_ref[b, h] - base) * LOG2E, out)
    return jnp.where(valid, out, NEG)


def _bias_kernel(rb_ref, bp_ref, blp_ref, bn_ref, *, tb, n_heads, dec_seq):
    h = pl.program_id(0)
    j = lax.broadcasted_iota(jnp.int32, (LANES, LANES), 0)
    i = lax.broadcasted_iota(jnp.int32, (LANES, LANES), 1)
    for jb in range(2 * tb // LANES):
        for ib in range(tb // LANES):
            base = (ib - jb) * LANES + tb
            rows, cols = slice(jb * LANES, (jb + 1) * LANES), slice(ib * LANES, (ib + 1) * LANES)
            if base - (LANES - 1) >= MAX_DISTANCE:
                bp_ref[0, rows, cols] = jnp.zeros((LANES, LANES), F32)
            elif base + (LANES - 1) < 0:
                bp_ref[0, rows, cols] = jnp.full((LANES, LANES), NEG, F32)
            else:
                dist = base + i - j
                bp_ref[0, rows, cols] = _bias_of(dist, dist >= 0, rb_ref, h)
    rows, page = blp_ref.shape
    t = lax.broadcasted_iota(jnp.int32, (rows, page), 0) % dec_seq
    key = lax.broadcasted_iota(jnp.int32, (rows, page), 1)
    dist = page + t - key
    blp_ref[...] = _bias_of(dist, dist >= 0, rb_ref, h)
    t = lax.broadcasted_iota(jnp.int32, (rows, LANES), 0) % dec_seq
    lane = lax.broadcasted_iota(jnp.int32, (rows, LANES), 1)
    dist = t - lane // n_heads
    bn_ref[...] = _bias_of(dist, jnp.logical_and((lane % n_heads) == h, dist >= 0), rb_ref, h)


def _bias_tiles(rel_bias, n_heads, tb, page, dec_seq):
    rows = 2 * dec_seq
    return pl.pallas_call(
        functools.partial(_bias_kernel, tb=tb, n_heads=n_heads, dec_seq=dec_seq),
        grid=(n_heads,),
        in_specs=[pl.BlockSpec(memory_space=pltpu.SMEM)],
        out_specs=[pl.BlockSpec((1, 2 * tb, tb), lambda h: (h, 0, 0)),
                   pl.BlockSpec((rows, page), lambda h: (h, 0)),
                   pl.BlockSpec((rows, LANES), lambda h: (h, 0))],
        out_shape=[jax.ShapeDtypeStruct((n_heads, 2 * tb, tb), F32),
                   jax.ShapeDtypeStruct((n_heads * rows, page), F32),
                   jax.ShapeDtypeStruct((n_heads * rows, LANES), F32)],
        compiler_params=_params(1),
        name="bias_tiles",
    )(rel_bias)


def _lambda(lq1, lk1, lq2, lk2, lambda_init):
    s1 = jnp.sum(lq1 * lk1, axis=-1, keepdims=True)
    s2 = jnp.sum(lq2 * lk2, axis=-1, keepdims=True)
    return jnp.exp(s1) - jnp.exp(s2) + lambda_init


def _qk(a, b):
    return lax.dot_general(a, b, (((1,), (1,)), ((), ())), preferred_element_type=F32)


def _all_sublanes(x, op):
    for dd in (1, 2, 4):
        x = op(x, pltpu.roll(x, dd, 0))
    return x


def _prompt_attn_kernel(q2_ref, k_ref, vt_ref, gb_ref, bias_ref, lq1_ref, lk1_ref, lq2_ref, lk2_ref,
                        sg_ref, o_ref, s0_ref, s1_ref, m0_ref, m1_ref, l0_ref, l1_ref, a0_ref, a1_ref,
                        *, tb, lambda_init):
    e, t_len = vt_ref.shape
    nq = t_len // tb
    s_refs = (s0_ref, s1_ref)
    m_refs = (m0_ref, m1_ref)
    l_refs = (l0_ref, l1_ref)
    acc_refs = (a0_ref, a1_ref)

    def block(i):
        return pl.ds(i * tb if isinstance(i, int) else pl.multiple_of(i * tb, tb), tb)

    def init():
        for mm in range(2):
            m_refs[mm][...] = jnp.full(m_refs[mm].shape, NEG, F32)
            l_refs[mm][...] = jnp.zeros(l_refs[mm].shape, F32)
            acc_refs[mm][...] = jnp.zeros(acc_refs[mm].shape, F32)

    def scores(mm, qi, kj):
        return _qk(k_ref[block(kj), :], q2_ref[mm, block(qi), :])

    def consume(mm, kj, bias):
        s = s_refs[mm][...]
        if bias is not None:
            s = s + bias
        s3 = s.reshape(tb // SUBLANES, SUBLANES, tb)
        m_prev = m_refs[mm][...]
        m_new = jnp.maximum(m_prev, _all_sublanes(jnp.max(s3, axis=0), jnp.maximum))
        alpha = jnp.exp2(m_prev - m_new)
        p3 = jnp.exp2(s3 - m_new[None])
        l_refs[mm][...] = alpha * l_refs[mm][...] + _all_sublanes(jnp.sum(p3, axis=0), jnp.add)
        pv = jnp.dot(vt_ref[:, block(kj)], p3.reshape(tb, tb).astype(BF16),
                     preferred_element_type=F32)
        acc = acc_refs[mm][...].reshape(e // SUBLANES, SUBLANES, tb) * alpha[None]
        acc_refs[mm][...] = acc.reshape(e, tb) + pv
        m_refs[mm][...] = m_new

    def step(qi, kj, bias, next_q, next_k):
        s1_ref[...] = scores(1, qi, kj)
        consume(0, kj, bias)
        s0_ref[...] = scores(0, next_q, next_k)
        consume(1, kj, bias)

    lam = _lambda(lq1_ref[...], lk1_ref[...], lq2_ref[...], lk2_ref[...], lambda_init)

    def finalize(qi):
        def normalised(mm):
            acc = acc_refs[mm][...].reshape(e // SUBLANES, SUBLANES, tb)
            return (acc / l_refs[mm][...][None]).reshape(e, tb)

        o = (normalised(0) - lam * normalised(1)).T
        y = _rms(o, sg_ref[...]) * (1.0 - lambda_init)
        o_ref[block(qi), :] = (gb_ref[block(qi), :] * y).astype(o_ref.dtype)

    bias_prev = lambda: bias_ref[0, :tb, :]
    bias_diag = lambda: bias_ref[0, tb:, :]

    init()
    s0_ref[...] = scores(0, 0, 0)
    step(0, 0, bias_diag(), 1 if nq > 1 else 0, 0)
    finalize(0)

    def tile(qi, carry):
        init()
        n_far = qi - 1
        odd = n_far & 1

        @pl.when(odd == 1)
        def _():
            step(qi, 0, None, qi, 1)

        def far_pair(jj, c):
            j = odd + 2 * jj
            step(qi, j, None, qi, j + 1)
            step(qi, j + 1, None, qi, j + 2)
            return c

        lax.fori_loop(0, lax.shift_right_logical(n_far, 1), far_pair, 0)
        step(qi, qi - 1, bias_prev(), qi, qi)
        step(qi, qi, bias_diag(), jnp.minimum(qi + 1, nq - 1), 0)
        finalize(qi)
        return carry

    lax.fori_loop(1, nq, tile, 0)


def _prompt_attn(q2, kb, vt, gb, bias_p, lams, subln_g, *, n_seq, n_heads, lambda_init):
    _, n, d = q2.shape
    e = d // n_heads
    t_len = n // n_seq
    tb = ATTN_BLOCK
    hd = lams[0].shape[-1]
    small = pl.BlockSpec((1, hd), lambda b, h: (0, 0))
    rows = pl.BlockSpec((t_len, e), lambda b, h: (b, h))
    stat = pltpu.VMEM((SUBLANES, tb), F32)
    return pl.pallas_call(
        functools.partial(_prompt_attn_kernel, tb=tb, lambda_init=lambda_init),
        grid=(n_seq, n_heads),
        in_specs=[pl.BlockSpec((2, t_len, e), lambda b, h: (0, b, h)),
                  rows,
                  pl.BlockSpec((e, t_len), lambda b, h: (h, b)),
                  rows,
                  pl.BlockSpec((1, 2 * tb, tb), lambda b, h: (h, 0, 0)),
                  small, small, small, small,
                  pl.BlockSpec((1, e), lambda b, h: (0, 0))],
        out_specs=rows,
        out_shape=jax.ShapeDtypeStruct((n, d), BF16),
        scratch_shapes=[pltpu.VMEM((tb, tb), F32), pltpu.VMEM((tb, tb), F32),
                        stat, stat, stat, stat,
                        pltpu.VMEM((e, tb), F32), pltpu.VMEM((e, tb), F32)],
        compiler_params=_params(2),
        name="prompt_attn",
    )(q2, kb, vt, gb, bias_p, *[x.reshape(1, hd) for x in lams], subln_g.reshape(1, e))


def _as_column(stat):
    r = stat.shape[1]
    row = lax.broadcasted_iota(jnp.int32, (r, r), 0)
    col = lax.broadcasted_iota(jnp.int32, (r, r), 1)
    full = jnp.concatenate([stat] * (r // SUBLANES), axis=0)
    return jnp.sum(jnp.where(row == col, full, 0.0), axis=1, keepdims=True)


def _sample_attn_kernel(pt_ref, q2_ref, *refs, n_heads, dec_seq, pps, lambda_init):
    kp = refs[:pps]
    vp = refs[pps:2 * pps]
    (kn_ref, vn_ref, gb_ref, blp_ref, bn_ref, lq1_ref, lk1_ref, lq2_ref, lk2_ref, sg_ref,
     o_ref, q_ref, qt_ref, blpt_ref, m_ref, l_ref, acc_ref) = refs[2 * pps:]
    del pt_ref
    c = pl.program_id(1)
    nc = pl.num_programs(1)
    e = kp[0].shape[1]
    d = n_heads * e
    page = kp[0].shape[0] // n_heads
    rows = 2 * dec_seq

    @pl.when(c == 0)
    def _():
        lane_head = lax.broadcasted_iota(jnp.int32, (dec_seq, d), 1) // e
        q_ref[...] = jnp.concatenate(
            [q2_ref[mm][:, h * e:(h + 1) * e] for h in range(n_heads) for mm in range(2)], axis=0)
        qbd = jnp.concatenate(
            [jnp.where(lane_head == h, q2_ref[mm], 0.0) for h in range(n_heads) for mm in range(2)],
            axis=0)
        qt_ref[...] = qbd.T
        blpt_ref[...] = blp_ref[...].T
        m_ref[...] = jnp.full(m_ref.shape, NEG, F32)
        l_ref[...] = jnp.zeros(l_ref.shape, F32)
        acc_ref[...] = jnp.zeros(acc_ref.shape, F32)

    def head_rows(ref, h):
        return ref[pl.ds(h, page, stride=n_heads), :]

    is_last = (c == nc - 1).astype(F32)
    parts = []
    for j in range(pps):
        keys = jnp.concatenate([head_rows(kp[j], h) for h in range(n_heads)], axis=1)
        parts.append(jnp.dot(keys, qt_ref[...], preferred_element_type=F32))
    parts[-1] = parts[-1] + blpt_ref[...] * is_last
    s3 = jnp.concatenate(parts, axis=0).reshape(pps * page // SUBLANES, SUBLANES, -1)
    m_prev = m_ref[...]
    m_new = jnp.maximum(m_prev, _all_sublanes(jnp.max(s3, axis=0), jnp.maximum))
    alpha = jnp.exp2(m_prev - m_new)
    p3 = jnp.exp2(s3 - m_new[None])
    l_ref[...] = alpha * l_ref[...] + _all_sublanes(jnp.sum(p3, axis=0), jnp.add)
    m_ref[...] = m_new
    p = p3.reshape(pps * page, -1).T
    out = []
    for h in range(n_heads):
        ph = p[h * rows:(h + 1) * rows]
        pv = jnp.dot(ph[:, :page], head_rows(vp[0], h), preferred_element_type=F32)
        for j in range(1, pps):
            pv = pv + jnp.dot(ph[:, j * page:(j + 1) * page], head_rows(vp[j], h),
                              preferred_element_type=F32)
        out.append(pv)
    acc_ref[...] = _as_column(alpha) * acc_ref[...] + jnp.concatenate(out, axis=0)

    @pl.when(c == nc - 1)
    def _():
        s = _qk(q_ref[...], kn_ref[...]) + bn_ref[...]
        m_old = _as_column(m_ref[...])
        m_fin = jnp.maximum(m_old, jnp.max(s, axis=-1, keepdims=True))
        scale = jnp.exp2(m_old - m_fin)
        pn = jnp.exp2(s - m_fin)
        l_fin = scale * _as_column(l_ref[...]) + jnp.sum(pn, axis=-1, keepdims=True)
        acc = scale * acc_ref[...] + jnp.dot(pn, vn_ref[...], preferred_element_type=F32)
        lam = _lambda(lq1_ref[...], lk1_ref[...], lq2_ref[...], lk2_ref[...], lambda_init)
        o_all = acc / l_fin
        outs = []
        for h in range(n_heads):
            r0 = h * rows
            o = o_all[r0:r0 + dec_seq] - lam * o_all[r0 + dec_seq:r0 + rows]
            outs.append(_rms(o, sg_ref[...]) * (1.0 - lambda_init))
        o_ref[...] = gb_ref[...] * jnp.concatenate(outs, axis=1)


def _sample_attn(page_table, q2f, cache_k, cache_v, kn_rows, vn_rows, gb, bias_lp, bias_n, lams,
                 subln_g, *, n_seq, n_heads, lambda_init):
    _, n, d = q2f.shape
    dec_seq = n // n_seq
    e = d // n_heads
    n_pool, page = cache_k.shape[:2]
    n_pages = page_table.shape[1]
    pps = PAGES_PER_STEP
    nc = n_pages // pps
    hd = lams[0].shape[-1]
    rows = n_heads * 2 * dec_seq
    width = page * n_heads

    def page_spec(j):
        return pl.BlockSpec((None, width, e), lambda b, c, pt: (pt[b, c * pps + j], 0, 0))

    const2 = lambda b, c, pt: (0, 0)
    small = pl.BlockSpec((1, hd), const2)
    new_rows = pl.BlockSpec((LANES, e), lambda b, c, pt: (b, 0))
    seq = pl.BlockSpec((dec_seq, d), lambda b, c, pt: (b, 0))
    grid_spec = pltpu.PrefetchScalarGridSpec(
        num_scalar_prefetch=1,
        grid=(n_seq, nc),
        in_specs=[pl.BlockSpec((2, dec_seq, d), lambda b, c, pt: (0, b, 0))]
                 + [page_spec(j) for j in range(pps)] + [page_spec(j) for j in range(pps)]
                 + [new_rows, new_rows, seq,
                    pl.BlockSpec((rows, page), const2), pl.BlockSpec((rows, LANES), const2),
                    small, small, small, small, pl.BlockSpec((1, e), const2)],
        out_specs=seq,
        scratch_shapes=[pltpu.VMEM((rows, e), F32), pltpu.VMEM((d, rows), F32),
                        pltpu.VMEM((page, rows), F32), pltpu.VMEM((SUBLANES, rows), F32),
                        pltpu.VMEM((SUBLANES, rows), F32), pltpu.VMEM((rows, e), F32)],
    )
    ck = cache_k.reshape(n_pool, width, e)
    cv = cache_v.reshape(n_pool, width, e)
    return pl.pallas_call(
        functools.partial(_sample_attn_kernel, n_heads=n_heads, dec_seq=dec_seq, pps=pps,
                          lambda_init=lambda_init),
        grid_spec=grid_spec,
        out_shape=jax.ShapeDtypeStruct((n, d), F32),
        compiler_params=_params(2),
        name="sample_attn",
    )(page_table, q2f, *([ck] * pps), *([cv] * pps), kn_rows, vn_rows, gb, bias_lp, bias_n,
      *[x.reshape(1, hd) for x in lams], subln_g.reshape(1, e))


def _new_token_rows(x2d, n_seq, n_heads):
    n, d = x2d.shape
    e = d // n_heads
    per_seq = (n // n_seq) * n_heads
    x = x2d.reshape(n_seq, per_seq, e)
    x = jnp.pad(x, ((0, 0), (0, LANES - per_seq), (0, 0)))
    return x.reshape(n_seq * LANES, e)


def _gelu_tanh(x):
    return 0.5 * x * (1.0 + jnp.tanh(math.sqrt(2.0 / math.pi) * (x + 0.044715 * (x * x * x))))


def _ffn_kernel(x_ref, za_ref, zb_ref, spad_ref, wout_ref, nffn_ref, wup_ref,
                cw_ref, cb_ref, wdown_ref, nfin_ref, y_ref, tail_ref, carry_ref, act_ref,
                *, group_mode, final_norm):
    ti = pl.program_id(1)
    tm, d = x_ref.shape
    g = tm // SUBLANES
    d_ff = wdown_ref.shape[0]

    merged = (za_ref[...].astype(F32) + zb_ref[...].astype(F32)).astype(BF16)
    x2 = x_ref[...] + jnp.dot(merged, wout_ref[...], preferred_element_type=F32)
    h2 = _rms(x2, nffn_ref[...]).astype(BF16)

    if not group_mode:
        @pl.when(ti == 0)
        def _():
            carry_ref[...] = spad_ref[0]

    def up_conv(cols):
        up3 = jnp.dot(h2, wup_ref[:, cols], preferred_element_type=F32).reshape(g, SUBLANES, -1)
        if group_mode:
            prev3 = spad_ref[:, :, cols]
            tail_ref[:, :, cols] = up3
        else:
            first = carry_ref[:, cols][None]
            prev3 = jnp.concatenate([first, up3[:g - 1]], axis=0) if g > 1 else first
            carry_ref[:, cols] = up3[g - 1]
            tail_ref[0, :, cols] = up3[g - 1]
        return _causal_conv(up3, prev3, cw_ref[:, cols], cb_ref[:, cols]).reshape(tm, -1)

    for c0 in range(0, d_ff, FFN_CHUNK):
        gate = up_conv(slice(c0, c0 + FFN_CHUNK))
        val = up_conv(slice(d_ff + c0, d_ff + c0 + FFN_CHUNK))
        act_ref[:, c0:c0 + FFN_CHUNK] = (_gelu_tanh(gate) * val).astype(BF16)
    x3 = x2 + jnp.dot(act_ref[...], wdown_ref[...], preferred_element_type=F32)
    y_ref[...] = _rms(x3, nfin_ref[...]) if final_norm else x3


def _ffn(x2d, za, zb, spad, wout_bf, nffn, wup_bf, cw, cb, wdown_bf, nfin,
         *, n_seq, group_mode, final_norm):
    n, d = x2d.shape
    f2 = wup_bf.shape[1]
    if group_mode:
        tm, grid = n, (1, 1)
        row = pl.BlockSpec((n, d), lambda b, t: (0, 0))
        seq = pl.BlockSpec((n_seq, SUBLANES, f2), lambda b, t: (0, 0, 0))
    else:
        t_len = n // n_seq
        tm = min(FFN_ROW_TILE, t_len)
        nt = t_len // tm
        grid = (n_seq, nt)
        row = pl.BlockSpec((tm, d), lambda b, t: (b * nt + t, 0))
        seq = pl.BlockSpec((1, SUBLANES, f2), lambda b, t: (b, 0, 0))
    return pl.pallas_call(
        functools.partial(_ffn_kernel, group_mode=group_mode, final_norm=final_norm),
        grid=grid,
        in_specs=[row, row, row, seq, _resident(wout_bf.shape), _resident((1, d)),
                  _resident(wup_bf.shape), _resident(cw.shape), _resident((1, f2)),
                  _resident(wdown_bf.shape), _resident((1, d))],
        out_specs=[row, seq],
        out_shape=[jax.ShapeDtypeStruct((n, d), F32), jax.ShapeDtypeStruct((n_seq, SUBLANES, f2), F32)],
        scratch_shapes=[pltpu.VMEM((SUBLANES, f2), F32), pltpu.VMEM((tm, f2 // 2), BF16)],
        compiler_params=_params(2),
        name="merge_ffn",
    )(x2d, za, zb, spad, wout_bf, nffn.reshape(1, d), wup_bf, cw, cb.reshape(1, f2),
      wdown_bf, nfin.reshape(1, d))


def _pad_state(buf):
    b, w1, c = buf.shape
    return jnp.concatenate([jnp.zeros((b, SUBLANES - w1, c), buf.dtype), buf], axis=1)


def kernel(x_prompt, x_sample, cache_k, cache_v, page_table, state_rglru_h, state_rglru_conv, state_ffn_conv, w_in, conv_w, conv_b, rg_w_a, rg_b_a, rg_w_x, rg_b_x, rg_a_param, lambda_q1, lambda_k1, lambda_q2, lambda_k2, subln_g, rel_bias, w_out, norm_attn, norm_ffn, w_up, ffn_conv_w, ffn_conv_b, w_down, norm_final):
    bsz, t_len, d = x_prompt.shape
    dbsz, dec_seq, _ = x_sample.shape
    depth = w_in.shape[0]
    n_heads = cache_k.shape[3]
    kv_dim = cache_k.shape[4]
    head_dim = kv_dim // 2
    page = cache_k.shape[2]
    past = page_table.shape[1] * page
    cw_w = conv_w.shape[1]
    fw_w = ffn_conv_w.shape[1]
    assert w_in.shape[2] == 6 * d and n_heads * kv_dim == d and rg_a_param.shape[1] == d
    assert dec_seq == SUBLANES and t_len % ATTN_BLOCK == 0 and page_table.shape[1] % PAGES_PER_STEP == 0
    assert ATTN_BLOCK >= MAX_DISTANCE and page >= MAX_DISTANCE and dec_seq * n_heads <= LANES
    assert kv_dim == LANES and n_heads == SUBLANES

    bias_p, bias_lp, bias_n = _bias_tiles(rel_bias, n_heads, ATTN_BLOCK, page, dec_seq)

    xp = x_prompt.reshape(bsz * t_len, d)
    xs = x_sample.reshape(dbsz * dec_seq, d)
    outs = [[] for _ in range(10)]
    for l in range(depth):
        lambda_init = 0.8 - 0.6 * math.exp(-0.3 * l)
        last = l == depth - 1
        w_in_bf = w_in[l].astype(BF16)
        wa_bf = rg_w_a[l].astype(BF16)
        wx_bf = rg_w_x[l].astype(BF16)
        wout_bf = w_out[l].astype(BF16)
        wup_bf = w_up[l].astype(BF16)
        wdown_bf = w_down[l].astype(BF16)
        lams = (lambda_q1[l], lambda_k1[l], lambda_q2[l], lambda_k2[l])

        def branch_a(xr, ga, spad, h0, n_seq, group_mode, pos_base):
            h0b = jnp.broadcast_to(h0[:, None, :], (n_seq, SUBLANES, d))
            return _rglru(xr, ga, spad, h0b, conv_w[l], conv_b[l], wa_bf, rg_b_a[l], wx_bf, rg_b_x[l],
                          rg_a_param[l], n_seq=n_seq, group_mode=group_mode, pos_base=pos_base)

        def channel_mix(x2d, za, zb, spad, n_seq, group_mode):
            return _ffn(x2d, za, zb, spad, wout_bf, norm_ffn[l], wup_bf, ffn_conv_w[l],
                        ffn_conv_b[l], wdown_bf, norm_final, n_seq=n_seq, group_mode=group_mode,
                        final_norm=last)

        xr, q2, k, kb, v, vt, ga, gb = _inproj(xp, norm_attn[l], w_in_bf, head_dim)
        za, hp = branch_a(xr, ga, jnp.zeros((bsz, SUBLANES, d), F32), jnp.zeros((bsz, d), F32),
                          bsz, False, 0)
        zb = _prompt_attn(q2, kb, vt, gb, bias_p, lams, subln_g[l], n_seq=bsz, n_heads=n_heads,
                          lambda_init=lambda_init)
        xp, fp = channel_mix(xp, za, zb, jnp.zeros((bsz, SUBLANES, ffn_conv_w.shape[2]), F32),
                             bsz, False)
        outs[0].append(k.reshape(bsz, t_len, n_heads, kv_dim))
        outs[1].append(v.reshape(bsz, t_len, n_heads, kv_dim))
        outs[4].append(hp[:, SUBLANES - 1])
        outs[6].append(xr.reshape(bsz, t_len, d)[:, t_len - (cw_w - 1):])
        outs[8].append(fp[:, SUBLANES - (fw_w - 1):])

        xr, q2, k, _, v, _, ga, gb = _inproj(xs, norm_attn[l], w_in_bf, head_dim)
        za, hs = branch_a(xr, ga, _pad_state(state_rglru_conv[l]), state_rglru_h[l], dbsz, True, past)
        zb = _sample_attn(page_table, q2.astype(F32), cache_k[l], cache_v[l],
                          _new_token_rows(k, dbsz, n_heads), _new_token_rows(v, dbsz, n_heads),
                          gb, bias_lp, bias_n, lams, subln_g[l], n_seq=dbsz, n_heads=n_heads,
                          lambda_init=lambda_init)
        xs, fs = channel_mix(xs, za, zb, _pad_state(state_ffn_conv[l]), dbsz, True)
        outs[2].append(k.reshape(dbsz, dec_seq, n_heads, kv_dim))
        outs[3].append(v.reshape(dbsz, dec_seq, n_heads, kv_dim))
        outs[5].append(hs[:, SUBLANES - 1])
        outs[7].append(xr.reshape(dbsz, dec_seq, d)[:, dec_seq - (cw_w - 1):])
        outs[9].append(fs[:, SUBLANES - (fw_w - 1):])

    return (xp.reshape(bsz, t_len, d), xs.reshape(dbsz, dec_seq, d),
            jnp.stack(outs[0]), jnp.stack(outs[1]), jnp.stack(outs[2]), jnp.stack(outs[3]),
            jnp.stack(outs[4]), jnp.stack(outs[5]), jnp.stack(outs[6]), jnp.stack(outs[7]),
            jnp.stack(outs[8]), jnp.stack(outs[9]))
```

```python
import functools
import math

import jax
import jax.numpy as jnp
from jax import lax
from jax.experimental import pallas as pl
from jax.experimental.pallas import tpu as pltpu

F32 = jnp.float32
BF16 = jnp.bfloat16

EPS = 1e-6
RG_C = 8.0
N_BUCKETS = 32
MAX_DISTANCE = 128
NEG = -1e30
LOG2E = math.log2(math.e)

SUBLANES = 8
LANES = 128
ROW_TILE = 256
FFN_ROW_TILE = 512
ATTN_BLOCK = 512
FFN_CHUNK = 512
PAGES_PER_STEP = 16
VMEM_LIMIT = 52 * 1024 * 1024


def _params(n_axes, vmem=VMEM_LIMIT):
    return pltpu.CompilerParams(
        dimension_semantics=("arbitrary",) * n_axes, vmem_limit_bytes=vmem)


def _resident(shape):
    nd = len(shape)
    return pl.BlockSpec(shape, lambda *_: (0,) * nd, pipeline_mode=pl.Buffered(1))


def _sigmoid(x):
    return 0.5 * (1.0 + jnp.tanh(0.5 * x))


def _rms(x, g):
    return x * lax.rsqrt(jnp.mean(x * x, axis=-1, keepdims=True) + EPS) * g


def _inproj_kernel(x_ref, g_ref, w_ref, xr_ref, q2_ref, k_ref, kb_ref, v_ref, vt_ref,
                   ga_ref, gb_ref, *, d, head_dim):
    h = _rms(x_ref[...], g_ref[...]).astype(BF16)

    def proj(j):
        return jnp.dot(h, w_ref[:, j * d:(j + 1) * d], preferred_element_type=F32)

    xr_ref[...] = proj(0)
    q = proj(1) * (head_dim ** -0.5 * LOG2E)
    lane = lax.broadcasted_iota(jnp.int32, q.shape, 1)
    first = (lane % (2 * head_dim)) < head_dim
    q2_ref[0] = jnp.where(first, q, 0.0).astype(BF16)
    q2_ref[1] = jnp.where(first, 0.0, q).astype(BF16)
    k = proj(2)
    k_ref[...] = k
    kb_ref[...] = k.astype(BF16)
    v = proj(3)
    v_ref[...] = v
    vt_ref[...] = v.T.astype(BF16)
    ga_ref[...] = _sigmoid(proj(4))
    gb_ref[...] = _sigmoid(proj(5))


def _inproj(x2d, g, w_bf, head_dim):
    n, d = x2d.shape
    tm = min(ROW_TILE, n)
    row = lambda i: (i, 0)
    blk = pl.BlockSpec((tm, d), row)
    f32 = jax.ShapeDtypeStruct((n, d), F32)
    return pl.pallas_call(
        functools.partial(_inproj_kernel, d=d, head_dim=head_dim),
        grid=(n // tm,),
        in_specs=[blk, _resident((1, d)), _resident(w_bf.shape)],
        out_specs=[blk, pl.BlockSpec((2, tm, d), lambda i: (0, i, 0)), blk, blk, blk,
                   pl.BlockSpec((d, tm), lambda i: (0, i)), blk, blk],
        out_shape=[f32, jax.ShapeDtypeStruct((2, n, d), BF16), f32,
                   jax.ShapeDtypeStruct((n, d), BF16), f32,
                   jax.ShapeDtypeStruct((d, n), BF16), f32, f32],
        compiler_params=_params(1),
        name="inproj",
    )(x2d, g.reshape(1, d), w_bf)


def _shifted(x3, prev3, d, t_idx):
    return jnp.where(t_idx >= d, pltpu.roll(x3, d, 1), pltpu.roll(prev3, d, 1))


def _causal_conv(x3, prev3, w, b):
    width = w.shape[0]
    t_idx = lax.broadcasted_iota(jnp.int32, x3.shape, 1)
    y = b + w[width - 1:width] * x3
    for dd in range(1, width):
        y = y + w[width - 1 - dd:width - dd] * _shifted(x3, prev3, dd, t_idx)
    return y


def _rglru_kernel(xr_ref, ga_ref, spad_ref, h0_ref, cw_ref, cb_ref, wa_ref, ba_ref, wx_ref, bx_ref,
                  ap_ref, za_ref, hlast_ref, carry_x, carry_h, a_scr, b_scr,
                  *, group_mode, pos_base, n_blocks):
    ti = pl.program_id(1)
    tt, c = xr_ref.shape
    g = tt // SUBLANES
    blk = c // n_blocks
    x3 = xr_ref[...].reshape(g, SUBLANES, c)

    if group_mode:
        prev3 = spad_ref[...]
    else:
        @pl.when(ti == 0)
        def _():
            carry_x[...] = spad_ref[0]
            carry_h[...] = h0_ref[0]
        prev3 = jnp.concatenate([carry_x[...][None], x3[:g - 1]], axis=0) if g > 1 else carry_x[...][None]

    xc3 = _causal_conv(x3, prev3, cw_ref[...], cb_ref[...])
    if not group_mode:
        carry_x[...] = x3[g - 1]

    xc = xc3.reshape(tt, c)
    xcb = xc.astype(BF16)

    def gate(w_ref, b_ref):
        parts = [jnp.dot(xcb[:, n * blk:(n + 1) * blk], w_ref[n], preferred_element_type=F32)
                 for n in range(n_blocks)]
        return jnp.concatenate(parts, axis=1) + b_ref[...]

    r = _sigmoid(gate(wa_ref, ba_ref))
    i = _sigmoid(gate(wx_ref, bx_ref))
    z = -ap_ref[...]
    softplus = jnp.maximum(z, 0.0) + jnp.log(1.0 + jnp.exp(-jnp.abs(z)))
    log_a = -RG_C * r * softplus
    a = jnp.exp(log_a)
    th = jnp.tanh(log_a)
    mult = jnp.sqrt(-2.0 * th / (1.0 - th))
    ix = i * xc
    a3 = a.reshape(g, SUBLANES, c)
    b3 = (mult * ix).reshape(g, SUBLANES, c)
    t_idx = lax.broadcasted_iota(jnp.int32, a3.shape, 1)
    if pos_base == 0:
        ix3 = ix.reshape(g, SUBLANES, c)
        if group_mode:
            reset = t_idx == 0
            a3 = jnp.where(reset, 0.0, a3)
            b3 = jnp.where(reset, ix3, b3)
        else:
            reset = (lax.broadcasted_iota(jnp.int32, (SUBLANES, c), 0) + ti) == 0
            rest = lambda x3: [x3[1:]] if g > 1 else []
            a3 = jnp.concatenate([jnp.where(reset, 0.0, a3[0])[None]] + rest(a3), axis=0)
            b3 = jnp.concatenate([jnp.where(reset, ix3[0], b3[0])[None]] + rest(b3), axis=0)

    for dd in (1, 2, 4):
        valid = t_idx >= dd
        b3 = jnp.where(valid, a3 * pltpu.roll(b3, dd, 1) + b3, b3)
        a3 = jnp.where(valid, a3 * pltpu.roll(a3, dd, 1), a3)

    if group_mode:
        h3 = a3 * h0_ref[...] + b3
        hlast_ref[...] = h3
    else:
        a_scr[...] = a3
        b_scr[...] = b3

        def body(gi, carry):
            hg = a_scr[gi] * carry + b_scr[gi]
            b_scr[gi] = hg
            return jnp.broadcast_to(hg[SUBLANES - 1:SUBLANES, :], (SUBLANES, c))

        last = lax.fori_loop(0, g, body, carry_h[...])
        carry_h[...] = last
        hlast_ref[0] = last
        h3 = b_scr[...]
    za_ref[...] = (ga_ref[...] * h3.reshape(tt, c)).astype(za_ref.dtype)


def _rglru(xr2d, ga, spad, h0b, cw, cb, wa_bf, ba, wx_bf, bx, ap, *, n_seq, group_mode, pos_base):
    n, c = xr2d.shape
    n_blocks = wa_bf.shape[0]
    if group_mode:
        tt, grid = n, (1, 1)
        row_spec = pl.BlockSpec((n, c), lambda b, t: (0, 0))
        seq_spec = pl.BlockSpec((n_seq, SUBLANES, c), lambda b, t: (0, 0, 0))
    else:
        t_len = n // n_seq
        tt = min(ROW_TILE, t_len)
        nt = t_len // tt
        grid = (n_seq, nt)
        row_spec = pl.BlockSpec((tt, c), lambda b, t: (b * nt + t, 0))
        seq_spec = pl.BlockSpec((1, SUBLANES, c), lambda b, t: (b, 0, 0))
    g = tt // SUBLANES
    vec = lambda a: a.reshape(1, c)
    return pl.pallas_call(
        functools.partial(_rglru_kernel, group_mode=group_mode, pos_base=pos_base, n_blocks=n_blocks),
        grid=grid,
        in_specs=[row_spec, row_spec, seq_spec, seq_spec, _resident(cw.shape), _resident((1, c)),
                  _resident(wa_bf.shape), _resident((1, c)), _resident(wx_bf.shape), _resident((1, c)),
                  _resident((1, c))],
        out_specs=[row_spec, seq_spec],
        out_shape=[jax.ShapeDtypeStruct((n, c), BF16), jax.ShapeDtypeStruct((n_seq, SUBLANES, c), F32)],
        scratch_shapes=[pltpu.VMEM((SUBLANES, c), F32), pltpu.VMEM((SUBLANES, c), F32),
                        pltpu.VMEM((g, SUBLANES, c), F32), pltpu.VMEM((g, SUBLANES, c), F32)],
        compiler_params=_params(2),
        name="rglru",
    )(xr2d, ga, spad, h0b, cw, vec(cb), wa_bf, vec(ba), wx_bf, vec(bx), vec(ap))


def _bias_of(dist, valid, rb_ref, h):
    n = jnp.maximum(dist, 0)
    max_exact = N_BUCKETS // 2
    nf = jnp.maximum(n, 1).astype(F32)
    large = max_exact + (jnp.log(nf / max_exact) / math.log(MAX_DISTANCE / max_exact)
                         * (N_BUCKETS - max_exact)).astype(jnp.int32)
    large = jnp.minimum(large, N_BUCKETS - 1)
    bucket = jnp.where(n < max_exact, n, large)
    base = rb_ref[N_BUCKETS - 1, h]
    out = jnp.zeros(dist.shape, F32)
    for b in range(N_BUCKETS - 1):
        out = jnp.where(bucket == b, (rb_ref[b, h] - base) * LOG2E, out)
    return jnp.where(valid, out, NEG)


def _bias_kernel(rb_ref, bp_ref, blp_ref, bn_ref, *, tb, n_heads, dec_seq):
    h = pl.program_id(0)
    j = lax.broadcasted_iota(jnp.int32, (LANES, LANES), 0)
    i = lax.broadcasted_iota(jnp.int32, (LANES, LANES), 1)
    for jb in range(2 * tb // LANES):
        for ib in range(tb // LANES):
            base = (ib - jb) * LANES + tb
            rows, cols = slice(jb * LANES, (jb + 1) * LANES), slice(ib * LANES, (ib + 1) * LANES)
            if base - (LANES - 1) >= MAX_DISTANCE:
                bp_ref[0, rows, cols] = jnp.zeros((LANES, LANES), F32)
            elif base + (LANES - 1) < 0:
                bp_ref[0, rows, cols] = jnp.full((LANES, LANES), NEG, F32)
            else:
                dist = base + i - j
                bp_ref[0, rows, cols] = _bias_of(dist, dist >= 0, rb_ref, h)
    rows, page = blp_ref.shape
    t = lax.broadcasted_iota(jnp.int32, (rows, page), 0) % dec_seq
    key = lax.broadcasted_iota(jnp.int32, (rows, page), 1)
    dist = page + t - key
    blp_ref[...] = _bias_of(dist, dist >= 0, rb_ref, h)
    t = lax.broadcasted_iota(jnp.int32, (rows, LANES), 0) % dec_seq
    lane = lax.broadcasted_iota(jnp.int32, (rows, LANES), 1)
    dist = t - lane // n_heads
    bn_ref[...] = _bias_of(dist, jnp.logical_and((lane % n_heads) == h, dist >= 0), rb_ref, h)


def _bias_tiles(rel_bias, n_heads, tb, page, dec_seq):
    rows = 2 * dec_seq
    return pl.pallas_call(
        functools.partial(_bias_kernel, tb=tb, n_heads=n_heads, dec_seq=dec_seq),
        grid=(n_heads,),
        in_specs=[pl.BlockSpec(memory_space=pltpu.SMEM)],
        out_specs=[pl.BlockSpec((1, 2 * tb, tb), lambda h: (h, 0, 0)),
                   pl.BlockSpec((rows, page), lambda h: (h, 0)),
                   pl.BlockSpec((rows, LANES), lambda h: (h, 0))],
        out_shape=[jax.ShapeDtypeStruct((n_heads, 2 * tb, tb), F32),
                   jax.ShapeDtypeStruct((n_heads * rows, page), F32),
                   jax.ShapeDtypeStruct((n_heads * rows, LANES), F32)],
        compiler_params=_params(1),
        name="bias_tiles",
    )(rel_bias)


def _lambda(lq1, lk1, lq2, lk2, lambda_init):
    s1 = jnp.sum(lq1 * lk1, axis=-1, keepdims=True)
    s2 = jnp.sum(lq2 * lk2, axis=-1, keepdims=True)
    return jnp.exp(s1) - jnp.exp(s2) + lambda_init


def _qk(a, b):
    return lax.dot_general(a, b, (((1,), (1,)), ((), ())), preferred_element_type=F32)


def _all_sublanes(x, op):
    for dd in (1, 2, 4):
        x = op(x, pltpu.roll(x, dd, 0))
    return x


def _prompt_attn_kernel(q2_ref, k_ref, vt_ref, gb_ref, bias_ref, lq1_ref, lk1_ref, lq2_ref, lk2_ref,
                        sg_ref, o_ref, s0_ref, s1_ref, m0_ref, m1_ref, l0_ref, l1_ref, a0_ref, a1_ref,
                        pl0_ref, pl1_ref, pa0_ref, pa1_ref, *, tb, lambda_init):
    e, t_len = vt_ref.shape
    nq = t_len // tb
    s_refs = (s0_ref, s1_ref)
    m_refs = (m0_ref, m1_ref)
    l_refs = (l0_ref, l1_ref)
    acc_refs = (a0_ref, a1_ref)
    pl_refs = (pl0_ref, pl1_ref)
    pa_refs = (pa0_ref, pa1_ref)

    def block(i):
        return pl.ds(i * tb if isinstance(i, int) else pl.multiple_of(i * tb, tb), tb)

    def init():
        for mm in range(2):
            m_refs[mm][...] = jnp.full(m_refs[mm].shape, NEG, F32)
            l_refs[mm][...] = jnp.zeros(l_refs[mm].shape, F32)
            acc_refs[mm][...] = jnp.zeros(acc_refs[mm].shape, F32)

    def scores(mm, qi, kj):
        return _qk(k_ref[block(kj), :], q2_ref[mm, block(qi), :])

    def consume(mm, kj, bias):
        s = s_refs[mm][...]
        if bias is not None:
            s = s + bias
        s3 = s.reshape(tb // SUBLANES, SUBLANES, tb)
        m_prev = m_refs[mm][...]
        m_new = jnp.maximum(m_prev, _all_sublanes(jnp.max(s3, axis=0), jnp.maximum))
        alpha = jnp.exp2(m_prev - m_new)
        p3 = jnp.exp2(s3 - m_new[None])
        l_refs[mm][...] = alpha * l_refs[mm][...] + _all_sublanes(jnp.sum(p3, axis=0), jnp.add)
        pv = jnp.dot(vt_ref[:, block(kj)], p3.reshape(tb, tb).astype(BF16),
                     preferred_element_type=F32)
        acc = acc_refs[mm][...].reshape(e // SUBLANES, SUBLANES, tb) * alpha[None]
        acc_refs[mm][...] = acc.reshape(e, tb) + pv
        m_refs[mm][...] = m_new

    def step(qi, kj, bias, next_q, next_k):
        s1_ref[...] = scores(1, qi, kj)
        consume(0, kj, bias)
        s0_ref[...] = scores(0, next_q, next_k)
        consume(1, kj, bias)

    lam = _lambda(lq1_ref[...], lk1_ref[...], lq2_ref[...], lk2_ref[...], lambda_init)

    def park():
        for mm in range(2):
            pa_refs[mm][...] = acc_refs[mm][...]
            pl_refs[mm][...] = l_refs[mm][...]

    def finalize_parked(qi):
        def normalised(mm):
            acc = pa_refs[mm][...].reshape(e // SUBLANES, SUBLANES, tb)
            return (acc / pl_refs[mm][...][None]).reshape(e, tb)

        o = (normalised(0) - lam * normalised(1)).T
        y = _rms(o, sg_ref[...]) * (1.0 - lambda_init)
        o_ref[block(qi), :] = (gb_ref[block(qi), :] * y).astype(o_ref.dtype)

    bias_prev = lambda: bias_ref[0, :tb, :]
    bias_diag = lambda: bias_ref[0, tb:, :]

    init()
    s0_ref[...] = scores(0, 0, 0)
    step(0, 0, bias_diag(), 1 if nq > 1 else 0, 0)
    park()

    def tile(qi, carry):
        init()
        n_far = qi - 1
        one = n_far & 1
        two = n_far & 2

        @pl.when(one != 0)
        def _():
            step(qi, 0, None, qi, 1)

        @pl.when(two != 0)
        def _():
            step(qi, one, None, qi, one + 1)
            step(qi, one + 1, None, qi, one + 2)

        def far_quad(jj, c):
            j = one + two + 4 * jj
            for u in range(4):
                step(qi, j + u, None, qi, j + u + 1)
            return c

        lax.fori_loop(0, lax.shift_right_logical(n_far, 2), far_quad, 0)
        finalize_parked(qi - 1)
        step(qi, qi - 1, bias_prev(), qi, qi)
        step(qi, qi, bias_diag(), jnp.minimum(qi + 1, nq - 1), 0)
        park()
        return carry

    lax.fori_loop(1, nq, tile, 0)
    finalize_parked(nq - 1)


def _prompt_attn(q2, kb, vt, gb, bias_p, lams, subln_g, *, n_seq, n_heads, lambda_init):
    _, n, d = q2.shape
    e = d // n_heads
    t_len = n // n_seq
    tb = ATTN_BLOCK
    hd = lams[0].shape[-1]
    small = pl.BlockSpec((1, hd), lambda b, h: (0, 0))
    rows = pl.BlockSpec((t_len, e), lambda b, h: (b, h))
    stat = pltpu.VMEM((SUBLANES, tb), F32)
    return pl.pallas_call(
        functools.partial(_prompt_attn_kernel, tb=tb, lambda_init=lambda_init),
        grid=(n_seq, n_heads),
        in_specs=[pl.BlockSpec((2, t_len, e), lambda b, h: (0, b, h)),
                  rows,
                  pl.BlockSpec((e, t_len), lambda b, h: (h, b)),
                  rows,
                  pl.BlockSpec((1, 2 * tb, tb), lambda b, h: (h, 0, 0)),
                  small, small, small, small,
                  pl.BlockSpec((1, e), lambda b, h: (0, 0))],
        out_specs=rows,
        out_shape=jax.ShapeDtypeStruct((n, d), BF16),
        scratch_shapes=[pltpu.VMEM((tb, tb), F32), pltpu.VMEM((tb, tb), F32),
                        stat, stat, stat, stat,
                        pltpu.VMEM((e, tb), F32), pltpu.VMEM((e, tb), F32),
                        stat, stat,
                        pltpu.VMEM((e, tb), F32), pltpu.VMEM((e, tb), F32)],
        compiler_params=_params(2),
        name="prompt_attn",
    )(q2, kb, vt, gb, bias_p, *[x.reshape(1, hd) for x in lams], subln_g.reshape(1, e))


def _as_column(stat):
    r = stat.shape[1]
    row = lax.broadcasted_iota(jnp.int32, (r, r), 0)
    col = lax.broadcasted_iota(jnp.int32, (r, r), 1)
    full = jnp.concatenate([stat] * (r // SUBLANES), axis=0)
    return jnp.sum(jnp.where(row == col, full, 0.0), axis=1, keepdims=True)


def _sample_attn_kernel(pt_ref, q2_ref, *refs, n_heads, dec_seq, pps, lambda_init):
    kp = refs[:pps]
    vp = refs[pps:2 * pps]
    (kn_ref, vn_ref, gb_ref, blp_ref, bn_ref, lq1_ref, lk1_ref, lq2_ref, lk2_ref, sg_ref,
     o_ref, q_ref, qt_ref, blpt_ref, m_ref, l_ref, acc_ref) = refs[2 * pps:]
    del pt_ref
    c = pl.program_id(1)
    nc = pl.num_programs(1)
    e = kp[0].shape[1]
    d = n_heads * e
    page = kp[0].shape[0] // n_heads
    rows = 2 * dec_seq

    @pl.when(c == 0)
    def _():
        lane_head = lax.broadcasted_iota(jnp.int32, (dec_seq, d), 1) // e
        q_ref[...] = jnp.concatenate(
            [q2_ref[mm][:, h * e:(h + 1) * e] for h in range(n_heads) for mm in range(2)], axis=0)
        qbd = jnp.concatenate(
            [jnp.where(lane_head == h, q2_ref[mm], 0.0) for h in range(n_heads) for mm in range(2)],
            axis=0)
        qt_ref[...] = qbd.T
        blpt_ref[...] = blp_ref[...].T
        m_ref[...] = jnp.full(m_ref.shape, NEG, F32)
        l_ref[...] = jnp.zeros(l_ref.shape, F32)
        acc_ref[...] = jnp.zeros(acc_ref.shape, F32)

    def head_rows(ref, h):
        return ref[pl.ds(h, page, stride=n_heads), :]

    is_last = (c == nc - 1).astype(F32)
    parts = []
    for j in range(pps):
        keys = jnp.concatenate([head_rows(kp[j], h) for h in range(n_heads)], axis=1)
        parts.append(jnp.dot(keys, qt_ref[...], preferred_element_type=F32))
    parts[-1] = parts[-1] + blpt_ref[...] * is_last
    s3 = jnp.concatenate(parts, axis=0).reshape(pps * page // SUBLANES, SUBLANES, -1)
    m_prev = m_ref[...]
    m_new = jnp.maximum(m_prev, _all_sublanes(jnp.max(s3, axis=0), jnp.maximum))
    alpha = jnp.exp2(m_prev - m_new)
    p3 = jnp.exp2(s3 - m_new[None])
    l_ref[...] = alpha * l_ref[...] + _all_sublanes(jnp.sum(p3, axis=0), jnp.add)
    m_ref[...] = m_new
    p = p3.reshape(pps * page, -1).T
    out = []
    for h in range(n_heads):
        ph = p[h * rows:(h + 1) * rows]
        pv = jnp.dot(ph[:, :page], head_rows(vp[0], h), preferred_element_type=F32)
        for j in range(1, pps):
            pv = pv + jnp.dot(ph[:, j * page:(j + 1) * page], head_rows(vp[j], h),
                              preferred_element_type=F32)
        out.append(pv)
    acc_ref[...] = _as_column(alpha) * acc_ref[...] + jnp.concatenate(out, axis=0)

    @pl.when(c == nc - 1)
    def _():
        s = _qk(q_ref[...], kn_ref[...]) + bn_ref[...]
        m_old = _as_column(m_ref[...])
        m_fin = jnp.maximum(m_old, jnp.max(s, axis=-1, keepdims=True))
        scale = jnp.exp2(m_old - m_fin)
        pn = jnp.exp2(s - m_fin)
        l_fin = scale * _as_column(l_ref[...]) + jnp.sum(pn, axis=-1, keepdims=True)
        acc = scale * acc_ref[...] + jnp.dot(pn, vn_ref[...], preferred_element_type=F32)
        lam = _lambda(lq1_ref[...], lk1_ref[...], lq2_ref[...], lk2_ref[...], lambda_init)
        o_all = acc / l_fin
        outs = []
        for h in range(n_heads):
            r0 = h * rows
            o = o_all[r0:r0 + dec_seq] - lam * o_all[r0 + dec_seq:r0 + rows]
            outs.append(_rms(o, sg_ref[...]) * (1.0 - lambda_init))
        o_ref[...] = gb_ref[...] * jnp.concatenate(outs, axis=1)


def _sample_attn(page_table, q2f, cache_k, cache_v, kn_rows, vn_rows, gb, bias_lp, bias_n, lams,
                 subln_g, *, n_seq, n_heads, lambda_init):
    _, n, d = q2f.shape
    dec_seq = n // n_seq
    e = d // n_heads
    n_pool, page = cache_k.shape[:2]
    n_pages = page_table.shape[1]
    pps = PAGES_PER_STEP
    nc = n_pages // pps
    hd = lams[0].shape[-1]
    rows = n_heads * 2 * dec_seq
    width = page * n_heads

    def page_spec(j):
        return pl.BlockSpec((None, width, e), lambda b, c, pt: (pt[b, c * pps + j], 0, 0))

    const2 = lambda b, c, pt: (0, 0)
    small = pl.BlockSpec((1, hd), const2)
    new_rows = pl.BlockSpec((LANES, e), lambda b, c, pt: (b, 0))
    seq = pl.BlockSpec((dec_seq, d), lambda b, c, pt: (b, 0))
    grid_spec = pltpu.PrefetchScalarGridSpec(
        num_scalar_prefetch=1,
        grid=(n_seq, nc),
        in_specs=[pl.BlockSpec((2, dec_seq, d), lambda b, c, pt: (0, b, 0))]
                 + [page_spec(j) for j in range(pps)] + [page_spec(j) for j in range(pps)]
                 + [new_rows, new_rows, seq,
                    pl.BlockSpec((rows, page), const2), pl.BlockSpec((rows, LANES), const2),
                    small, small, small, small, pl.BlockSpec((1, e), const2)],
        out_specs=seq,
        scratch_shapes=[pltpu.VMEM((rows, e), F32), pltpu.VMEM((d, rows), F32),
                        pltpu.VMEM((page, rows), F32), pltpu.VMEM((SUBLANES, rows), F32),
                        pltpu.VMEM((SUBLANES, rows), F32), pltpu.VMEM((rows, e), F32)],
    )
    ck = cache_k.reshape(n_pool, width, e)
    cv = cache_v.reshape(n_pool, width, e)
    return pl.pallas_call(
        functools.partial(_sample_attn_kernel, n_heads=n_heads, dec_seq=dec_seq, pps=pps,
                          lambda_init=lambda_init),
        grid_spec=grid_spec,
        out_shape=jax.ShapeDtypeStruct((n, d), F32),
        compiler_params=_params(2),
        name="sample_attn",
    )(page_table, q2f, *([ck] * pps), *([cv] * pps), kn_rows, vn_rows, gb, bias_lp, bias_n,
      *[x.reshape(1, hd) for x in lams], subln_g.reshape(1, e))


def _new_token_rows(x2d, n_seq, n_heads):
    n, d = x2d.shape
    e = d // n_heads
    per_seq = (n // n_seq) * n_heads
    x = x2d.reshape(n_seq, per_seq, e)
    x = jnp.pad(x, ((0, 0), (0, LANES - per_seq), (0, 0)))
    return x.reshape(n_seq * LANES, e)


def _gelu_tanh(x):
    return 0.5 * x * (1.0 + jnp.tanh(math.sqrt(2.0 / math.pi) * (x + 0.044715 * (x * x * x))))


def _ffn_kernel(x_ref, za_ref, zb_ref, spad_ref, wout_ref, nffn_ref, wup_ref,
                cw_ref, cb_ref, wdown_ref, nfin_ref, y_ref, tail_ref, carry_ref, act_ref,
                *, group_mode, final_norm):
    ti = pl.program_id(1)
    tm, d = x_ref.shape
    g = tm // SUBLANES
    d_ff = wdown_ref.shape[0]

    merged = (za_ref[...].astype(F32) + zb_ref[...].astype(F32)).astype(BF16)
    x2 = x_ref[...] + jnp.dot(merged, wout_ref[...], preferred_element_type=F32)
    h2 = _rms(x2, nffn_ref[...]).astype(BF16)

    if not group_mode:
        @pl.when(ti == 0)
        def _():
            carry_ref[...] = spad_ref[0]

    def up_conv(cols):
        up3 = jnp.dot(h2, wup_ref[:, cols], preferred_element_type=F32).reshape(g, SUBLANES, -1)
        if group_mode:
            prev3 = spad_ref[:, :, cols]
            tail_ref[:, :, cols] = up3
        else:
            first = carry_ref[:, cols][None]
            prev3 = jnp.concatenate([first, up3[:g - 1]], axis=0) if g > 1 else first
            carry_ref[:, cols] = up3[g - 1]
            tail_ref[0, :, cols] = up3[g - 1]
        return _causal_conv(up3, prev3, cw_ref[:, cols], cb_ref[:, cols]).reshape(tm, -1)

    for c0 in range(0, d_ff, FFN_CHUNK):
        gate = up_conv(slice(c0, c0 + FFN_CHUNK))
        val = up_conv(slice(d_ff + c0, d_ff + c0 + FFN_CHUNK))
        act_ref[:, c0:c0 + FFN_CHUNK] = (_gelu_tanh(gate) * val).astype(BF16)
    x3 = x2 + jnp.dot(act_ref[...], wdown_ref[...], preferred_element_type=F32)
    y_ref[...] = _rms(x3, nfin_ref[...]) if final_norm else x3


def _ffn(x2d, za, zb, spad, wout_bf, nffn, wup_bf, cw, cb, wdown_bf, nfin,
         *, n_seq, group_mode, final_norm):
    n, d = x2d.shape
    f2 = wup_bf.shape[1]
    if group_mode:
        tm, grid = n, (1, 1)
        row = pl.BlockSpec((n, d), lambda b, t: (0, 0))
        seq = pl.BlockSpec((n_seq, SUBLANES, f2), lambda b, t: (0, 0, 0))
    else:
        t_len = n // n_seq
        tm = min(FFN_ROW_TILE, t_len)
        nt = t_len // tm
        grid = (n_seq, nt)
        row = pl.BlockSpec((tm, d), lambda b, t: (b * nt + t, 0))
        seq = pl.BlockSpec((1, SUBLANES, f2), lambda b, t: (b, 0, 0))
    return pl.pallas_call(
        functools.partial(_ffn_kernel, group_mode=group_mode, final_norm=final_norm),
        grid=grid,
        in_specs=[row, row, row, seq, _resident(wout_bf.shape), _resident((1, d)),
                  _resident(wup_bf.shape), _resident(cw.shape), _resident((1, f2)),
                  _resident(wdown_bf.shape), _resident((1, d))],
        out_specs=[row, seq],
        out_shape=[jax.ShapeDtypeStruct((n, d), F32), jax.ShapeDtypeStruct((n_seq, SUBLANES, f2), F32)],
        scratch_shapes=[pltpu.VMEM((SUBLANES, f2), F32), pltpu.VMEM((tm, f2 // 2), BF16)],
        compiler_params=_params(2),
        name="merge_ffn",
    )(x2d, za, zb, spad, wout_bf, nffn.reshape(1, d), wup_bf, cw, cb.reshape(1, f2),
      wdown_bf, nfin.reshape(1, d))


def _pad_state(buf):
    b, w1, c = buf.shape
    return jnp.concatenate([jnp.zeros((b, SUBLANES - w1, c), buf.dtype), buf], axis=1)


def kernel(x_prompt, x_sample, cache_k, cache_v, page_table, state_rglru_h, state_rglru_conv, state_ffn_conv, w_in, conv_w, conv_b, rg_w_a, rg_b_a, rg_w_x, rg_b_x, rg_a_param, lambda_q1, lambda_k1, lambda_q2, lambda_k2, subln_g, rel_bias, w_out, norm_attn, norm_ffn, w_up, ffn_conv_w, ffn_conv_b, w_down, norm_final):
    bsz, t_len, d = x_prompt.shape
    dbsz, dec_seq, _ = x_sample.shape
    depth = w_in.shape[0]
    n_heads = cache_k.shape[3]
    kv_dim = cache_k.shape[4]
    head_dim = kv_dim // 2
    page = cache_k.shape[2]
    past = page_table.shape[1] * page
    cw_w = conv_w.shape[1]
    fw_w = ffn_conv_w.shape[1]
    assert w_in.shape[2] == 6 * d and n_heads * kv_dim == d and rg_a_param.shape[1] == d
    assert dec_seq == SUBLANES and t_len % ATTN_BLOCK == 0 and page_table.shape[1] % PAGES_PER_STEP == 0
    assert ATTN_BLOCK >= MAX_DISTANCE and page >= MAX_DISTANCE and dec_seq * n_heads <= LANES
    assert kv_dim == LANES and n_heads == SUBLANES

    bias_p, bias_lp, bias_n = _bias_tiles(rel_bias, n_heads, ATTN_BLOCK, page, dec_seq)

    xp = x_prompt.reshape(bsz * t_len, d)
    xs = x_sample.reshape(dbsz * dec_seq, d)
    outs = [[] for _ in range(10)]
    for l in range(depth):
        lambda_init = 0.8 - 0.6 * math.exp(-0.3 * l)
        last = l == depth - 1
        w_in_bf = w_in[l].astype(BF16)
        wa_bf = rg_w_a[l].astype(BF16)
        wx_bf = rg_w_x[l].astype(BF16)
        wout_bf = w_out[l].astype(BF16)
        wup_bf = w_up[l].astype(BF16)
        wdown_bf = w_down[l].astype(BF16)
        lams = (lambda_q1[l], lambda_k1[l], lambda_q2[l], lambda_k2[l])

        def branch_a(xr, ga, spad, h0, n_seq, group_mode, pos_base):
            h0b = jnp.broadcast_to(h0[:, None, :], (n_seq, SUBLANES, d))
            return _rglru(xr, ga, spad, h0b, conv_w[l], conv_b[l], wa_bf, rg_b_a[l], wx_bf, rg_b_x[l],
                          rg_a_param[l], n_seq=n_seq, group_mode=group_mode, pos_base=pos_base)

        def channel_mix(x2d, za, zb, spad, n_seq, group_mode):
            return _ffn(x2d, za, zb, spad, wout_bf, norm_ffn[l], wup_bf, ffn_conv_w[l],
                        ffn_conv_b[l], wdown_bf, norm_final, n_seq=n_seq, group_mode=group_mode,
                        final_norm=last)

        xr, q2, k, kb, v, vt, ga, gb = _inproj(xp, norm_attn[l], w_in_bf, head_dim)
        za, hp = branch_a(xr, ga, jnp.zeros((bsz, SUBLANES, d), F32), jnp.zeros((bsz, d), F32),
                          bsz, False, 0)
        zb = _prompt_attn(q2, kb, vt, gb, bias_p, lams, subln_g[l], n_seq=bsz, n_heads=n_heads,
                          lambda_init=lambda_init)
        xp, fp = channel_mix(xp, za, zb, jnp.zeros((bsz, SUBLANES, ffn_conv_w.shape[2]), F32),
                             bsz, False)
        outs[0].append(k.reshape(bsz, t_len, n_heads, kv_dim))
        outs[1].append(v.reshape(bsz, t_len, n_heads, kv_dim))
        outs[4].append(hp[:, SUBLANES - 1])
        outs[6].append(xr.reshape(bsz, t_len, d)[:, t_len - (cw_w - 1):])
        outs[8].append(fp[:, SUBLANES - (fw_w - 1):])

        xr, q2, k, _, v, _, ga, gb = _inproj(xs, norm_attn[l], w_in_bf, head_dim)
        za, hs = branch_a(xr, ga, _pad_state(state_rglru_conv[l]), state_rglru_h[l], dbsz, True, past)
        zb = _sample_attn(page_table, q2.astype(F32), cache_k[l], cache_v[l],
                          _new_token_rows(k, dbsz, n_heads), _new_token_rows(v, dbsz, n_heads),
                          gb, bias_lp, bias_n, lams, subln_g[l], n_seq=dbsz, n_heads=n_heads,
                          lambda_init=lambda_init)
        xs, fs = channel_mix(xs, za, zb, _pad_state(state_ffn_conv[l]), dbsz, True)
        outs[2].append(k.reshape(dbsz, dec_seq, n_heads, kv_dim))
        outs[3].append(v.reshape(dbsz, dec_seq, n_heads, kv_dim))
        outs[5].append(hs[:, SUBLANES - 1])
        outs[7].append(xr.reshape(dbsz, dec_seq, d)[:, dec_seq - (cw_w - 1):])
        outs[9].append(fs[:, SUBLANES - (fw_w - 1):])

    return (xp.reshape(bsz, t_len, d), xs.reshape(dbsz, dec_seq, d),
            jnp.stack(outs[0]), jnp.stack(outs[1]), jnp.stack(outs[2]), jnp.stack(outs[3]),
            jnp.stack(outs[4]), jnp.stack(outs[5]), jnp.stack(outs[6]), jnp.stack(outs[7]),
            jnp.stack(outs[8]), jnp.stack(outs[9]))
```

```python
import functools
import math

import jax
import jax.numpy as jnp
from jax import lax
from jax.experimental import pallas as pl
from jax.experimental.pallas import tpu as pltpu

F32 = jnp.float32
BF16 = jnp.bfloat16

EPS = 1e-6
RG_C = 8.0
N_BUCKETS = 32
MAX_DISTANCE = 128
NEG = -1e30
LOG2E = math.log2(math.e)

SUBLANES = 8
LANES = 128
ROW_TILE = 256
FFN_ROW_TILE = 512
ATTN_BLOCK = 512
FFN_CHUNK = 512
PAGES_PER_STEP = 16
VMEM_LIMIT = 52 * 1024 * 1024


def _params(n_axes, vmem=VMEM_LIMIT):
    return pltpu.CompilerParams(
        dimension_semantics=("arbitrary",) * n_axes, vmem_limit_bytes=vmem)


def _resident(shape):
    nd = len(shape)
    return pl.BlockSpec(shape, lambda *_: (0,) * nd, pipeline_mode=pl.Buffered(1))


def _sigmoid(x):
    return 0.5 * (1.0 + jnp.tanh(0.5 * x))


def _rms(x, g):
    return x * lax.rsqrt(jnp.mean(x * x, axis=-1, keepdims=True) + EPS) * g


def _inproj_kernel(x_ref, g_ref, w_ref, spad_ref, h0_ref, cw_ref, cb_ref, wa_ref, ba_ref, wx_ref,
                   bx_ref, ap_ref, q2_ref, k_ref, kb_ref, v_ref, vt_ref, gb_ref, za_ref, hlast_ref,
                   xtail_ref, carry_x, carry_h, a_scr, b_scr,
                   *, head_dim, group_mode, pos_base, n_blocks):
    ti = pl.program_id(1)
    tm, d = x_ref.shape
    h = _rms(x_ref[...], g_ref[...]).astype(BF16)

    def proj(j):
        return jnp.dot(h, w_ref[:, j * d:(j + 1) * d], preferred_element_type=F32)

    if not group_mode:
        @pl.when(ti == 0)
        def _():
            carry_x[...] = spad_ref[0]
            carry_h[...] = h0_ref[0]

    def emit_q():
        q = proj(1) * (head_dim ** -0.5 * LOG2E)
        lane = lax.broadcasted_iota(jnp.int32, q.shape, 1)
        first = (lane % (2 * head_dim)) < head_dim
        q2_ref[0] = jnp.where(first, q, 0.0).astype(BF16)
        q2_ref[1] = jnp.where(first, 0.0, q).astype(BF16)

    def emit_k():
        k = proj(2)
        k_ref[...] = k
        kb_ref[...] = k.astype(BF16)

    def emit_v():
        v = proj(3)
        v_ref[...] = v
        vt_ref[...] = v.T.astype(BF16)

    def emit_gb():
        gb_ref[...] = _sigmoid(proj(5))

    emitters = (emit_q, emit_k, emit_v, emit_gb)
    pc = d // len(emitters)
    bpp = n_blocks // len(emitters)
    a_parts, b_parts, ga_parts = [], [], []
    for p, emit in enumerate(emitters):
        cols = slice(p * pc, (p + 1) * pc)
        xr = jnp.dot(h, w_ref[:, cols], preferred_element_type=F32)
        a3, b3 = _rglru_group_scan(xr, ti, cols, range(p * bpp, (p + 1) * bpp), spad_ref, cw_ref, cb_ref,
                                   wa_ref, ba_ref, wx_ref, bx_ref, ap_ref, xtail_ref, carry_x,
                                   group_mode=group_mode, pos_base=pos_base)
        a_parts.append(a3)
        b_parts.append(b3)
        ga_parts.append(_sigmoid(jnp.dot(h, w_ref[:, 4 * d + p * pc:4 * d + (p + 1) * pc],
                                         preferred_element_type=F32)))
        emit()

    h3 = _rglru_carry(jnp.concatenate(a_parts, axis=2), jnp.concatenate(b_parts, axis=2), h0_ref,
                      hlast_ref, carry_h, a_scr, b_scr, group_mode=group_mode)
    za_ref[...] = (jnp.concatenate(ga_parts, axis=1) * h3.reshape(tm, d)).astype(za_ref.dtype)


def _inproj(x2d, g, w_bf, spad, h0b, cw, cb, wa_bf, ba, wx_bf, bx, ap,
            *, head_dim, n_seq, group_mode, pos_base):
    n, d = x2d.shape
    n_blocks = wa_bf.shape[0]
    if group_mode:
        tm, nt, grid = n, 1, (1, 1)
        seq = pl.BlockSpec((n_seq, SUBLANES, d), lambda b, t: (0, 0, 0))
    else:
        t_len = n // n_seq
        tm = min(ROW_TILE, t_len)
        nt = t_len // tm
        grid = (n_seq, nt)
        seq = pl.BlockSpec((1, SUBLANES, d), lambda b, t: (b, 0, 0))
    blk = pl.BlockSpec((tm, d), lambda b, t: (b * nt + t, 0))
    g8 = tm // SUBLANES
    vec = lambda a: a.reshape(1, d)
    f32 = jax.ShapeDtypeStruct((n, d), F32)
    b16 = jax.ShapeDtypeStruct((n, d), BF16)
    state = jax.ShapeDtypeStruct((n_seq, SUBLANES, d), F32)
    return pl.pallas_call(
        functools.partial(_inproj_kernel, head_dim=head_dim, group_mode=group_mode, pos_base=pos_base,
                          n_blocks=n_blocks),
        grid=grid,
        in_specs=[blk, _resident((1, d)), _resident(w_bf.shape), seq, seq, _resident(cw.shape),
                  _resident((1, d)), _resident(wa_bf.shape), _resident((1, d)), _resident(wx_bf.shape),
                  _resident((1, d)), _resident((1, d))],
        out_specs=[pl.BlockSpec((2, tm, d), lambda b, t: (0, b * nt + t, 0)), blk, blk, blk,
                   pl.BlockSpec((d, tm), lambda b, t: (0, b * nt + t)), blk, blk, seq, seq],
        out_shape=[jax.ShapeDtypeStruct((2, n, d), BF16), f32, b16, f32,
                   jax.ShapeDtypeStruct((d, n), BF16), f32, b16, state, state],
        scratch_shapes=[pltpu.VMEM((SUBLANES, d), F32), pltpu.VMEM((SUBLANES, d), F32),
                        pltpu.VMEM((g8, SUBLANES, d), F32), pltpu.VMEM((g8, SUBLANES, d), F32)],
        compiler_params=_params(2),
        name="inproj",
    )(x2d, g.reshape(1, d), w_bf, spad, h0b, cw, vec(cb), wa_bf, vec(ba), wx_bf, vec(bx), vec(ap))


def _shifted(x3, prev3, d, t_idx):
    return jnp.where(t_idx >= d, pltpu.roll(x3, d, 1), pltpu.roll(prev3, d, 1))


def _causal_conv(x3, prev3, w, b):
    width = w.shape[0]
    t_idx = lax.broadcasted_iota(jnp.int32, x3.shape, 1)
    y = b + w[width - 1:width] * x3
    for dd in range(1, width):
        y = y + w[width - 1 - dd:width - dd] * _shifted(x3, prev3, dd, t_idx)
    return y


def _rglru_group_scan(xr, ti, cols, blocks, spad_ref, cw_ref, cb_ref, wa_ref, ba_ref, wx_ref, bx_ref,
                      ap_ref, xtail_ref, carry_x, *, group_mode, pos_base):
    tt, c = xr.shape
    g = tt // SUBLANES
    blk = c // len(blocks)
    x3 = xr.reshape(g, SUBLANES, c)

    if group_mode:
        prev3 = spad_ref[:, :, cols]
        xtail_ref[:, :, cols] = x3
    else:
        first = carry_x[:, cols][None]
        prev3 = jnp.concatenate([first, x3[:g - 1]], axis=0) if g > 1 else first

    xc3 = _causal_conv(x3, prev3, cw_ref[:, cols], cb_ref[:, cols])
    if not group_mode:
        carry_x[:, cols] = x3[g - 1]
        xtail_ref[0, :, cols] = x3[g - 1]

    xc = xc3.reshape(tt, c)
    xcb = xc.astype(BF16)

    def gate(w_ref, b_ref):
        parts = [jnp.dot(xcb[:, i * blk:(i + 1) * blk], w_ref[n], preferred_element_type=F32)
                 for i, n in enumerate(blocks)]
        return jnp.concatenate(parts, axis=1) + b_ref[:, cols]

    r = _sigmoid(gate(wa_ref, ba_ref))
    i = _sigmoid(gate(wx_ref, bx_ref))
    z = -ap_ref[:, cols]
    softplus = jnp.maximum(z, 0.0) + jnp.log(1.0 + jnp.exp(-jnp.abs(z)))
    log_a = -RG_C * r * softplus
    a = jnp.exp(log_a)
    th = jnp.tanh(log_a)
    mult = jnp.sqrt(-2.0 * th / (1.0 - th))
    ix = i * xc
    a3 = a.reshape(g, SUBLANES, c)
    b3 = (mult * ix).reshape(g, SUBLANES, c)
    t_idx = lax.broadcasted_iota(jnp.int32, a3.shape, 1)
    if pos_base == 0:
        ix3 = ix.reshape(g, SUBLANES, c)
        if group_mode:
            reset = t_idx == 0
            a3 = jnp.where(reset, 0.0, a3)
            b3 = jnp.where(reset, ix3, b3)
        else:
            reset = (lax.broadcasted_iota(jnp.int32, (SUBLANES, c), 0) + ti) == 0
            rest = lambda x3: [x3[1:]] if g > 1 else []
            a3 = jnp.concatenate([jnp.where(reset, 0.0, a3[0])[None]] + rest(a3), axis=0)
            b3 = jnp.concatenate([jnp.where(reset, ix3[0], b3[0])[None]] + rest(b3), axis=0)

    for dd in (1, 2, 4):
        valid = t_idx >= dd
        b3 = jnp.where(valid, a3 * pltpu.roll(b3, dd, 1) + b3, b3)
        a3 = jnp.where(valid, a3 * pltpu.roll(a3, dd, 1), a3)
    return a3, b3


def _rglru_carry(a3, b3, h0_ref, hlast_ref, carry_h, a_scr, b_scr, *, group_mode):
    if group_mode:
        h3 = a3 * h0_ref[...] + b3
        hlast_ref[...] = h3
        return h3
    g, _, c = a3.shape
    a_scr[...] = a3
    b_scr[...] = b3

    def body(gi, carry):
        hg = a_scr[gi] * carry + b_scr[gi]
        b_scr[gi] = hg
        return jnp.broadcast_to(hg[SUBLANES - 1:SUBLANES, :], (SUBLANES, c))

    last = lax.fori_loop(0, g, body, carry_h[...])
    carry_h[...] = last
    hlast_ref[0] = last
    return b_scr[...]


def _bias_of(dist, valid, rb_ref, h):
    n = jnp.maximum(dist, 0)
    max_exact = N_BUCKETS // 2
    nf = jnp.maximum(n, 1).astype(F32)
    large = max_exact + (jnp.log(nf / max_exact) / math.log(MAX_DISTANCE / max_exact)
                         * (N_BUCKETS - max_exact)).astype(jnp.int32)
    large = jnp.minimum(large, N_BUCKETS - 1)
    bucket = jnp.where(n < max_exact, n, large)
    base = rb_ref[N_BUCKETS - 1, h]
    out = jnp.zeros(dist.shape, F32)
    for b in range(N_BUCKETS - 1):
        out = jnp.where(bucket == b, (rb_ref[b, h] - base) * LOG2E, out)
    return jnp.where(valid, out, NEG)


def _bias_kernel(rb_ref, bp_ref, blp_ref, bn_ref, *, tb, n_heads, dec_seq):
    h = pl.program_id(0)
    j = lax.broadcasted_iota(jnp.int32, (LANES, LANES), 0)
    i = lax.broadcasted_iota(jnp.int32, (LANES, LANES), 1)
    for jb in range(2 * tb // LANES):
        for ib in range(tb // LANES):
            base = (ib - jb) * LANES + tb
            rows, cols = slice(jb * LANES, (jb + 1) * LANES), slice(ib * LANES, (ib + 1) * LANES)
            if base - (LANES - 1) >= MAX_DISTANCE:
                bp_ref[0, rows, cols] = jnp.zeros((LANES, LANES), F32)
            elif base + (LANES - 1) < 0:
                bp_ref[0, rows, cols] = jnp.full((LANES, LANES), NEG, F32)
            else:
                dist = base + i - j
                bp_ref[0, rows, cols] = _bias_of(dist, dist >= 0, rb_ref, h)
    rows, page = blp_ref.shape
    t = lax.broadcasted_iota(jnp.int32, (rows, page), 0) % dec_seq
    key = lax.broadcasted_iota(jnp.int32, (rows, page), 1)
    dist = page + t - key
    blp_ref[...] = _bias_of(dist, dist >= 0, rb_ref, h)
    t = lax.broadcasted_iota(jnp.int32, (rows, LANES), 0) % dec_seq
    lane = lax.broadcasted_iota(jnp.int32, (rows, LANES), 1)
    dist = t - lane // n_heads
    bn_ref[...] = _bias_of(dist, jnp.logical_and((lane % n_heads) == h, dist >= 0), rb_ref, h)


def _bias_tiles(rel_bias, n_heads, tb, page, dec_seq):
    rows = 2 * dec_seq
    return pl.pallas_call(
        functools.partial(_bias_kernel, tb=tb, n_heads=n_heads, dec_seq=dec_seq),
        grid=(n_heads,),
        in_specs=[pl.BlockSpec(memory_space=pltpu.SMEM)],
        out_specs=[pl.BlockSpec((1, 2 * tb, tb), lambda h: (h, 0, 0)),
                   pl.BlockSpec((rows, page), lambda h: (h, 0)),
                   pl.BlockSpec((rows, LANES), lambda h: (h, 0))],
        out_shape=[jax.ShapeDtypeStruct((n_heads, 2 * tb, tb), F32),
                   jax.ShapeDtypeStruct((n_heads * rows, page), F32),
                   jax.ShapeDtypeStruct((n_heads * rows, LANES), F32)],
        compiler_params=_params(1),
        name="bias_tiles",
    )(rel_bias)


def _lambda(lq1, lk1, lq2, lk2, lambda_init):
    s1 = jnp.sum(lq1 * lk1, axis=-1, keepdims=True)
    s2 = jnp.sum(lq2 * lk2, axis=-1, keepdims=True)
    return jnp.exp(s1) - jnp.exp(s2) + lambda_init


def _qk(a, b):
    return lax.dot_general(a, b, (((1,), (1,)), ((), ())), preferred_element_type=F32)


def _all_sublanes(x, op):
    for dd in (1, 2, 4):
        x = op(x, pltpu.roll(x, dd, 0))
    return x


def _prompt_attn_kernel(q2_ref, k_ref, vt_ref, gb_ref, bias_ref, lq1_ref, lk1_ref, lq2_ref, lk2_ref,
                        sg_ref, o_ref, s0_ref, s1_ref, m0_ref, m1_ref, l0_ref, l1_ref, a0_ref, a1_ref,
                        pl0_ref, pl1_ref, pa0_ref, pa1_ref, *, tb, lambda_init):
    e, t_len = vt_ref.shape
    nq = t_len // tb
    s_refs = (s0_ref, s1_ref)
    m_refs = (m0_ref, m1_ref)
    l_refs = (l0_ref, l1_ref)
    acc_refs = (a0_ref, a1_ref)
    pl_refs = (pl0_ref, pl1_ref)
    pa_refs = (pa0_ref, pa1_ref)

    def block(i):
        return pl.ds(i * tb if isinstance(i, int) else pl.multiple_of(i * tb, tb), tb)

    def init():
        for mm in range(2):
            m_refs[mm][...] = jnp.full(m_refs[mm].shape, NEG, F32)
            l_refs[mm][...] = jnp.zeros(l_refs[mm].shape, F32)
            acc_refs[mm][...] = jnp.zeros(acc_refs[mm].shape, F32)

    def scores(mm, qi, kj):
        return _qk(k_ref[block(kj), :], q2_ref[mm, block(qi), :])

    def consume(mm, kj, bias):
        s = s_refs[mm][...]
        if bias is not None:
            s = s + bias
        s3 = s.reshape(tb // SUBLANES, SUBLANES, tb)
        m_prev = m_refs[mm][...]
        m_new = jnp.maximum(m_prev, _all_sublanes(jnp.max(s3, axis=0), jnp.maximum))
        alpha = jnp.exp2(m_prev - m_new)
        p3 = jnp.exp2(s3 - m_new[None])
        l_refs[mm][...] = alpha * l_refs[mm][...] + _all_sublanes(jnp.sum(p3, axis=0), jnp.add)
        pv = jnp.dot(vt_ref[:, block(kj)], p3.reshape(tb, tb).astype(BF16),
                     preferred_element_type=F32)
        acc = acc_refs[mm][...].reshape(e // SUBLANES, SUBLANES, tb) * alpha[None]
        acc_refs[mm][...] = acc.reshape(e, tb) + pv
        m_refs[mm][...] = m_new

    def step(qi, kj, bias, next_q, next_k):
        s1_ref[...] = scores(1, qi, kj)
        consume(0, kj, bias)
        s0_ref[...] = scores(0, next_q, next_k)
        consume(1, kj, bias)

    lam = _lambda(lq1_ref[...], lk1_ref[...], lq2_ref[...], lk2_ref[...], lambda_init)

    def park():
        for mm in range(2):
            pa_refs[mm][...] = acc_refs[mm][...]
            pl_refs[mm][...] = l_refs[mm][...]

    def finalize_parked(qi):
        def normalised(mm):
            acc = pa_refs[mm][...].reshape(e // SUBLANES, SUBLANES, tb)
            return (acc / pl_refs[mm][...][None]).reshape(e, tb)

        o = (normalised(0) - lam * normalised(1)).T
        y = _rms(o, sg_ref[...]) * (1.0 - lambda_init)
        o_ref[block(qi), :] = (gb_ref[block(qi), :] * y).astype(o_ref.dtype)

    bias_prev = lambda: bias_ref[0, :tb, :]
    bias_diag = lambda: bias_ref[0, tb:, :]

    init()
    s0_ref[...] = scores(0, 0, 0)
    step(0, 0, bias_diag(), 1 if nq > 1 else 0, 0)
    park()

    def tile(qi, carry):
        init()
        n_far = qi - 1
        one = n_far & 1
        two = n_far & 2

        @pl.when(one != 0)
        def _():
            step(qi, 0, None, qi, 1)

        @pl.when(two != 0)
        def _():
            step(qi, one, None, qi, one + 1)
            step(qi, one + 1, None, qi, one + 2)

        def far_quad(jj, c):
            j = one + two + 4 * jj
            for u in range(4):
                step(qi, j + u, None, qi, j + u + 1)
            return c

        lax.fori_loop(0, lax.shift_right_logical(n_far, 2), far_quad, 0)
        finalize_parked(qi - 1)
        step(qi, qi - 1, bias_prev(), qi, qi)
        step(qi, qi, bias_diag(), jnp.minimum(qi + 1, nq - 1), 0)
        park()
        return carry

    lax.fori_loop(1, nq, tile, 0)
    finalize_parked(nq - 1)


def _prompt_attn(q2, kb, vt, gb, bias_p, lams, subln_g, *, n_seq, n_heads, lambda_init):
    _, n, d = q2.shape
    e = d // n_heads
    t_len = n // n_seq
    tb = ATTN_BLOCK
    hd = lams[0].shape[-1]
    small = pl.BlockSpec((1, hd), lambda b, h: (0, 0))
    rows = pl.BlockSpec((t_len, e), lambda b, h: (b, h))
    stat = pltpu.VMEM((SUBLANES, tb), F32)
    return pl.pallas_call(
        functools.partial(_prompt_attn_kernel, tb=tb, lambda_init=lambda_init),
        grid=(n_seq, n_heads),
        in_specs=[pl.BlockSpec((2, t_len, e), lambda b, h: (0, b, h)),
                  rows,
                  pl.BlockSpec((e, t_len), lambda b, h: (h, b)),
                  rows,
                  pl.BlockSpec((1, 2 * tb, tb), lambda b, h: (h, 0, 0)),
                  small, small, small, small,
                  pl.BlockSpec((1, e), lambda b, h: (0, 0))],
        out_specs=rows,
        out_shape=jax.ShapeDtypeStruct((n, d), BF16),
        scratch_shapes=[pltpu.VMEM((tb, tb), F32), pltpu.VMEM((tb, tb), F32),
                        stat, stat, stat, stat,
                        pltpu.VMEM((e, tb), F32), pltpu.VMEM((e, tb), F32),
                        stat, stat,
                        pltpu.VMEM((e, tb), F32), pltpu.VMEM((e, tb), F32)],
        compiler_params=_params(2),
        name="prompt_attn",
    )(q2, kb, vt, gb, bias_p, *[x.reshape(1, hd) for x in lams], subln_g.reshape(1, e))


def _as_column(stat):
    r = stat.shape[1]
    row = lax.broadcasted_iota(jnp.int32, (r, r), 0)
    col = lax.broadcasted_iota(jnp.int32, (r, r), 1)
    full = jnp.concatenate([stat] * (r // SUBLANES), axis=0)
    return jnp.sum(jnp.where(row == col, full, 0.0), axis=1, keepdims=True)


def _sample_attn_kernel(pt_ref, q2_ref, *refs, n_heads, dec_seq, pps, lambda_init):
    kp = refs[:pps]
    vp = refs[pps:2 * pps]
    (kn_ref, vn_ref, gb_ref, blp_ref, bn_ref, lq1_ref, lk1_ref, lq2_ref, lk2_ref, sg_ref,
     o_ref, q_ref, qt_ref, blpt_ref, m_ref, l_ref, acc_ref) = refs[2 * pps:]
    del pt_ref
    c = pl.program_id(1)
    nc = pl.num_programs(1)
    e = kp[0].shape[1]
    d = n_heads * e
    page = kp[0].shape[0] // n_heads
    rows = 2 * dec_seq

    @pl.when(c == 0)
    def _():
        lane_head = lax.broadcasted_iota(jnp.int32, (dec_seq, d), 1) // e
        q_ref[...] = jnp.concatenate(
            [q2_ref[mm][:, h * e:(h + 1) * e] for h in range(n_heads) for mm in range(2)], axis=0)
        qbd = jnp.concatenate(
            [jnp.where(lane_head == h, q2_ref[mm], 0.0) for h in range(n_heads) for mm in range(2)],
            axis=0)
        qt_ref[...] = qbd.T
        blpt_ref[...] = blp_ref[...].T
        m_ref[...] = jnp.full(m_ref.shape, NEG, F32)
        l_ref[...] = jnp.zeros(l_ref.shape, F32)
        acc_ref[...] = jnp.zeros(acc_ref.shape, F32)

    def head_rows(ref, h):
        return ref[pl.ds(h, page, stride=n_heads), :]

    is_last = (c == nc - 1).astype(F32)
    parts = []
    for j in range(pps):
        keys = jnp.concatenate([head_rows(kp[j], h) for h in range(n_heads)], axis=1)
        parts.append(jnp.dot(keys, qt_ref[...], preferred_element_type=F32))
    parts[-1] = parts[-1] + blpt_ref[...] * is_last
    s3 = jnp.concatenate(parts, axis=0).reshape(pps * page // SUBLANES, SUBLANES, -1)
    m_prev = m_ref[...]
    m_new = jnp.maximum(m_prev, _all_sublanes(jnp.max(s3, axis=0), jnp.maximum))
    alpha = jnp.exp2(m_prev - m_new)
    p3 = jnp.exp2(s3 - m_new[None])
    l_ref[...] = alpha * l_ref[...] + _all_sublanes(jnp.sum(p3, axis=0), jnp.add)
    m_ref[...] = m_new
    p = p3.reshape(pps * page, -1).T
    out = []
    for h in range(n_heads):
        ph = p[h * rows:(h + 1) * rows]
        pv = jnp.dot(ph[:, :page], head_rows(vp[0], h), preferred_element_type=F32)
        for j in range(1, pps):
            pv = pv + jnp.dot(ph[:, j * page:(j + 1) * page], head_rows(vp[j], h),
                              preferred_element_type=F32)
        out.append(pv)
    acc_ref[...] = _as_column(alpha) * acc_ref[...] + jnp.concatenate(out, axis=0)

    @pl.when(c == nc - 1)
    def _():
        s = _qk(q_ref[...], kn_ref[...]) + bn_ref[...]
        m_old = _as_column(m_ref[...])
        m_fin = jnp.maximum(m_old, jnp.max(s, axis=-1, keepdims=True))
        scale = jnp.exp2(m_old - m_fin)
        pn = jnp.exp2(s - m_fin)
        l_fin = scale * _as_column(l_ref[...]) + jnp.sum(pn, axis=-1, keepdims=True)
        acc = scale * acc_ref[...] + jnp.dot(pn, vn_ref[...], preferred_element_type=F32)
        lam = _lambda(lq1_ref[...], lk1_ref[...], lq2_ref[...], lk2_ref[...], lambda_init)
        o_all = acc / l_fin
        outs = []
        for h in range(n_heads):
            r0 = h * rows
            o = o_all[r0:r0 + dec_seq] - lam * o_all[r0 + dec_seq:r0 + rows]
            outs.append(_rms(o, sg_ref[...]) * (1.0 - lambda_init))
        o_ref[...] = gb_ref[...] * jnp.concatenate(outs, axis=1)


def _sample_attn(page_table, q2f, cache_k, cache_v, kn_rows, vn_rows, gb, bias_lp, bias_n, lams,
                 subln_g, *, n_seq, n_heads, lambda_init):
    _, n, d = q2f.shape
    dec_seq = n // n_seq
    e = d // n_heads
    n_pool, page = cache_k.shape[:2]
    n_pages = page_table.shape[1]
    pps = PAGES_PER_STEP
    nc = n_pages // pps
    hd = lams[0].shape[-1]
    rows = n_heads * 2 * dec_seq
    width = page * n_heads

    def page_spec(j):
        return pl.BlockSpec((None, width, e), lambda b, c, pt: (pt[b, c * pps + j], 0, 0))

    const2 = lambda b, c, pt: (0, 0)
    small = pl.BlockSpec((1, hd), const2)
    new_rows = pl.BlockSpec((LANES, e), lambda b, c, pt: (b, 0))
    seq = pl.BlockSpec((dec_seq, d), lambda b, c, pt: (b, 0))
    grid_spec = pltpu.PrefetchScalarGridSpec(
        num_scalar_prefetch=1,
        grid=(n_seq, nc),
        in_specs=[pl.BlockSpec((2, dec_seq, d), lambda b, c, pt: (0, b, 0))]
                 + [page_spec(j) for j in range(pps)] + [page_spec(j) for j in range(pps)]
                 + [new_rows, new_rows, seq,
                    pl.BlockSpec((rows, page), const2), pl.BlockSpec((rows, LANES), const2),
                    small, small, small, small, pl.BlockSpec((1, e), const2)],
        out_specs=seq,
        scratch_shapes=[pltpu.VMEM((rows, e), F32), pltpu.VMEM((d, rows), F32),
                        pltpu.VMEM((page, rows), F32), pltpu.VMEM((SUBLANES, rows), F32),
                        pltpu.VMEM((SUBLANES, rows), F32), pltpu.VMEM((rows, e), F32)],
    )
    ck = cache_k.reshape(n_pool, width, e)
    cv = cache_v.reshape(n_pool, width, e)
    return pl.pallas_call(
        functools.partial(_sample_attn_kernel, n_heads=n_heads, dec_seq=dec_seq, pps=pps,
                          lambda_init=lambda_init),
        grid_spec=grid_spec,
        out_shape=jax.ShapeDtypeStruct((n, d), F32),
        compiler_params=_params(2),
        name="sample_attn",
    )(page_table, q2f, *([ck] * pps), *([cv] * pps), kn_rows, vn_rows, gb, bias_lp, bias_n,
      *[x.reshape(1, hd) for x in lams], subln_g.reshape(1, e))


def _new_token_rows(x2d, n_seq, n_heads):
    n, d = x2d.shape
    e = d // n_heads
    per_seq = (n // n_seq) * n_heads
    x = x2d.reshape(n_seq, per_seq, e)
    x = jnp.pad(x, ((0, 0), (0, LANES - per_seq), (0, 0)))
    return x.reshape(n_seq * LANES, e)


def _gelu_tanh(x):
    return 0.5 * x * (1.0 + jnp.tanh(math.sqrt(2.0 / math.pi) * (x + 0.044715 * (x * x * x))))


def _ffn_kernel(x_ref, za_ref, zb_ref, spad_ref, wout_ref, nffn_ref, wup_ref,
                cw_ref, cb_ref, wdown_ref, nfin_ref, y_ref, tail_ref, carry_ref, act_ref,
                *, group_mode, final_norm):
    ti = pl.program_id(1)
    tm, d = x_ref.shape
    g = tm // SUBLANES
    d_ff = wdown_ref.shape[0]

    merged = (za_ref[...].astype(F32) + zb_ref[...].astype(F32)).astype(BF16)
    x2 = x_ref[...] + jnp.dot(merged, wout_ref[...], preferred_element_type=F32)
    h2 = _rms(x2, nffn_ref[...]).astype(BF16)

    if not group_mode:
        @pl.when(ti == 0)
        def _():
            carry_ref[...] = spad_ref[0]

    def up_conv(cols):
        up3 = jnp.dot(h2, wup_ref[:, cols], preferred_element_type=F32).reshape(g, SUBLANES, -1)
        if group_mode:
            prev3 = spad_ref[:, :, cols]
            tail_ref[:, :, cols] = up3
        else:
            first = carry_ref[:, cols][None]
            prev3 = jnp.concatenate([first, up3[:g - 1]], axis=0) if g > 1 else first
            carry_ref[:, cols] = up3[g - 1]
            tail_ref[0, :, cols] = up3[g - 1]
        return _causal_conv(up3, prev3, cw_ref[:, cols], cb_ref[:, cols]).reshape(tm, -1)

    for c0 in range(0, d_ff, FFN_CHUNK):
        gate = up_conv(slice(c0, c0 + FFN_CHUNK))
        val = up_conv(slice(d_ff + c0, d_ff + c0 + FFN_CHUNK))
        act_ref[:, c0:c0 + FFN_CHUNK] = (_gelu_tanh(gate) * val).astype(BF16)
    x3 = x2 + jnp.dot(act_ref[...], wdown_ref[...], preferred_element_type=F32)
    y_ref[...] = _rms(x3, nfin_ref[...]) if final_norm else x3


def _ffn(x2d, za, zb, spad, wout_bf, nffn, wup_bf, cw, cb, wdown_bf, nfin,
         *, n_seq, group_mode, final_norm):
    n, d = x2d.shape
    f2 = wup_bf.shape[1]
    if group_mode:
        tm, grid = n, (1, 1)
        row = pl.BlockSpec((n, d), lambda b, t: (0, 0))
        seq = pl.BlockSpec((n_seq, SUBLANES, f2), lambda b, t: (0, 0, 0))
    else:
        t_len = n // n_seq
        tm = min(FFN_ROW_TILE, t_len)
        nt = t_len // tm
        grid = (n_seq, nt)
        row = pl.BlockSpec((tm, d), lambda b, t: (b * nt + t, 0))
        seq = pl.BlockSpec((1, SUBLANES, f2), lambda b, t: (b, 0, 0))
    return pl.pallas_call(
        functools.partial(_ffn_kernel, group_mode=group_mode, final_norm=final_norm),
        grid=grid,
        in_specs=[row, row, row, seq, _resident(wout_bf.shape), _resident((1, d)),
                  _resident(wup_bf.shape), _resident(cw.shape), _resident((1, f2)),
                  _resident(wdown_bf.shape), _resident((1, d))],
        out_specs=[row, seq],
        out_shape=[jax.ShapeDtypeStruct((n, d), F32), jax.ShapeDtypeStruct((n_seq, SUBLANES, f2), F32)],
        scratch_shapes=[pltpu.VMEM((SUBLANES, f2), F32), pltpu.VMEM((tm, f2 // 2), BF16)],
        compiler_params=_params(2),
        name="merge_ffn",
    )(x2d, za, zb, spad, wout_bf, nffn.reshape(1, d), wup_bf, cw, cb.reshape(1, f2),
      wdown_bf, nfin.reshape(1, d))


def _pad_state(buf):
    b, w1, c = buf.shape
    return jnp.concatenate([jnp.zeros((b, SUBLANES - w1, c), buf.dtype), buf], axis=1)


def kernel(x_prompt, x_sample, cache_k, cache_v, page_table, state_rglru_h, state_rglru_conv, state_ffn_conv, w_in, conv_w, conv_b, rg_w_a, rg_b_a, rg_w_x, rg_b_x, rg_a_param, lambda_q1, lambda_k1, lambda_q2, lambda_k2, subln_g, rel_bias, w_out, norm_attn, norm_ffn, w_up, ffn_conv_w, ffn_conv_b, w_down, norm_final):
    bsz, t_len, d = x_prompt.shape
    dbsz, dec_seq, _ = x_sample.shape
    depth = w_in.shape[0]
    n_heads = cache_k.shape[3]
    kv_dim = cache_k.shape[4]
    head_dim = kv_dim // 2
    page = cache_k.shape[2]
    past = page_table.shape[1] * page
    cw_w = conv_w.shape[1]
    fw_w = ffn_conv_w.shape[1]
    assert w_in.shape[2] == 6 * d and n_heads * kv_dim == d and rg_a_param.shape[1] == d
    assert dec_seq == SUBLANES and t_len % ATTN_BLOCK == 0 and page_table.shape[1] % PAGES_PER_STEP == 0
    assert ATTN_BLOCK >= MAX_DISTANCE and page >= MAX_DISTANCE and dec_seq * n_heads <= LANES
    assert kv_dim == LANES and n_heads == SUBLANES

    bias_p, bias_lp, bias_n = _bias_tiles(rel_bias, n_heads, ATTN_BLOCK, page, dec_seq)

    xp = x_prompt.reshape(bsz * t_len, d)
    xs = x_sample.reshape(dbsz * dec_seq, d)
    outs = [[] for _ in range(10)]
    for l in range(depth):
        lambda_init = 0.8 - 0.6 * math.exp(-0.3 * l)
        last = l == depth - 1
        w_in_bf = w_in[l].astype(BF16)
        wa_bf = rg_w_a[l].astype(BF16)
        wx_bf = rg_w_x[l].astype(BF16)
        wout_bf = w_out[l].astype(BF16)
        wup_bf = w_up[l].astype(BF16)
        wdown_bf = w_down[l].astype(BF16)
        lams = (lambda_q1[l], lambda_k1[l], lambda_q2[l], lambda_k2[l])

        def project(x2d, spad, h0, n_seq, group_mode, pos_base):
            h0b = jnp.broadcast_to(h0[:, None, :], (n_seq, SUBLANES, d))
            return _inproj(x2d, norm_attn[l], w_in_bf, spad, h0b, conv_w[l], conv_b[l], wa_bf,
                           rg_b_a[l], wx_bf, rg_b_x[l], rg_a_param[l], head_dim=head_dim,
                           n_seq=n_seq, group_mode=group_mode, pos_base=pos_base)

        def channel_mix(x2d, za, zb, spad, n_seq, group_mode):
            return _ffn(x2d, za, zb, spad, wout_bf, norm_ffn[l], wup_bf, ffn_conv_w[l],
                        ffn_conv_b[l], wdown_bf, norm_final, n_seq=n_seq, group_mode=group_mode,
                        final_norm=last)

        q2, k, kb, v, vt, gb, za, hp, xtail = project(
            xp, jnp.zeros((bsz, SUBLANES, d), F32), jnp.zeros((bsz, d), F32), bsz, False, 0)
        zb = _prompt_attn(q2, kb, vt, gb, bias_p, lams, subln_g[l], n_seq=bsz, n_heads=n_heads,
                          lambda_init=lambda_init)
        xp, fp = channel_mix(xp, za, zb, jnp.zeros((bsz, SUBLANES, ffn_conv_w.shape[2]), F32),
                             bsz, False)
        outs[0].append(k.reshape(bsz, t_len, n_heads, kv_dim))
        outs[1].append(v.reshape(bsz, t_len, n_heads, kv_dim))
        outs[4].append(hp[:, SUBLANES - 1])
        outs[6].append(xtail[:, SUBLANES - (cw_w - 1):])
        outs[8].append(fp[:, SUBLANES - (fw_w - 1):])

        q2, k, _, v, _, gb, za, hs, xtail = project(
            xs, _pad_state(state_rglru_conv[l]), state_rglru_h[l], dbsz, True, past)
        zb = _sample_attn(page_table, q2.astype(F32), cache_k[l], cache_v[l],
                          _new_token_rows(k, dbsz, n_heads), _new_token_rows(v, dbsz, n_heads),
                          gb, bias_lp, bias_n, lams, subln_g[l], n_seq=dbsz, n_heads=n_heads,
                          lambda_init=lambda_init)
        xs, fs = channel_mix(xs, za, zb, _pad_state(state_ffn_conv[l]), dbsz, True)
        outs[2].append(k.reshape(dbsz, dec_seq, n_heads, kv_dim))
        outs[3].append(v.reshape(dbsz, dec_seq, n_heads, kv_dim))
        outs[5].append(hs[:, SUBLANES - 1])
        outs[7].append(xtail[:, SUBLANES - (cw_w - 1):])
        outs[9].append(fs[:, SUBLANES - (fw_w - 1):])

    return (xp.reshape(bsz, t_len, d), xs.reshape(dbsz, dec_seq, d),
            jnp.stack(outs[0]), jnp.stack(outs[1]), jnp.stack(outs[2]), jnp.stack(outs[3]),
            jnp.stack(outs[4]), jnp.stack(outs[5]), jnp.stack(outs[6]), jnp.stack(outs[7]),
            jnp.stack(outs[8]), jnp.stack(outs[9]))
```

```python
import functools
import math

import jax
import jax.numpy as jnp
from jax import lax
from jax.experimental import pallas as pl
from jax.experimental.pallas import tpu as pltpu

F32 = jnp.float32
BF16 = jnp.bfloat16

EPS = 1e-6
RG_C = 8.0
N_BUCKETS = 32
MAX_DISTANCE = 128
NEG = -1e30
LOG2E = math.log2(math.e)

SUBLANES = 8
LANES = 128
ROW_TILE = 512
FFN_ROW_TILE = 512
ATTN_BLOCK = 512
FFN_CHUNK = 512
PAGES_PER_STEP = 16
VMEM_LIMIT = 52 * 1024 * 1024


def _params(n_axes, vmem=VMEM_LIMIT):
    return pltpu.CompilerParams(
        dimension_semantics=("arbitrary",) * n_axes, vmem_limit_bytes=vmem)


def _resident(shape):
    nd = len(shape)
    return pl.BlockSpec(shape, lambda *_: (0,) * nd, pipeline_mode=pl.Buffered(1))


def _sigmoid(x):
    return 0.5 * (1.0 + jnp.tanh(0.5 * x))


def _rms(x, g):
    return x * lax.rsqrt(jnp.mean(x * x, axis=-1, keepdims=True) + EPS) * g


def _inproj_kernel(x_ref, g_ref, w_ref, spad_ref, h0_ref, cw_ref, cb_ref, wa_ref, ba_ref, wx_ref,
                   bx_ref, ap_ref, q2_ref, k_ref, kb_ref, v_ref, vt_ref, gb_ref, za_ref, hlast_ref,
                   xtail_ref, carry_x, carry_h, a_scr, b_scr,
                   *, head_dim, group_mode, pos_base, n_blocks):
    ti = pl.program_id(1)
    tm, d = x_ref.shape
    h = _rms(x_ref[...], g_ref[...]).astype(BF16)

    if not group_mode:
        @pl.when(ti == 0)
        def _():
            carry_x[...] = spad_ref[0]
            carry_h[...] = h0_ref[0]

    pieces = 4
    pc = d // pieces

    def cols_of(j, c):
        return slice(j * d + c * pc, j * d + (c + 1) * pc)

    def part(j, c):
        return jnp.dot(h, w_ref[:, cols_of(j, c)], preferred_element_type=F32)

    def emit_q(c):
        q = part(1, c) * (head_dim ** -0.5 * LOG2E)
        lane = lax.broadcasted_iota(jnp.int32, q.shape, 1)
        first = (lane % (2 * head_dim)) < head_dim
        q2_ref[0, :, cols_of(0, c)] = jnp.where(first, q, 0.0).astype(BF16)
        q2_ref[1, :, cols_of(0, c)] = jnp.where(first, 0.0, q).astype(BF16)

    def emit_k(c):
        k = part(2, c)
        k_ref[:, cols_of(0, c)] = k
        kb_ref[:, cols_of(0, c)] = k.astype(BF16)

    def emit_v(c):
        v = part(3, c)
        v_ref[:, cols_of(0, c)] = v
        vt_ref[cols_of(0, c), :] = v.T.astype(BF16)

    def emit_gb(c):
        gb_ref[:, cols_of(0, c)] = _sigmoid(part(5, c))

    emitters = (emit_q, emit_k, emit_v, emit_gb)
    bpp = n_blocks // pieces
    rq = tm // pieces
    a_cols, b_cols, ga_parts = [], [], []
    for p, emit in enumerate(emitters):
        xr = part(0, p)
        ga_parts.append(_sigmoid(part(4, p)))
        a_rows, b_rows = [], []
        for r in range(pieces):
            a3, b3 = _rglru_group_scan(
                xr[r * rq:(r + 1) * rq], ti, cols_of(0, p), range(p * bpp, (p + 1) * bpp),
                slice(r * rq // SUBLANES, (r + 1) * rq // SUBLANES), spad_ref, cw_ref, cb_ref, wa_ref,
                ba_ref, wx_ref, bx_ref, ap_ref, xtail_ref, carry_x, group_mode=group_mode,
                pos_base=pos_base, first_rows=r == 0)
            a_rows.append(a3)
            b_rows.append(b3)
            emit(r)
        a_cols.append(jnp.concatenate(a_rows, axis=0))
        b_cols.append(jnp.concatenate(b_rows, axis=0))

    h3 = _rglru_carry(jnp.concatenate(a_cols, axis=2), jnp.concatenate(b_cols, axis=2), h0_ref,
                      hlast_ref, carry_h, a_scr, b_scr, group_mode=group_mode)
    za_ref[...] = (jnp.concatenate(ga_parts, axis=1) * h3.reshape(tm, d)).astype(za_ref.dtype)


def _inproj(x2d, g, w_bf, spad, h0b, cw, cb, wa_bf, ba, wx_bf, bx, ap,
            *, head_dim, n_seq, group_mode, pos_base):
    n, d = x2d.shape
    n_blocks = wa_bf.shape[0]
    if group_mode:
        tm, nt, grid = n, 1, (1, 1)
        seq = pl.BlockSpec((n_seq, SUBLANES, d), lambda b, t: (0, 0, 0))
    else:
        t_len = n // n_seq
        tm = min(ROW_TILE, t_len)
        nt = t_len // tm
        grid = (n_seq, nt)
        seq = pl.BlockSpec((1, SUBLANES, d), lambda b, t: (b, 0, 0))
    blk = pl.BlockSpec((tm, d), lambda b, t: (b * nt + t, 0))
    g8 = tm // SUBLANES
    vec = lambda a: a.reshape(1, d)
    f32 = jax.ShapeDtypeStruct((n, d), F32)
    b16 = jax.ShapeDtypeStruct((n, d), BF16)
    state = jax.ShapeDtypeStruct((n_seq, SUBLANES, d), F32)
    return pl.pallas_call(
        functools.partial(_inproj_kernel, head_dim=head_dim, group_mode=group_mode, pos_base=pos_base,
                          n_blocks=n_blocks),
        grid=grid,
        in_specs=[blk, _resident((1, d)), _resident(w_bf.shape), seq, seq, _resident(cw.shape),
                  _resident((1, d)), _resident(wa_bf.shape), _resident((1, d)), _resident(wx_bf.shape),
                  _resident((1, d)), _resident((1, d))],
        out_specs=[pl.BlockSpec((2, tm, d), lambda b, t: (0, b * nt + t, 0)), blk, blk, blk,
                   pl.BlockSpec((d, tm), lambda b, t: (0, b * nt + t)), blk, blk, seq, seq],
        out_shape=[jax.ShapeDtypeStruct((2, n, d), BF16), f32, b16, f32,
                   jax.ShapeDtypeStruct((d, n), BF16), f32, b16, state, state],
        scratch_shapes=[pltpu.VMEM((SUBLANES, d), F32), pltpu.VMEM((SUBLANES, d), F32),
                        pltpu.VMEM((g8, SUBLANES, d), F32), pltpu.VMEM((g8, SUBLANES, d), F32)],
        compiler_params=_params(2),
        name="inproj",
    )(x2d, g.reshape(1, d), w_bf, spad, h0b, cw, vec(cb), wa_bf, vec(ba), wx_bf, vec(bx), vec(ap))


def _shifted(x3, prev3, d, t_idx):
    return jnp.where(t_idx >= d, pltpu.roll(x3, d, 1), pltpu.roll(prev3, d, 1))


def _causal_conv(x3, prev3, w, b):
    width = w.shape[0]
    t_idx = lax.broadcasted_iota(jnp.int32, x3.shape, 1)
    y = b + w[width - 1:width] * x3
    for dd in range(1, width):
        y = y + w[width - 1 - dd:width - dd] * _shifted(x3, prev3, dd, t_idx)
    return y


def _rglru_group_scan(xr, ti, cols, blocks, groups, spad_ref, cw_ref, cb_ref, wa_ref, ba_ref, wx_ref,
                      bx_ref, ap_ref, xtail_ref, carry_x, *, group_mode, pos_base, first_rows):
    tt, c = xr.shape
    g = tt // SUBLANES
    blk = c // len(blocks)
    x3 = xr.reshape(g, SUBLANES, c)

    if group_mode:
        prev3 = spad_ref[groups, :, cols]
        xtail_ref[groups, :, cols] = x3
    else:
        first = carry_x[:, cols][None]
        prev3 = jnp.concatenate([first, x3[:g - 1]], axis=0) if g > 1 else first

    xc3 = _causal_conv(x3, prev3, cw_ref[:, cols], cb_ref[:, cols])
    if not group_mode:
        carry_x[:, cols] = x3[g - 1]
        xtail_ref[0, :, cols] = x3[g - 1]

    xc = xc3.reshape(tt, c)
    xcb = xc.astype(BF16)

    def gate(w_ref, b_ref):
        parts = [jnp.dot(xcb[:, i * blk:(i + 1) * blk], w_ref[n], preferred_element_type=F32)
                 for i, n in enumerate(blocks)]
        return jnp.concatenate(parts, axis=1) + b_ref[:, cols]

    r = _sigmoid(gate(wa_ref, ba_ref))
    i = _sigmoid(gate(wx_ref, bx_ref))
    z = -ap_ref[:, cols]
    softplus = jnp.maximum(z, 0.0) + jnp.log(1.0 + jnp.exp(-jnp.abs(z)))
    log_a = -RG_C * r * softplus
    a = jnp.exp(log_a)
    th = jnp.tanh(log_a)
    mult = jnp.sqrt(-2.0 * th / (1.0 - th))
    ix = i * xc
    a3 = a.reshape(g, SUBLANES, c)
    b3 = (mult * ix).reshape(g, SUBLANES, c)
    t_idx = lax.broadcasted_iota(jnp.int32, a3.shape, 1)
    if pos_base == 0 and (group_mode or first_rows):
        ix3 = ix.reshape(g, SUBLANES, c)
        if group_mode:
            reset = t_idx == 0
            a3 = jnp.where(reset, 0.0, a3)
            b3 = jnp.where(reset, ix3, b3)
        else:
            reset = (lax.broadcasted_iota(jnp.int32, (SUBLANES, c), 0) + ti) == 0
            rest = lambda x3: [x3[1:]] if g > 1 else []
            a3 = jnp.concatenate([jnp.where(reset, 0.0, a3[0])[None]] + rest(a3), axis=0)
            b3 = jnp.concatenate([jnp.where(reset, ix3[0], b3[0])[None]] + rest(b3), axis=0)

    for dd in (1, 2, 4):
        valid = t_idx >= dd
        b3 = jnp.where(valid, a3 * pltpu.roll(b3, dd, 1) + b3, b3)
        a3 = jnp.where(valid, a3 * pltpu.roll(a3, dd, 1), a3)
    return a3, b3


def _rglru_carry(a3, b3, h0_ref, hlast_ref, carry_h, a_scr, b_scr, *, group_mode):
    if group_mode:
        h3 = a3 * h0_ref[...] + b3
        hlast_ref[...] = h3
        return h3
    g, _, c = a3.shape
    a_scr[...] = a3
    b_scr[...] = b3

    def body(gi, carry):
        hg = a_scr[gi] * carry + b_scr[gi]
        b_scr[gi] = hg
        return jnp.broadcast_to(hg[SUBLANES - 1:SUBLANES, :], (SUBLANES, c))

    last = lax.fori_loop(0, g, body, carry_h[...])
    carry_h[...] = last
    hlast_ref[0] = last
    return b_scr[...]


def _bias_of(dist, valid, rb_ref, h):
    n = jnp.maximum(dist, 0)
    max_exact = N_BUCKETS // 2
    nf = jnp.maximum(n, 1).astype(F32)
    large = max_exact + (jnp.log(nf / max_exact) / math.log(MAX_DISTANCE / max_exact)
                         * (N_BUCKETS - max_exact)).astype(jnp.int32)
    large = jnp.minimum(large, N_BUCKETS - 1)
    bucket = jnp.where(n < max_exact, n, large)
    base = rb_ref[N_BUCKETS - 1, h]
    out = jnp.zeros(dist.shape, F32)
    for b in range(N_BUCKETS - 1):
        out = jnp.where(bucket == b, (rb_ref[b, h] - base) * LOG2E, out)
    return jnp.where(valid, out, NEG)


def _bias_kernel(rb_ref, bp_ref, blp_ref, bn_ref, *, tb, n_heads, dec_seq):
    h = pl.program_id(0)
    j = lax.broadcasted_iota(jnp.int32, (LANES, LANES), 0)
    i = lax.broadcasted_iota(jnp.int32, (LANES, LANES), 1)
    for jb in range(2 * tb // LANES):
        for ib in range(tb // LANES):
            base = (ib - jb) * LANES + tb
            rows, cols = slice(jb * LANES, (jb + 1) * LANES), slice(ib * LANES, (ib + 1) * LANES)
            if base - (LANES - 1) >= MAX_DISTANCE:
                bp_ref[0, rows, cols] = jnp.zeros((LANES, LANES), F32)
            elif base + (LANES - 1) < 0:
                bp_ref[0, rows, cols] = jnp.full((LANES, LANES), NEG, F32)
            else:
                dist = base + i - j
                bp_ref[0, rows, cols] = _bias_of(dist, dist >= 0, rb_ref, h)
    rows, page = blp_ref.shape
    t = lax.broadcasted_iota(jnp.int32, (rows, page), 0) % dec_seq
    key = lax.broadcasted_iota(jnp.int32, (rows, page), 1)
    dist = page + t - key
    blp_ref[...] = _bias_of(dist, dist >= 0, rb_ref, h)
    t = lax.broadcasted_iota(jnp.int32, (rows, LANES), 0) % dec_seq
    lane = lax.broadcasted_iota(jnp.int32, (rows, LANES), 1)
    dist = t - lane // n_heads
    bn_ref[...] = _bias_of(dist, jnp.logical_and((lane % n_heads) == h, dist >= 0), rb_ref, h)


def _bias_tiles(rel_bias, n_heads, tb, page, dec_seq):
    rows = 2 * dec_seq
    return pl.pallas_call(
        functools.partial(_bias_kernel, tb=tb, n_heads=n_heads, dec_seq=dec_seq),
        grid=(n_heads,),
        in_specs=[pl.BlockSpec(memory_space=pltpu.SMEM)],
        out_specs=[pl.BlockSpec((1, 2 * tb, tb), lambda h: (h, 0, 0)),
                   pl.BlockSpec((rows, page), lambda h: (h, 0)),
                   pl.BlockSpec((rows, LANES), lambda h: (h, 0))],
        out_shape=[jax.ShapeDtypeStruct((n_heads, 2 * tb, tb), F32),
                   jax.ShapeDtypeStruct((n_heads * rows, page), F32),
                   jax.ShapeDtypeStruct((n_heads * rows, LANES), F32)],
        compiler_params=_params(1),
        name="bias_tiles",
    )(rel_bias)


def _lambda(lq1, lk1, lq2, lk2, lambda_init):
    s1 = jnp.sum(lq1 * lk1, axis=-1, keepdims=True)
    s2 = jnp.sum(lq2 * lk2, axis=-1, keepdims=True)
    return jnp.exp(s1) - jnp.exp(s2) + lambda_init


def _qk(a, b):
    return lax.dot_general(a, b, (((1,), (1,)), ((), ())), preferred_element_type=F32)


def _all_sublanes(x, op):
    for dd in (1, 2, 4):
        x = op(x, pltpu.roll(x, dd, 0))
    return x


def _prompt_attn_kernel(q2_ref, k_ref, vt_ref, gb_ref, bias_ref, lq1_ref, lk1_ref, lq2_ref, lk2_ref,
                        sg_ref, o_ref, s0_ref, s1_ref, m0_ref, m1_ref, l0_ref, l1_ref, a0_ref, a1_ref,
                        pl0_ref, pl1_ref, pa0_ref, pa1_ref, *, tb, lambda_init):
    e, t_len = vt_ref.shape
    nq = t_len // tb
    s_refs = (s0_ref, s1_ref)
    m_refs = (m0_ref, m1_ref)
    l_refs = (l0_ref, l1_ref)
    acc_refs = (a0_ref, a1_ref)
    pl_refs = (pl0_ref, pl1_ref)
    pa_refs = (pa0_ref, pa1_ref)

    def block(i):
        return pl.ds(i * tb if isinstance(i, int) else pl.multiple_of(i * tb, tb), tb)

    def init():
        for mm in range(2):
            m_refs[mm][...] = jnp.full(m_refs[mm].shape, NEG, F32)
            l_refs[mm][...] = jnp.zeros(l_refs[mm].shape, F32)
            acc_refs[mm][...] = jnp.zeros(acc_refs[mm].shape, F32)

    def scores(mm, qi, kj):
        return _qk(k_ref[block(kj), :], q2_ref[mm, block(qi), :])

    def consume(mm, kj, bias):
        s = s_refs[mm][...]
        if bias is not None:
            s = s + bias
        s3 = s.reshape(tb // SUBLANES, SUBLANES, tb)
        m_prev = m_refs[mm][...]
        m_new = jnp.maximum(m_prev, _all_sublanes(jnp.max(s3, axis=0), jnp.maximum))
        alpha = jnp.exp2(m_prev - m_new)
        p3 = jnp.exp2(s3 - m_new[None])
        l_refs[mm][...] = alpha * l_refs[mm][...] + _all_sublanes(jnp.sum(p3, axis=0), jnp.add)
        pv = jnp.dot(vt_ref[:, block(kj)], p3.reshape(tb, tb).astype(BF16),
                     preferred_element_type=F32)
        acc = acc_refs[mm][...].reshape(e // SUBLANES, SUBLANES, tb) * alpha[None]
        acc_refs[mm][...] = acc.reshape(e, tb) + pv
        m_refs[mm][...] = m_new

    def step(qi, kj, bias, next_q, next_k):
        s1_ref[...] = scores(1, qi, kj)
        consume(0, kj, bias)
        s0_ref[...] = scores(0, next_q, next_k)
        consume(1, kj, bias)

    lam = _lambda(lq1_ref[...], lk1_ref[...], lq2_ref[...], lk2_ref[...], lambda_init)

    def park():
        for mm in range(2):
            pa_refs[mm][...] = acc_refs[mm][...]
            pl_refs[mm][...] = l_refs[mm][...]

    def finalize_parked(qi):
        def normalised(mm):
            acc = pa_refs[mm][...].reshape(e // SUBLANES, SUBLANES, tb)
            return (acc / pl_refs[mm][...][None]).reshape(e, tb)

        o = (normalised(0) - lam * normalised(1)).T
        y = _rms(o, sg_ref[...]) * (1.0 - lambda_init)
        o_ref[block(qi), :] = (gb_ref[block(qi), :] * y).astype(o_ref.dtype)

    bias_prev = lambda: bias_ref[0, :tb, :]
    bias_diag = lambda: bias_ref[0, tb:, :]

    init()
    s0_ref[...] = scores(0, 0, 0)
    step(0, 0, bias_diag(), 1 if nq > 1 else 0, 0)
    park()

    def tile(qi, carry):
        init()
        n_far = qi - 1
        one = n_far & 1
        two = n_far & 2

        @pl.when(one != 0)
        def _():
            step(qi, 0, None, qi, 1)

        @pl.when(two != 0)
        def _():
            step(qi, one, None, qi, one + 1)
            step(qi, one + 1, None, qi, one + 2)

        def far_quad(jj, c):
            j = one + two + 4 * jj
            for u in range(4):
                step(qi, j + u, None, qi, j + u + 1)
            return c

        lax.fori_loop(0, lax.shift_right_logical(n_far, 2), far_quad, 0)
        finalize_parked(qi - 1)
        step(qi, qi - 1, bias_prev(), qi, qi)
        step(qi, qi, bias_diag(), jnp.minimum(qi + 1, nq - 1), 0)
        park()
        return carry

    lax.fori_loop(1, nq, tile, 0)
    finalize_parked(nq - 1)


def _prompt_attn(q2, kb, vt, gb, bias_p, lams, subln_g, *, n_seq, n_heads, lambda_init):
    _, n, d = q2.shape
    e = d // n_heads
    t_len = n // n_seq
    tb = ATTN_BLOCK
    hd = lams[0].shape[-1]
    small = pl.BlockSpec((1, hd), lambda b, h: (0, 0))
    rows = pl.BlockSpec((t_len, e), lambda b, h: (b, h))
    stat = pltpu.VMEM((SUBLANES, tb), F32)
    return pl.pallas_call(
        functools.partial(_prompt_attn_kernel, tb=tb, lambda_init=lambda_init),
        grid=(n_seq, n_heads),
        in_specs=[pl.BlockSpec((2, t_len, e), lambda b, h: (0, b, h)),
                  rows,
                  pl.BlockSpec((e, t_len), lambda b, h: (h, b)),
                  rows,
                  pl.BlockSpec((1, 2 * tb, tb), lambda b, h: (h, 0, 0)),
                  small, small, small, small,
                  pl.BlockSpec((1, e), lambda b, h: (0, 0))],
        out_specs=rows,
        out_shape=jax.ShapeDtypeStruct((n, d), BF16),
        scratch_shapes=[pltpu.VMEM((tb, tb), F32), pltpu.VMEM((tb, tb), F32),
                        stat, stat, stat, stat,
                        pltpu.VMEM((e, tb), F32), pltpu.VMEM((e, tb), F32),
                        stat, stat,
                        pltpu.VMEM((e, tb), F32), pltpu.VMEM((e, tb), F32)],
        compiler_params=_params(2),
        name="prompt_attn",
    )(q2, kb, vt, gb, bias_p, *[x.reshape(1, hd) for x in lams], subln_g.reshape(1, e))


def _as_column(stat):
    r = stat.shape[1]
    row = lax.broadcasted_iota(jnp.int32, (r, r), 0)
    col = lax.broadcasted_iota(jnp.int32, (r, r), 1)
    full = jnp.concatenate([stat] * (r // SUBLANES), axis=0)
    return jnp.sum(jnp.where(row == col, full, 0.0), axis=1, keepdims=True)


def _sample_attn_kernel(pt_ref, q2_ref, *refs, n_heads, dec_seq, pps, lambda_init):
    kp = refs[:pps]
    vp = refs[pps:2 * pps]
    (kn_ref, vn_ref, gb_ref, blp_ref, bn_ref, lq1_ref, lk1_ref, lq2_ref, lk2_ref, sg_ref,
     o_ref, q_ref, qt_ref, blpt_ref, m_ref, l_ref, acc_ref) = refs[2 * pps:]
    del pt_ref
    c = pl.program_id(1)
    nc = pl.num_programs(1)
    e = kp[0].shape[1]
    d = n_heads * e
    page = kp[0].shape[0] // n_heads
    rows = 2 * dec_seq

    @pl.when(c == 0)
    def _():
        lane_head = lax.broadcasted_iota(jnp.int32, (dec_seq, d), 1) // e
        q_ref[...] = jnp.concatenate(
            [q2_ref[mm][:, h * e:(h + 1) * e] for h in range(n_heads) for mm in range(2)], axis=0)
        qbd = jnp.concatenate(
            [jnp.where(lane_head == h, q2_ref[mm], 0.0) for h in range(n_heads) for mm in range(2)],
            axis=0)
        qt_ref[...] = qbd.T
        blpt_ref[...] = blp_ref[...].T
        m_ref[...] = jnp.full(m_ref.shape, NEG, F32)
        l_ref[...] = jnp.zeros(l_ref.shape, F32)
        acc_ref[...] = jnp.zeros(acc_ref.shape, F32)

    def head_rows(ref, h):
        return ref[pl.ds(h, page, stride=n_heads), :]

    is_last = (c == nc - 1).astype(F32)
    parts = []
    for j in range(pps):
        keys = jnp.concatenate([head_rows(kp[j], h) for h in range(n_heads)], axis=1)
        parts.append(jnp.dot(keys, qt_ref[...], preferred_element_type=F32))
    parts[-1] = parts[-1] + blpt_ref[...] * is_last
    s3 = jnp.concatenate(parts, axis=0).reshape(pps * page // SUBLANES, SUBLANES, -1)
    m_prev = m_ref[...]
    m_new = jnp.maximum(m_prev, _all_sublanes(jnp.max(s3, axis=0), jnp.maximum))
    alpha = jnp.exp2(m_prev - m_new)
    p3 = jnp.exp2(s3 - m_new[None])
    l_ref[...] = alpha * l_ref[...] + _all_sublanes(jnp.sum(p3, axis=0), jnp.add)
    m_ref[...] = m_new
    p = p3.reshape(pps * page, -1).T
    out = []
    for h in range(n_heads):
        ph = p[h * rows:(h + 1) * rows]
        pv = jnp.dot(ph[:, :page], head_rows(vp[0], h), preferred_element_type=F32)
        for j in range(1, pps):
            pv = pv + jnp.dot(ph[:, j * page:(j + 1) * page], head_rows(vp[j], h),
                              preferred_element_type=F32)
        out.append(pv)
    acc_ref[...] = _as_column(alpha) * acc_ref[...] + jnp.concatenate(out, axis=0)

    @pl.when(c == nc - 1)
    def _():
        s = _qk(q_ref[...], kn_ref[...]) + bn_ref[...]
        m_old = _as_column(m_ref[...])
        m_fin = jnp.maximum(m_old, jnp.max(s, axis=-1, keepdims=True))
        scale = jnp.exp2(m_old - m_fin)
        pn = jnp.exp2(s - m_fin)
        l_fin = scale * _as_column(l_ref[...]) + jnp.sum(pn, axis=-1, keepdims=True)
        acc = scale * acc_ref[...] + jnp.dot(pn, vn_ref[...], preferred_element_type=F32)
        lam = _lambda(lq1_ref[...], lk1_ref[...], lq2_ref[...], lk2_ref[...], lambda_init)
        o_all = acc / l_fin
        outs = []
        for h in range(n_heads):
            r0 = h * rows
            o = o_all[r0:r0 + dec_seq] - lam * o_all[r0 + dec_seq:r0 + rows]
            outs.append(_rms(o, sg_ref[...]) * (1.0 - lambda_init))
        o_ref[...] = gb_ref[...] * jnp.concatenate(outs, axis=1)


def _sample_attn(page_table, q2f, cache_k, cache_v, kn_rows, vn_rows, gb, bias_lp, bias_n, lams,
                 subln_g, *, n_seq, n_heads, lambda_init):
    _, n, d = q2f.shape
    dec_seq = n // n_seq
    e = d // n_heads
    n_pool, page = cache_k.shape[:2]
    n_pages = page_table.shape[1]
    pps = PAGES_PER_STEP
    nc = n_pages // pps
    hd = lams[0].shape[-1]
    rows = n_heads * 2 * dec_seq
    width = page * n_heads

    def page_spec(j):
        return pl.BlockSpec((None, width, e), lambda b, c, pt: (pt[b, c * pps + j], 0, 0))

    const2 = lambda b, c, pt: (0, 0)
    small = pl.BlockSpec((1, hd), const2)
    new_rows = pl.BlockSpec((LANES, e), lambda b, c, pt: (b, 0))
    seq = pl.BlockSpec((dec_seq, d), lambda b, c, pt: (b, 0))
    grid_spec = pltpu.PrefetchScalarGridSpec(
        num_scalar_prefetch=1,
        grid=(n_seq, nc),
        in_specs=[pl.BlockSpec((2, dec_seq, d), lambda b, c, pt: (0, b, 0))]
                 + [page_spec(j) for j in range(pps)] + [page_spec(j) for j in range(pps)]
                 + [new_rows, new_rows, seq,
                    pl.BlockSpec((rows, page), const2), pl.BlockSpec((rows, LANES), const2),
                    small, small, small, small, pl.BlockSpec((1, e), const2)],
        out_specs=seq,
        scratch_shapes=[pltpu.VMEM((rows, e), F32), pltpu.VMEM((d, rows), F32),
                        pltpu.VMEM((page, rows), F32), pltpu.VMEM((SUBLANES, rows), F32),
                        pltpu.VMEM((SUBLANES, rows), F32), pltpu.VMEM((rows, e), F32)],
    )
    ck = cache_k.reshape(n_pool, width, e)
    cv = cache_v.reshape(n_pool, width, e)
    return pl.pallas_call(
        functools.partial(_sample_attn_kernel, n_heads=n_heads, dec_seq=dec_seq, pps=pps,
                          lambda_init=lambda_init),
        grid_spec=grid_spec,
        out_shape=jax.ShapeDtypeStruct((n, d), F32),
        compiler_params=_params(2),
        name="sample_attn",
    )(page_table, q2f, *([ck] * pps), *([cv] * pps), kn_rows, vn_rows, gb, bias_lp, bias_n,
      *[x.reshape(1, hd) for x in lams], subln_g.reshape(1, e))


def _new_token_rows(x2d, n_seq, n_heads):
    n, d = x2d.shape
    e = d // n_heads
    per_seq = (n // n_seq) * n_heads
    x = x2d.reshape(n_seq, per_seq, e)
    x = jnp.pad(x, ((0, 0), (0, LANES - per_seq), (0, 0)))
    return x.reshape(n_seq * LANES, e)


def _gelu_tanh(x):
    return 0.5 * x * (1.0 + jnp.tanh(math.sqrt(2.0 / math.pi) * (x + 0.044715 * (x * x * x))))


def _ffn_kernel(x_ref, za_ref, zb_ref, spad_ref, wout_ref, nffn_ref, wup_ref,
                cw_ref, cb_ref, wdown_ref, nfin_ref, y_ref, tail_ref, carry_ref, act_ref,
                *, group_mode, final_norm):
    ti = pl.program_id(1)
    tm, d = x_ref.shape
    g = tm // SUBLANES
    d_ff = wdown_ref.shape[0]

    merged = (za_ref[...].astype(F32) + zb_ref[...].astype(F32)).astype(BF16)
    x2 = x_ref[...] + jnp.dot(merged, wout_ref[...], preferred_element_type=F32)
    h2 = _rms(x2, nffn_ref[...]).astype(BF16)

    if not group_mode:
        @pl.when(ti == 0)
        def _():
            carry_ref[...] = spad_ref[0]

    def up_conv(cols):
        up3 = jnp.dot(h2, wup_ref[:, cols], preferred_element_type=F32).reshape(g, SUBLANES, -1)
        if group_mode:
            prev3 = spad_ref[:, :, cols]
            tail_ref[:, :, cols] = up3
        else:
            first = carry_ref[:, cols][None]
            prev3 = jnp.concatenate([first, up3[:g - 1]], axis=0) if g > 1 else first
            carry_ref[:, cols] = up3[g - 1]
            tail_ref[0, :, cols] = up3[g - 1]
        return _causal_conv(up3, prev3, cw_ref[:, cols], cb_ref[:, cols]).reshape(tm, -1)

    for c0 in range(0, d_ff, FFN_CHUNK):
        gate = up_conv(slice(c0, c0 + FFN_CHUNK))
        val = up_conv(slice(d_ff + c0, d_ff + c0 + FFN_CHUNK))
        act_ref[:, c0:c0 + FFN_CHUNK] = (_gelu_tanh(gate) * val).astype(BF16)
    x3 = x2 + jnp.dot(act_ref[...], wdown_ref[...], preferred_element_type=F32)
    y_ref[...] = _rms(x3, nfin_ref[...]) if final_norm else x3


def _ffn(x2d, za, zb, spad, wout_bf, nffn, wup_bf, cw, cb, wdown_bf, nfin,
         *, n_seq, group_mode, final_norm):
    n, d = x2d.shape
    f2 = wup_bf.shape[1]
    if group_mode:
        tm, grid = n, (1, 1)
        row = pl.BlockSpec((n, d), lambda b, t: (0, 0))
        seq = pl.BlockSpec((n_seq, SUBLANES, f2), lambda b, t: (0, 0, 0))
    else:
        t_len = n // n_seq
        tm = min(FFN_ROW_TILE, t_len)
        nt = t_len // tm
        grid = (n_seq, nt)
        row = pl.BlockSpec((tm, d), lambda b, t: (b * nt + t, 0))
        seq = pl.BlockSpec((1, SUBLANES, f2), lambda b, t: (b, 0, 0))
    return pl.pallas_call(
        functools.partial(_ffn_kernel, group_mode=group_mode, final_norm=final_norm),
        grid=grid,
        in_specs=[row, row, row, seq, _resident(wout_bf.shape), _resident((1, d)),
                  _resident(wup_bf.shape), _resident(cw.shape), _resident((1, f2)),
                  _resident(wdown_bf.shape), _resident((1, d))],
        out_specs=[row, seq],
        out_shape=[jax.ShapeDtypeStruct((n, d), F32), jax.ShapeDtypeStruct((n_seq, SUBLANES, f2), F32)],
        scratch_shapes=[pltpu.VMEM((SUBLANES, f2), F32), pltpu.VMEM((tm, f2 // 2), BF16)],
        compiler_params=_params(2),
        name="merge_ffn",
    )(x2d, za, zb, spad, wout_bf, nffn.reshape(1, d), wup_bf, cw, cb.reshape(1, f2),
      wdown_bf, nfin.reshape(1, d))


def _pad_state(buf):
    b, w1, c = buf.shape
    return jnp.concatenate([jnp.zeros((b, SUBLANES - w1, c), buf.dtype), buf], axis=1)


def kernel(x_prompt, x_sample, cache_k, cache_v, page_table, state_rglru_h, state_rglru_conv, state_ffn_conv, w_in, conv_w, conv_b, rg_w_a, rg_b_a, rg_w_x, rg_b_x, rg_a_param, lambda_q1, lambda_k1, lambda_q2, lambda_k2, subln_g, rel_bias, w_out, norm_attn, norm_ffn, w_up, ffn_conv_w, ffn_conv_b, w_down, norm_final):
    bsz, t_len, d = x_prompt.shape
    dbsz, dec_seq, _ = x_sample.shape
    depth = w_in.shape[0]
    n_heads = cache_k.shape[3]
    kv_dim = cache_k.shape[4]
    head_dim = kv_dim // 2
    page = cache_k.shape[2]
    past = page_table.shape[1] * page
    cw_w = conv_w.shape[1]
    fw_w = ffn_conv_w.shape[1]
    assert w_in.shape[2] == 6 * d and n_heads * kv_dim == d and rg_a_param.shape[1] == d
    assert dec_seq == SUBLANES and t_len % ATTN_BLOCK == 0 and page_table.shape[1] % PAGES_PER_STEP == 0
    assert ATTN_BLOCK >= MAX_DISTANCE and page >= MAX_DISTANCE and dec_seq * n_heads <= LANES
    assert kv_dim == LANES and n_heads == SUBLANES

    bias_p, bias_lp, bias_n = _bias_tiles(rel_bias, n_heads, ATTN_BLOCK, page, dec_seq)

    xp = x_prompt.reshape(bsz * t_len, d)
    xs = x_sample.reshape(dbsz * dec_seq, d)
    outs = [[] for _ in range(10)]
    for l in range(depth):
        lambda_init = 0.8 - 0.6 * math.exp(-0.3 * l)
        last = l == depth - 1
        w_in_bf = w_in[l].astype(BF16)
        wa_bf = rg_w_a[l].astype(BF16)
        wx_bf = rg_w_x[l].astype(BF16)
        wout_bf = w_out[l].astype(BF16)
        wup_bf = w_up[l].astype(BF16)
        wdown_bf = w_down[l].astype(BF16)
        lams = (lambda_q1[l], lambda_k1[l], lambda_q2[l], lambda_k2[l])

        def project(x2d, spad, h0, n_seq, group_mode, pos_base):
            h0b = jnp.broadcast_to(h0[:, None, :], (n_seq, SUBLANES, d))
            return _inproj(x2d, norm_attn[l], w_in_bf, spad, h0b, conv_w[l], conv_b[l], wa_bf,
                           rg_b_a[l], wx_bf, rg_b_x[l], rg_a_param[l], head_dim=head_dim,
                           n_seq=n_seq, group_mode=group_mode, pos_base=pos_base)

        def channel_mix(x2d, za, zb, spad, n_seq, group_mode):
            return _ffn(x2d, za, zb, spad, wout_bf, norm_ffn[l], wup_bf, ffn_conv_w[l],
                        ffn_conv_b[l], wdown_bf, norm_final, n_seq=n_seq, group_mode=group_mode,
                        final_norm=last)

        q2, k, kb, v, vt, gb, za, hp, xtail = project(
            xp, jnp.zeros((bsz, SUBLANES, d), F32), jnp.zeros((bsz, d), F32), bsz, False, 0)
        zb = _prompt_attn(q2, kb, vt, gb, bias_p, lams, subln_g[l], n_seq=bsz, n_heads=n_heads,
                          lambda_init=lambda_init)
        xp, fp = channel_mix(xp, za, zb, jnp.zeros((bsz, SUBLANES, ffn_conv_w.shape[2]), F32),
                             bsz, False)
        outs[0].append(k.reshape(bsz, t_len, n_heads, kv_dim))
        outs[1].append(v.reshape(bsz, t_len, n_heads, kv_dim))
        outs[4].append(hp[:, SUBLANES - 1])
        outs[6].append(xtail[:, SUBLANES - (cw_w - 1):])
        outs[8].append(fp[:, SUBLANES - (fw_w - 1):])

        q2, k, _, v, _, gb, za, hs, xtail = project(
            xs, _pad_state(state_rglru_conv[l]), state_rglru_h[l], dbsz, True, past)
        zb = _sample_attn(page_table, q2.astype(F32), cache_k[l], cache_v[l],
                          _new_token_rows(k, dbsz, n_heads), _new_token_rows(v, dbsz, n_heads),
                          gb, bias_lp, bias_n, lams, subln_g[l], n_seq=dbsz, n_heads=n_heads,
                          lambda_init=lambda_init)
        xs, fs = channel_mix(xs, za, zb, _pad_state(state_ffn_conv[l]), dbsz, True)
        outs[2].append(k.reshape(dbsz, dec_seq, n_heads, kv_dim))
        outs[3].append(v.reshape(dbsz, dec_seq, n_heads, kv_dim))
        outs[5].append(hs[:, SUBLANES - 1])
        outs[7].append(xtail[:, SUBLANES - (cw_w - 1):])
        outs[9].append(fs[:, SUBLANES - (fw_w - 1):])

    return (xp.reshape(bsz, t_len, d), xs.reshape(dbsz, dec_seq, d),
            jnp.stack(outs[0]), jnp.stack(outs[1]), jnp.stack(outs[2]), jnp.stack(outs[3]),
            jnp.stack(outs[4]), jnp.stack(outs[5]), jnp.stack(outs[6]), jnp.stack(outs[7]),
            jnp.stack(outs[8]), jnp.stack(outs[9]))
```

```python
import functools
import math

import jax
import jax.numpy as jnp
from jax import lax
from jax.experimental import pallas as pl
from jax.experimental.pallas import tpu as pltpu

F32 = jnp.float32
BF16 = jnp.bfloat16

EPS = 1e-6
RG_C = 8.0
N_BUCKETS = 32
MAX_DISTANCE = 128
NEG = -1e30
LOG2E = math.log2(math.e)

SUBLANES = 8
LANES = 128
ROW_TILE = 512
FFN_ROW_TILE = 512
ATTN_BLOCK = 512
FFN_CHUNK = 512
PAGES_PER_STEP = 16
VMEM_LIMIT = 52 * 1024 * 1024


def _params(n_axes, vmem=VMEM_LIMIT):
    return pltpu.CompilerParams(
        dimension_semantics=("arbitrary",) * n_axes, vmem_limit_bytes=vmem)


def _resident(shape):
    nd = len(shape)
    return pl.BlockSpec(shape, lambda *_: (0,) * nd, pipeline_mode=pl.Buffered(1))


def _sigmoid(x):
    return 0.5 * (1.0 + jnp.tanh(0.5 * x))


def _rms(x, g):
    return x * lax.rsqrt(jnp.mean(x * x, axis=-1, keepdims=True) + EPS) * g


def _inproj_kernel(x_ref, g_ref, w_ref, spad_ref, h0_ref, cw_ref, cb_ref, wa_ref, ba_ref, wx_ref,
                   bx_ref, ap_ref, q2_ref, k_ref, kb_ref, v_ref, vt_ref, gb_ref, za_ref, hlast_ref,
                   xtail_ref, carry_x, carry_h, a_scr, b_scr,
                   *, head_dim, group_mode, pos_base, n_blocks):
    ti = pl.program_id(1)
    tm, d = x_ref.shape
    h = _rms(x_ref[...], g_ref[...]).astype(BF16)

    if not group_mode:
        @pl.when(ti == 0)
        def _():
            carry_x[...] = spad_ref[0]
            carry_h[...] = h0_ref[0]

    pieces = 4
    pc = d // pieces
    assert n_blocks % pieces == 0 and tm % (pieces * SUBLANES) == 0 and pc % LANES == 0

    def cols_of(j, c):
        return slice(j * d + c * pc, j * d + (c + 1) * pc)

    def part(j, c):
        return jnp.dot(h, w_ref[:, cols_of(j, c)], preferred_element_type=F32)

    def emit_q(c):
        q = part(1, c) * (head_dim ** -0.5 * LOG2E)
        lane = lax.broadcasted_iota(jnp.int32, q.shape, 1)
        first = (lane % (2 * head_dim)) < head_dim
        q2_ref[0, :, cols_of(0, c)] = jnp.where(first, q, 0.0).astype(BF16)
        q2_ref[1, :, cols_of(0, c)] = jnp.where(first, 0.0, q).astype(BF16)

    def emit_k(c):
        k = part(2, c)
        k_ref[:, cols_of(0, c)] = k
        kb_ref[:, cols_of(0, c)] = k.astype(BF16)

    def emit_v(c):
        v = part(3, c)
        v_ref[:, cols_of(0, c)] = v
        vt_ref[cols_of(0, c), :] = v.T.astype(BF16)

    def emit_gb(c):
        gb_ref[:, cols_of(0, c)] = _sigmoid(part(5, c))

    emitters = (emit_q, emit_k, emit_v, emit_gb)
    bpp = n_blocks // pieces
    rq = tm // pieces
    a_cols, b_cols, ga_parts = [], [], []
    for p, emit in enumerate(emitters):
        xr = part(0, p)
        ga_parts.append(_sigmoid(part(4, p)))
        a_rows, b_rows = [], []
        for r in range(pieces):
            a3, b3 = _rglru_group_scan(
                xr[r * rq:(r + 1) * rq], ti, cols_of(0, p), range(p * bpp, (p + 1) * bpp),
                slice(r * rq // SUBLANES, (r + 1) * rq // SUBLANES), spad_ref, cw_ref, cb_ref, wa_ref,
                ba_ref, wx_ref, bx_ref, ap_ref, xtail_ref, carry_x, group_mode=group_mode,
                pos_base=pos_base, first_rows=r == 0)
            a_rows.append(a3)
            b_rows.append(b3)
            emit(r)
        a_cols.append(jnp.concatenate(a_rows, axis=0))
        b_cols.append(jnp.concatenate(b_rows, axis=0))

    h3 = _rglru_carry(jnp.concatenate(a_cols, axis=2), jnp.concatenate(b_cols, axis=2), h0_ref,
                      hlast_ref, carry_h, a_scr, b_scr, group_mode=group_mode)
    za_ref[...] = (jnp.concatenate(ga_parts, axis=1) * h3.reshape(tm, d)).astype(za_ref.dtype)


def _inproj(x2d, g, w_bf, spad, h0b, cw, cb, wa_bf, ba, wx_bf, bx, ap,
            *, head_dim, n_seq, group_mode, pos_base):
    n, d = x2d.shape
    n_blocks = wa_bf.shape[0]
    if group_mode:
        tm, nt, grid = n, 1, (1, 1)
        seq = pl.BlockSpec((n_seq, SUBLANES, d), lambda b, t: (0, 0, 0))
    else:
        t_len = n // n_seq
        tm = min(ROW_TILE, t_len)
        nt = t_len // tm
        grid = (n_seq, nt)
        seq = pl.BlockSpec((1, SUBLANES, d), lambda b, t: (b, 0, 0))
    blk = pl.BlockSpec((tm, d), lambda b, t: (b * nt + t, 0))
    g8 = tm // SUBLANES
    vec = lambda a: a.reshape(1, d)
    f32 = jax.ShapeDtypeStruct((n, d), F32)
    b16 = jax.ShapeDtypeStruct((n, d), BF16)
    state = jax.ShapeDtypeStruct((n_seq, SUBLANES, d), F32)
    return pl.pallas_call(
        functools.partial(_inproj_kernel, head_dim=head_dim, group_mode=group_mode, pos_base=pos_base,
                          n_blocks=n_blocks),
        grid=grid,
        in_specs=[blk, _resident((1, d)), _resident(w_bf.shape), seq, seq, _resident(cw.shape),
                  _resident((1, d)), _resident(wa_bf.shape), _resident((1, d)), _resident(wx_bf.shape),
                  _resident((1, d)), _resident((1, d))],
        out_specs=[pl.BlockSpec((2, tm, d), lambda b, t: (0, b * nt + t, 0)), blk, blk, blk,
                   pl.BlockSpec((d, tm), lambda b, t: (0, b * nt + t)), blk, blk, seq, seq],
        out_shape=[jax.ShapeDtypeStruct((2, n, d), BF16), f32, b16, f32,
                   jax.ShapeDtypeStruct((d, n), BF16), f32, b16, state, state],
        scratch_shapes=[pltpu.VMEM((SUBLANES, d), F32), pltpu.VMEM((SUBLANES, d), F32),
                        pltpu.VMEM((g8, SUBLANES, d), F32), pltpu.VMEM((g8, SUBLANES, d), F32)],
        compiler_params=_params(2),
        name="inproj",
    )(x2d, g.reshape(1, d), w_bf, spad, h0b, cw, vec(cb), wa_bf, vec(ba), wx_bf, vec(bx), vec(ap))


def _shifted(x3, prev3, d, t_idx):
    return jnp.where(t_idx >= d, pltpu.roll(x3, d, 1), pltpu.roll(prev3, d, 1))


def _causal_conv(x3, prev3, w, b):
    width = w.shape[0]
    t_idx = lax.broadcasted_iota(jnp.int32, x3.shape, 1)
    y = b + w[width - 1:width] * x3
    for dd in range(1, width):
        y = y + w[width - 1 - dd:width - dd] * _shifted(x3, prev3, dd, t_idx)
    return y


def _rglru_group_scan(xr, ti, cols, blocks, groups, spad_ref, cw_ref, cb_ref, wa_ref, ba_ref, wx_ref,
                      bx_ref, ap_ref, xtail_ref, carry_x, *, group_mode, pos_base, first_rows):
    tt, c = xr.shape
    g = tt // SUBLANES
    blk = c // len(blocks)
    x3 = xr.reshape(g, SUBLANES, c)

    if group_mode:
        prev3 = spad_ref[groups, :, cols]
        xtail_ref[groups, :, cols] = x3
    else:
        first = carry_x[:, cols][None]
        prev3 = jnp.concatenate([first, x3[:g - 1]], axis=0) if g > 1 else first

    xc3 = _causal_conv(x3, prev3, cw_ref[:, cols], cb_ref[:, cols])
    if not group_mode:
        carry_x[:, cols] = x3[g - 1]
        xtail_ref[0, :, cols] = x3[g - 1]

    xc = xc3.reshape(tt, c)
    xcb = xc.astype(BF16)

    def gate(w_ref, b_ref):
        parts = [jnp.dot(xcb[:, i * blk:(i + 1) * blk], w_ref[n], preferred_element_type=F32)
                 for i, n in enumerate(blocks)]
        return jnp.concatenate(parts, axis=1) + b_ref[:, cols]

    r = _sigmoid(gate(wa_ref, ba_ref))
    i = _sigmoid(gate(wx_ref, bx_ref))
    z = -ap_ref[:, cols]
    softplus = jnp.maximum(z, 0.0) + jnp.log(1.0 + jnp.exp(-jnp.abs(z)))
    log_a = -RG_C * r * softplus
    a = jnp.exp(log_a)
    th = jnp.tanh(log_a)
    mult = jnp.sqrt(-2.0 * th / (1.0 - th))
    ix = i * xc
    a3 = a.reshape(g, SUBLANES, c)
    b3 = (mult * ix).reshape(g, SUBLANES, c)
    t_idx = lax.broadcasted_iota(jnp.int32, a3.shape, 1)
    if pos_base == 0 and (group_mode or first_rows):
        ix3 = ix.reshape(g, SUBLANES, c)
        if group_mode:
            reset = t_idx == 0
            a3 = jnp.where(reset, 0.0, a3)
            b3 = jnp.where(reset, ix3, b3)
        else:
            reset = (lax.broadcasted_iota(jnp.int32, (SUBLANES, c), 0) + ti) == 0
            rest = lambda x3: [x3[1:]] if g > 1 else []
            a3 = jnp.concatenate([jnp.where(reset, 0.0, a3[0])[None]] + rest(a3), axis=0)
            b3 = jnp.concatenate([jnp.where(reset, ix3[0], b3[0])[None]] + rest(b3), axis=0)

    for dd in (1, 2, 4):
        valid = t_idx >= dd
        b3 = jnp.where(valid, a3 * pltpu.roll(b3, dd, 1) + b3, b3)
        a3 = jnp.where(valid, a3 * pltpu.roll(a3, dd, 1), a3)
    return a3, b3


def _rglru_carry(a3, b3, h0_ref, hlast_ref, carry_h, a_scr, b_scr, *, group_mode):
    if group_mode:
        h3 = a3 * h0_ref[...] + b3
        hlast_ref[...] = h3
        return h3
    g, _, c = a3.shape
    a_scr[...] = a3
    b_scr[...] = b3

    def body(gi, carry):
        hg = a_scr[gi] * carry + b_scr[gi]
        b_scr[gi] = hg
        return jnp.broadcast_to(hg[SUBLANES - 1:SUBLANES, :], (SUBLANES, c))

    last = lax.fori_loop(0, g, body, carry_h[...])
    carry_h[...] = last
    hlast_ref[0] = last
    return b_scr[...]


def _bias_of(dist, valid, rb_ref, h):
    n = jnp.maximum(dist, 0)
    max_exact = N_BUCKETS // 2
    nf = jnp.maximum(n, 1).astype(F32)
    large = max_exact + (jnp.log(nf / max_exact) / math.log(MAX_DISTANCE / max_exact)
                         * (N_BUCKETS - max_exact)).astype(jnp.int32)
    large = jnp.minimum(large, N_BUCKETS - 1)
    bucket = jnp.where(n < max_exact, n, large)
    base = rb_ref[N_BUCKETS - 1, h]
    out = jnp.zeros(dist.shape, F32)
    for b in range(N_BUCKETS - 1):
        out = jnp.where(bucket == b, (rb_ref[b, h] - base) * LOG2E, out)
    return jnp.where(valid, out, NEG)


def _bias_kernel(rb_ref, bp_ref, blp_ref, bn_ref, *, tb, n_heads, dec_seq):
    h = pl.program_id(0)
    j = lax.broadcasted_iota(jnp.int32, (LANES, LANES), 0)
    i = lax.broadcasted_iota(jnp.int32, (LANES, LANES), 1)
    for jb in range(2 * tb // LANES):
        for ib in range(tb // LANES):
            base = (ib - jb) * LANES + tb
            rows, cols = slice(jb * LANES, (jb + 1) * LANES), slice(ib * LANES, (ib + 1) * LANES)
            if base - (LANES - 1) >= MAX_DISTANCE:
                bp_ref[0, rows, cols] = jnp.zeros((LANES, LANES), F32)
            elif base + (LANES - 1) < 0:
                bp_ref[0, rows, cols] = jnp.full((LANES, LANES), NEG, F32)
            else:
                dist = base + i - j
                bp_ref[0, rows, cols] = _bias_of(dist, dist >= 0, rb_ref, h)
    rows, page = blp_ref.shape
    t = lax.broadcasted_iota(jnp.int32, (rows, page), 0) % dec_seq
    key = lax.broadcasted_iota(jnp.int32, (rows, page), 1)
    dist = page + t - key
    blp_ref[...] = _bias_of(dist, dist >= 0, rb_ref, h)
    t = lax.broadcasted_iota(jnp.int32, (rows, LANES), 0) % dec_seq
    lane = lax.broadcasted_iota(jnp.int32, (rows, LANES), 1)
    dist = t - lane // n_heads
    bn_ref[...] = _bias_of(dist, jnp.logical_and((lane % n_heads) == h, dist >= 0), rb_ref, h)


def _bias_tiles(rel_bias, n_heads, tb, page, dec_seq):
    rows = 2 * dec_seq
    return pl.pallas_call(
        functools.partial(_bias_kernel, tb=tb, n_heads=n_heads, dec_seq=dec_seq),
        grid=(n_heads,),
        in_specs=[pl.BlockSpec(memory_space=pltpu.SMEM)],
        out_specs=[pl.BlockSpec((1, 2 * tb, tb), lambda h: (h, 0, 0)),
                   pl.BlockSpec((rows, page), lambda h: (h, 0)),
                   pl.BlockSpec((rows, LANES), lambda h: (h, 0))],
        out_shape=[jax.ShapeDtypeStruct((n_heads, 2 * tb, tb), F32),
                   jax.ShapeDtypeStruct((n_heads * rows, page), F32),
                   jax.ShapeDtypeStruct((n_heads * rows, LANES), F32)],
        compiler_params=_params(1),
        name="bias_tiles",
    )(rel_bias)


def _lambda(lq1, lk1, lq2, lk2, lambda_init):
    s1 = jnp.sum(lq1 * lk1, axis=-1, keepdims=True)
    s2 = jnp.sum(lq2 * lk2, axis=-1, keepdims=True)
    return jnp.exp(s1) - jnp.exp(s2) + lambda_init


def _qk(a, b):
    return lax.dot_general(a, b, (((1,), (1,)), ((), ())), preferred_element_type=F32)


def _all_sublanes(x, op):
    for dd in (1, 2, 4):
        x = op(x, pltpu.roll(x, dd, 0))
    return x


def _prompt_attn_kernel(q2_ref, k_ref, vt_ref, gb_ref, bias_ref, lq1_ref, lk1_ref, lq2_ref, lk2_ref,
                        sg_ref, o_ref, s0_ref, s1_ref, m0_ref, m1_ref, l0_ref, l1_ref, a0_ref, a1_ref,
                        pl0_ref, pl1_ref, pa0_ref, pa1_ref, *, tb, lambda_init):
    e, t_len = vt_ref.shape
    nq = t_len // tb
    s_refs = (s0_ref, s1_ref)
    m_refs = (m0_ref, m1_ref)
    l_refs = (l0_ref, l1_ref)
    acc_refs = (a0_ref, a1_ref)
    pl_refs = (pl0_ref, pl1_ref)
    pa_refs = (pa0_ref, pa1_ref)

    def block(i):
        return pl.ds(i * tb if isinstance(i, int) else pl.multiple_of(i * tb, tb), tb)

    def init():
        for mm in range(2):
            m_refs[mm][...] = jnp.full(m_refs[mm].shape, NEG, F32)
            l_refs[mm][...] = jnp.zeros(l_refs[mm].shape, F32)
            acc_refs[mm][...] = jnp.zeros(acc_refs[mm].shape, F32)

    def scores(mm, qi, kj):
        return _qk(k_ref[block(kj), :], q2_ref[mm, block(qi), :])

    def consume(mm, kj, bias):
        s = s_refs[mm][...]
        if bias is not None:
            s = s + bias
        s3 = s.reshape(tb // SUBLANES, SUBLANES, tb)
        m_prev = m_refs[mm][...]
        m_new = jnp.maximum(m_prev, _all_sublanes(jnp.max(s3, axis=0), jnp.maximum))
        alpha = jnp.exp2(m_prev - m_new)
        p3 = jnp.exp2(s3 - m_new[None])
        l_refs[mm][...] = alpha * l_refs[mm][...] + _all_sublanes(jnp.sum(p3, axis=0), jnp.add)
        pv = jnp.dot(vt_ref[:, block(kj)], p3.reshape(tb, tb).astype(BF16),
                     preferred_element_type=F32)
        acc = acc_refs[mm][...].reshape(e // SUBLANES, SUBLANES, tb) * alpha[None]
        acc_refs[mm][...] = acc.reshape(e, tb) + pv
        m_refs[mm][...] = m_new

    def step(qi, kj, bias, next_q, next_k):
        s1_ref[...] = scores(1, qi, kj)
        consume(0, kj, bias)
        s0_ref[...] = scores(0, next_q, next_k)
        consume(1, kj, bias)

    lam = _lambda(lq1_ref[...], lk1_ref[...], lq2_ref[...], lk2_ref[...], lambda_init)

    def park():
        for mm in range(2):
            pa_refs[mm][...] = acc_refs[mm][...]
            pl_refs[mm][...] = l_refs[mm][...]

    def finalize_parked(qi):
        def normalised(mm):
            acc = pa_refs[mm][...].reshape(e // SUBLANES, SUBLANES, tb)
            return (acc / pl_refs[mm][...][None]).reshape(e, tb)

        o = (normalised(0) - lam * normalised(1)).T
        y = _rms(o, sg_ref[...]) * (1.0 - lambda_init)
        o_ref[block(qi), :] = (gb_ref[block(qi), :] * y).astype(o_ref.dtype)

    bias_prev = lambda: bias_ref[0, :tb, :]
    bias_diag = lambda: bias_ref[0, tb:, :]

    init()
    s0_ref[...] = scores(0, 0, 0)
    step(0, 0, bias_diag(), 1 if nq > 1 else 0, 0)
    park()

    def tile(qi, carry):
        init()
        n_far = qi - 1
        one = n_far & 1
        two = n_far & 2

        @pl.when(one != 0)
        def _():
            step(qi, 0, None, qi, 1)

        @pl.when(two != 0)
        def _():
            step(qi, one, None, qi, one + 1)
            step(qi, one + 1, None, qi, one + 2)

        def far_quad(jj, c):
            j = one + two + 4 * jj
            for u in range(4):
                step(qi, j + u, None, qi, j + u + 1)
            return c

        lax.fori_loop(0, lax.shift_right_logical(n_far, 2), far_quad, 0)
        finalize_parked(qi - 1)
        step(qi, qi - 1, bias_prev(), qi, qi)
        step(qi, qi, bias_diag(), jnp.minimum(qi + 1, nq - 1), 0)
        park()
        return carry

    lax.fori_loop(1, nq, tile, 0)
    finalize_parked(nq - 1)


def _prompt_attn(q2, kb, vt, gb, bias_p, lams, subln_g, *, n_seq, n_heads, lambda_init):
    _, n, d = q2.shape
    e = d // n_heads
    t_len = n // n_seq
    tb = ATTN_BLOCK
    hd = lams[0].shape[-1]
    small = pl.BlockSpec((1, hd), lambda b, h: (0, 0))
    rows = pl.BlockSpec((t_len, e), lambda b, h: (b, h))
    stat = pltpu.VMEM((SUBLANES, tb), F32)
    return pl.pallas_call(
        functools.partial(_prompt_attn_kernel, tb=tb, lambda_init=lambda_init),
        grid=(n_seq, n_heads),
        in_specs=[pl.BlockSpec((2, t_len, e), lambda b, h: (0, b, h)),
                  rows,
                  pl.BlockSpec((e, t_len), lambda b, h: (h, b)),
                  rows,
                  pl.BlockSpec((1, 2 * tb, tb), lambda b, h: (h, 0, 0)),
                  small, small, small, small,
                  pl.BlockSpec((1, e), lambda b, h: (0, 0))],
        out_specs=rows,
        out_shape=jax.ShapeDtypeStruct((n, d), BF16),
        scratch_shapes=[pltpu.VMEM((tb, tb), F32), pltpu.VMEM((tb, tb), F32),
                        stat, stat, stat, stat,
                        pltpu.VMEM((e, tb), F32), pltpu.VMEM((e, tb), F32),
                        stat, stat,
                        pltpu.VMEM((e, tb), F32), pltpu.VMEM((e, tb), F32)],
        compiler_params=_params(2),
        name="prompt_attn",
    )(q2, kb, vt, gb, bias_p, *[x.reshape(1, hd) for x in lams], subln_g.reshape(1, e))


def _as_column(stat):
    r = stat.shape[1]
    row = lax.broadcasted_iota(jnp.int32, (r, r), 0)
    col = lax.broadcasted_iota(jnp.int32, (r, r), 1)
    full = jnp.concatenate([stat] * (r // SUBLANES), axis=0)
    return jnp.sum(jnp.where(row == col, full, 0.0), axis=1, keepdims=True)


def _sample_attn_kernel(pt_ref, q2_ref, *refs, n_heads, dec_seq, pps, lambda_init):
    kp = refs[:pps]
    vp = refs[pps:2 * pps]
    (kn_ref, vn_ref, gb_ref, blp_ref, bn_ref, lq1_ref, lk1_ref, lq2_ref, lk2_ref, sg_ref,
     o_ref, q_ref, qt_ref, blpt_ref, m_ref, l_ref, acc_ref) = refs[2 * pps:]
    del pt_ref
    c = pl.program_id(1)
    nc = pl.num_programs(1)
    e = kp[0].shape[1]
    d = n_heads * e
    page = kp[0].shape[0] // n_heads
    rows = 2 * dec_seq

    @pl.when(c == 0)
    def _():
        lane_head = lax.broadcasted_iota(jnp.int32, (dec_seq, d), 1) // e
        q_ref[...] = jnp.concatenate(
            [q2_ref[mm][:, h * e:(h + 1) * e] for h in range(n_heads) for mm in range(2)], axis=0)
        qbd = jnp.concatenate(
            [jnp.where(lane_head == h, q2_ref[mm], 0.0) for h in range(n_heads) for mm in range(2)],
            axis=0)
        qt_ref[...] = qbd.T
        blpt_ref[...] = blp_ref[...].T
        m_ref[...] = jnp.full(m_ref.shape, NEG, F32)
        l_ref[...] = jnp.zeros(l_ref.shape, F32)
        acc_ref[...] = jnp.zeros(acc_ref.shape, F32)

    def head_rows(ref, h):
        return ref[pl.ds(h, page, stride=n_heads), :]

    is_last = (c == nc - 1).astype(F32)
    parts = []
    for j in range(pps):
        keys = jnp.concatenate([head_rows(kp[j], h) for h in range(n_heads)], axis=1)
        parts.append(jnp.dot(keys, qt_ref[...], preferred_element_type=F32))
    parts[-1] = parts[-1] + blpt_ref[...] * is_last
    s3 = jnp.concatenate(parts, axis=0).reshape(pps * page // SUBLANES, SUBLANES, -1)
    m_prev = m_ref[...]
    m_new = jnp.maximum(m_prev, _all_sublanes(jnp.max(s3, axis=0), jnp.maximum))
    alpha = jnp.exp2(m_prev - m_new)
    p3 = jnp.exp2(s3 - m_new[None])
    l_ref[...] = alpha * l_ref[...] + _all_sublanes(jnp.sum(p3, axis=0), jnp.add)
    m_ref[...] = m_new
    p = p3.reshape(pps * page, -1).T
    out = []
    for h in range(n_heads):
        ph = p[h * rows:(h + 1) * rows]
        pv = jnp.dot(ph[:, :page], head_rows(vp[0], h), preferred_element_type=F32)
        for j in range(1, pps):
            pv = pv + jnp.dot(ph[:, j * page:(j + 1) * page], head_rows(vp[j], h),
                              preferred_element_type=F32)
        out.append(pv)
    acc_ref[...] = _as_column(alpha) * acc_ref[...] + jnp.concatenate(out, axis=0)

    @pl.when(c == nc - 1)
    def _():
        s = _qk(q_ref[...], kn_ref[...]) + bn_ref[...]
        m_old = _as_column(m_ref[...])
        m_fin = jnp.maximum(m_old, jnp.max(s, axis=-1, keepdims=True))
        scale = jnp.exp2(m_old - m_fin)
        pn = jnp.exp2(s - m_fin)
        l_fin = scale * _as_column(l_ref[...]) + jnp.sum(pn, axis=-1, keepdims=True)
        acc = scale * acc_ref[...] + jnp.dot(pn, vn_ref[...], preferred_element_type=F32)
        lam = _lambda(lq1_ref[...], lk1_ref[...], lq2_ref[...], lk2_ref[...], lambda_init)
        o_all = acc / l_fin
        outs = []
        for h in range(n_heads):
            r0 = h * rows
            o = o_all[r0:r0 + dec_seq] - lam * o_all[r0 + dec_seq:r0 + rows]
            outs.append(_rms(o, sg_ref[...]) * (1.0 - lambda_init))
        o_ref[...] = gb_ref[...] * jnp.concatenate(outs, axis=1)


def _sample_attn(page_table, q2f, cache_k, cache_v, kn_rows, vn_rows, gb, bias_lp, bias_n, lams,
                 subln_g, *, n_seq, n_heads, lambda_init):
    _, n, d = q2f.shape
    dec_seq = n // n_seq
    e = d // n_heads
    n_pool, page = cache_k.shape[:2]
    n_pages = page_table.shape[1]
    pps = PAGES_PER_STEP
    nc = n_pages // pps
    hd = lams[0].shape[-1]
    rows = n_heads * 2 * dec_seq
    width = page * n_heads

    def page_spec(j):
        return pl.BlockSpec((None, width, e), lambda b, c, pt: (pt[b, c * pps + j], 0, 0))

    const2 = lambda b, c, pt: (0, 0)
    small = pl.BlockSpec((1, hd), const2)
    new_rows = pl.BlockSpec((LANES, e), lambda b, c, pt: (b, 0))
    seq = pl.BlockSpec((dec_seq, d), lambda b, c, pt: (b, 0))
    grid_spec = pltpu.PrefetchScalarGridSpec(
        num_scalar_prefetch=1,
        grid=(n_seq, nc),
        in_specs=[pl.BlockSpec((2, dec_seq, d), lambda b, c, pt: (0, b, 0))]
                 + [page_spec(j) for j in range(pps)] + [page_spec(j) for j in range(pps)]
                 + [new_rows, new_rows, seq,
                    pl.BlockSpec((rows, page), const2), pl.BlockSpec((rows, LANES), const2),
                    small, small, small, small, pl.BlockSpec((1, e), const2)],
        out_specs=seq,
        scratch_shapes=[pltpu.VMEM((rows, e), F32), pltpu.VMEM((d, rows), F32),
                        pltpu.VMEM((page, rows), F32), pltpu.VMEM((SUBLANES, rows), F32),
                        pltpu.VMEM((SUBLANES, rows), F32), pltpu.VMEM((rows, e), F32)],
    )
    ck = cache_k.reshape(n_pool, width, e)
    cv = cache_v.reshape(n_pool, width, e)
    return pl.pallas_call(
        functools.partial(_sample_attn_kernel, n_heads=n_heads, dec_seq=dec_seq, pps=pps,
                          lambda_init=lambda_init),
        grid_spec=grid_spec,
        out_shape=jax.ShapeDtypeStruct((n, d), F32),
        compiler_params=_params(2),
        name="sample_attn",
    )(page_table, q2f, *([ck] * pps), *([cv] * pps), kn_rows, vn_rows, gb, bias_lp, bias_n,
      *[x.reshape(1, hd) for x in lams], subln_g.reshape(1, e))


def _new_token_rows(x2d, n_seq, n_heads):
    n, d = x2d.shape
    e = d // n_heads
    per_seq = (n // n_seq) * n_heads
    x = x2d.reshape(n_seq, per_seq, e)
    x = jnp.pad(x, ((0, 0), (0, LANES - per_seq), (0, 0)))
    return x.reshape(n_seq * LANES, e)


def _gelu_tanh(x):
    return 0.5 * x * (1.0 + jnp.tanh(math.sqrt(2.0 / math.pi) * (x + 0.044715 * (x * x * x))))


def _ffn_kernel(x_ref, za_ref, zb_ref, spad_ref, wout_ref, nffn_ref, wup_ref,
                cw_ref, cb_ref, wdown_ref, nfin_ref, y_ref, tail_ref, carry_ref, act_ref,
                *, group_mode, final_norm):
    ti = pl.program_id(1)
    tm, d = x_ref.shape
    g = tm // SUBLANES
    d_ff = wdown_ref.shape[0]

    merged = (za_ref[...].astype(F32) + zb_ref[...].astype(F32)).astype(BF16)
    x2 = x_ref[...] + jnp.dot(merged, wout_ref[...], preferred_element_type=F32)
    h2 = _rms(x2, nffn_ref[...]).astype(BF16)

    if not group_mode:
        @pl.when(ti == 0)
        def _():
            carry_ref[...] = spad_ref[0]

    def up_conv(cols):
        up3 = jnp.dot(h2, wup_ref[:, cols], preferred_element_type=F32).reshape(g, SUBLANES, -1)
        if group_mode:
            prev3 = spad_ref[:, :, cols]
            tail_ref[:, :, cols] = up3
        else:
            first = carry_ref[:, cols][None]
            prev3 = jnp.concatenate([first, up3[:g - 1]], axis=0) if g > 1 else first
            carry_ref[:, cols] = up3[g - 1]
            tail_ref[0, :, cols] = up3[g - 1]
        return _causal_conv(up3, prev3, cw_ref[:, cols], cb_ref[:, cols]).reshape(tm, -1)

    for c0 in range(0, d_ff, FFN_CHUNK):
        gate = up_conv(slice(c0, c0 + FFN_CHUNK))
        val = up_conv(slice(d_ff + c0, d_ff + c0 + FFN_CHUNK))
        act_ref[:, c0:c0 + FFN_CHUNK] = (_gelu_tanh(gate) * val).astype(BF16)
    x3 = x2 + jnp.dot(act_ref[...], wdown_ref[...], preferred_element_type=F32)
    y_ref[...] = _rms(x3, nfin_ref[...]) if final_norm else x3


def _ffn(x2d, za, zb, spad, wout_bf, nffn, wup_bf, cw, cb, wdown_bf, nfin,
         *, n_seq, group_mode, final_norm):
    n, d = x2d.shape
    f2 = wup_bf.shape[1]
    if group_mode:
        tm, grid = n, (1, 1)
        row = pl.BlockSpec((n, d), lambda b, t: (0, 0))
        seq = pl.BlockSpec((n_seq, SUBLANES, f2), lambda b, t: (0, 0, 0))
    else:
        t_len = n // n_seq
        tm = min(FFN_ROW_TILE, t_len)
        nt = t_len // tm
        grid = (n_seq, nt)
        row = pl.BlockSpec((tm, d), lambda b, t: (b * nt + t, 0))
        seq = pl.BlockSpec((1, SUBLANES, f2), lambda b, t: (b, 0, 0))
    return pl.pallas_call(
        functools.partial(_ffn_kernel, group_mode=group_mode, final_norm=final_norm),
        grid=grid,
        in_specs=[row, row, row, seq, _resident(wout_bf.shape), _resident((1, d)),
                  _resident(wup_bf.shape), _resident(cw.shape), _resident((1, f2)),
                  _resident(wdown_bf.shape), _resident((1, d))],
        out_specs=[row, seq],
        out_shape=[jax.ShapeDtypeStruct((n, d), F32), jax.ShapeDtypeStruct((n_seq, SUBLANES, f2), F32)],
        scratch_shapes=[pltpu.VMEM((SUBLANES, f2), F32), pltpu.VMEM((tm, f2 // 2), BF16)],
        compiler_params=_params(2),
        name="merge_ffn",
    )(x2d, za, zb, spad, wout_bf, nffn.reshape(1, d), wup_bf, cw, cb.reshape(1, f2),
      wdown_bf, nfin.reshape(1, d))


def _pad_state(buf):
    b, w1, c = buf.shape
    return jnp.concatenate([jnp.zeros((b, SUBLANES - w1, c), buf.dtype), buf], axis=1)


def kernel(x_prompt, x_sample, cache_k, cache_v, page_table, state_rglru_h, state_rglru_conv, state_ffn_conv, w_in, conv_w, conv_b, rg_w_a, rg_b_a, rg_w_x, rg_b_x, rg_a_param, lambda_q1, lambda_k1, lambda_q2, lambda_k2, subln_g, rel_bias, w_out, norm_attn, norm_ffn, w_up, ffn_conv_w, ffn_conv_b, w_down, norm_final):
    bsz, t_len, d = x_prompt.shape
    dbsz, dec_seq, _ = x_sample.shape
    depth = w_in.shape[0]
    n_heads = cache_k.shape[3]
    kv_dim = cache_k.shape[4]
    head_dim = kv_dim // 2
    page = cache_k.shape[2]
    past = page_table.shape[1] * page
    cw_w = conv_w.shape[1]
    fw_w = ffn_conv_w.shape[1]
    assert w_in.shape[2] == 6 * d and n_heads * kv_dim == d and rg_a_param.shape[1] == d
    assert dec_seq == SUBLANES and t_len % ATTN_BLOCK == 0 and page_table.shape[1] % PAGES_PER_STEP == 0
    assert ATTN_BLOCK >= MAX_DISTANCE and page >= MAX_DISTANCE and dec_seq * n_heads <= LANES
    assert kv_dim == LANES and n_heads == SUBLANES

    bias_p, bias_lp, bias_n = _bias_tiles(rel_bias, n_heads, ATTN_BLOCK, page, dec_seq)

    xp = x_prompt.reshape(bsz * t_len, d)
    xs = x_sample.reshape(dbsz * dec_seq, d)
    outs = [[] for _ in range(10)]
    for l in range(depth):
        lambda_init = 0.8 - 0.6 * math.exp(-0.3 * l)
        last = l == depth - 1
        w_in_bf = w_in[l].astype(BF16)
        wa_bf = rg_w_a[l].astype(BF16)
        wx_bf = rg_w_x[l].astype(BF16)
        wout_bf = w_out[l].astype(BF16)
        wup_bf = w_up[l].astype(BF16)
        wdown_bf = w_down[l].astype(BF16)
        lams = (lambda_q1[l], lambda_k1[l], lambda_q2[l], lambda_k2[l])

        def project(x2d, spad, h0, n_seq, group_mode, pos_base):
            h0b = jnp.broadcast_to(h0[:, None, :], (n_seq, SUBLANES, d))
            return _inproj(x2d, norm_attn[l], w_in_bf, spad, h0b, conv_w[l], conv_b[l], wa_bf,
                           rg_b_a[l], wx_bf, rg_b_x[l], rg_a_param[l], head_dim=head_dim,
                           n_seq=n_seq, group_mode=group_mode, pos_base=pos_base)

        def channel_mix(x2d, za, zb, spad, n_seq, group_mode):
            return _ffn(x2d, za, zb, spad, wout_bf, norm_ffn[l], wup_bf, ffn_conv_w[l],
                        ffn_conv_b[l], wdown_bf, norm_final, n_seq=n_seq, group_mode=group_mode,
                        final_norm=last)

        q2, k, kb, v, vt, gb, za, hp, xtail = project(
            xp, jnp.zeros((bsz, SUBLANES, d), F32), jnp.zeros((bsz, d), F32), bsz, False, 0)
        zb = _prompt_attn(q2, kb, vt, gb, bias_p, lams, subln_g[l], n_seq=bsz, n_heads=n_heads,
                          lambda_init=lambda_init)
        xp, fp = channel_mix(xp, za, zb, jnp.zeros((bsz, SUBLANES, ffn_conv_w.shape[2]), F32),
                             bsz, False)
        outs[0].append(k.reshape(bsz, t_len, n_heads, kv_dim))
        outs[1].append(v.reshape(bsz, t_len, n_heads, kv_dim))
        outs[4].append(hp[:, SUBLANES - 1])
        outs[6].append(xtail[:, SUBLANES - (cw_w - 1):])
        outs[8].append(fp[:, SUBLANES - (fw_w - 1):])

        q2, k, _, v, _, gb, za, hs, xtail = project(
            xs, _pad_state(state_rglru_conv[l]), state_rglru_h[l], dbsz, True, past)
        zb = _sample_attn(page_table, q2.astype(F32), cache_k[l], cache_v[l],
                          _new_token_rows(k, dbsz, n_heads), _new_token_rows(v, dbsz, n_heads),
                          gb, bias_lp, bias_n, lams, subln_g[l], n_seq=dbsz, n_heads=n_heads,
                          lambda_init=lambda_init)
        xs, fs = channel_mix(xs, za, zb, _pad_state(state_ffn_conv[l]), dbsz, True)
        outs[2].append(k.reshape(dbsz, dec_seq, n_heads, kv_dim))
        outs[3].append(v.reshape(dbsz, dec_seq, n_heads, kv_dim))
        outs[5].append(hs[:, SUBLANES - 1])
        outs[7].append(xtail[:, SUBLANES - (cw_w - 1):])
        outs[9].append(fs[:, SUBLANES - (fw_w - 1):])

    return (xp.reshape(bsz, t_len, d), xs.reshape(dbsz, dec_seq, d),
            jnp.stack(outs[0]), jnp.stack(outs[1]), jnp.stack(outs[2]), jnp.stack(outs[3]),
            jnp.stack(outs[4]), jnp.stack(outs[5]), jnp.stack(outs[6]), jnp.stack(outs[7]),
            jnp.stack(outs[8]), jnp.stack(outs[9]))
```

```python
import functools
import math

import jax
import jax.numpy as jnp
from jax import lax
from jax.experimental import pallas as pl
from jax.experimental.pallas import tpu as pltpu

F32 = jnp.float32
BF16 = jnp.bfloat16

EPS = 1e-6
RG_C = 8.0
N_BUCKETS = 32
MAX_DISTANCE = 128
NEG = -1e30
LOG2E = math.log2(math.e)

SUBLANES = 8
LANES = 128
ROW_TILE = 512
FFN_ROW_TILE = 512
ATTN_BLOCK = 512
FFN_CHUNK = 512
PAGES_PER_STEP = 16
VMEM_LIMIT = 52 * 1024 * 1024


def _params(n_axes, vmem=VMEM_LIMIT):
    return pltpu.CompilerParams(
        dimension_semantics=("arbitrary",) * n_axes, vmem_limit_bytes=vmem)


def _resident(shape):
    nd = len(shape)
    return pl.BlockSpec(shape, lambda *_: (0,) * nd, pipeline_mode=pl.Buffered(1))


def _sigmoid(x):
    return 0.5 * (1.0 + jnp.tanh(0.5 * x))


def _rms(x, g):
    return x * lax.rsqrt(jnp.mean(x * x, axis=-1, keepdims=True) + EPS) * g


def _inproj_kernel(x_ref, g_ref, w_ref, spad_ref, h0_ref, cw_ref, cb_ref, wa_ref, ba_ref, wx_ref,
                   bx_ref, ap_ref, q2_ref, k_ref, kb_ref, v_ref, vt_ref, gb_ref, za_ref, hlast_ref,
                   xtail_ref, carry_x, carry_h, a_scr, b_scr, ga_scr,
                   *, head_dim, group_mode, pos_base, n_blocks):
    ti = pl.program_id(1)
    tm, d = x_ref.shape
    h = _rms(x_ref[...], g_ref[...]).astype(BF16)

    if not group_mode:
        @pl.when(ti == 0)
        def _():
            carry_x[...] = spad_ref[0]
            carry_h[...] = h0_ref[0]

    pieces = 4
    pc = d // pieces
    assert n_blocks % pieces == 0 and tm % (pieces * SUBLANES) == 0 and pc % LANES == 0

    def cols_of(j, c):
        return slice(j * d + c * pc, j * d + (c + 1) * pc)

    def part(j, c):
        return jnp.dot(h, w_ref[:, cols_of(j, c)], preferred_element_type=F32)

    def emit_q(c):
        q = part(1, c) * (head_dim ** -0.5 * LOG2E)
        lane = lax.broadcasted_iota(jnp.int32, q.shape, 1)
        first = (lane % (2 * head_dim)) < head_dim
        q2_ref[0, :, cols_of(0, c)] = jnp.where(first, q, 0.0).astype(BF16)
        q2_ref[1, :, cols_of(0, c)] = jnp.where(first, 0.0, q).astype(BF16)

    def emit_k(c):
        k = part(2, c)
        k_ref[:, cols_of(0, c)] = k
        kb_ref[:, cols_of(0, c)] = k.astype(BF16)

    def emit_v(c):
        v = part(3, c)
        v_ref[:, cols_of(0, c)] = v
        vt_ref[cols_of(0, c), :] = v.T.astype(BF16)

    def emit_gb(c):
        gb_ref[:, cols_of(0, c)] = _sigmoid(part(5, c))

    emitters = (emit_q, emit_k, emit_v, emit_gb)
    bpp = n_blocks // pieces
    rq = tm // pieces
    for p, emit in enumerate(emitters):
        xr = part(0, p)
        ga_scr[:, cols_of(0, p)] = _sigmoid(part(4, p))
        for r in range(pieces):
            groups = slice(r * rq // SUBLANES, (r + 1) * rq // SUBLANES)
            a3, b3 = _rglru_group_scan(
                xr[r * rq:(r + 1) * rq], ti, cols_of(0, p), range(p * bpp, (p + 1) * bpp), groups,
                spad_ref, cw_ref, cb_ref, wa_ref, ba_ref, wx_ref, bx_ref, ap_ref, xtail_ref, carry_x,
                group_mode=group_mode, pos_base=pos_base, first_rows=r == 0)
            a_scr[groups, :, cols_of(0, p)] = a3
            b_scr[groups, :, cols_of(0, p)] = b3
            emit(r)

    h3 = _rglru_carry(h0_ref, hlast_ref, carry_h, a_scr, b_scr, group_mode=group_mode)
    za_ref[...] = (ga_scr[...] * h3.reshape(tm, d)).astype(za_ref.dtype)


def _inproj(x2d, g, w_bf, spad, h0b, cw, cb, wa_bf, ba, wx_bf, bx, ap,
            *, head_dim, n_seq, group_mode, pos_base):
    n, d = x2d.shape
    n_blocks = wa_bf.shape[0]
    if group_mode:
        tm, nt, grid = n, 1, (1, 1)
        seq = pl.BlockSpec((n_seq, SUBLANES, d), lambda b, t: (0, 0, 0))
    else:
        t_len = n // n_seq
        tm = min(ROW_TILE, t_len)
        nt = t_len // tm
        grid = (n_seq, nt)
        seq = pl.BlockSpec((1, SUBLANES, d), lambda b, t: (b, 0, 0))
    blk = pl.BlockSpec((tm, d), lambda b, t: (b * nt + t, 0))
    g8 = tm // SUBLANES
    vec = lambda a: a.reshape(1, d)
    f32 = jax.ShapeDtypeStruct((n, d), F32)
    b16 = jax.ShapeDtypeStruct((n, d), BF16)
    state = jax.ShapeDtypeStruct((n_seq, SUBLANES, d), F32)
    return pl.pallas_call(
        functools.partial(_inproj_kernel, head_dim=head_dim, group_mode=group_mode, pos_base=pos_base,
                          n_blocks=n_blocks),
        grid=grid,
        in_specs=[blk, _resident((1, d)), _resident(w_bf.shape), seq, seq, _resident(cw.shape),
                  _resident((1, d)), _resident(wa_bf.shape), _resident((1, d)), _resident(wx_bf.shape),
                  _resident((1, d)), _resident((1, d))],
        out_specs=[pl.BlockSpec((2, tm, d), lambda b, t: (0, b * nt + t, 0)), blk, blk, blk,
                   pl.BlockSpec((d, tm), lambda b, t: (0, b * nt + t)), blk, blk, seq, seq],
        out_shape=[jax.ShapeDtypeStruct((2, n, d), BF16), f32, b16, f32,
                   jax.ShapeDtypeStruct((d, n), BF16), f32, b16, state, state],
        scratch_shapes=[pltpu.VMEM((SUBLANES, d), F32), pltpu.VMEM((SUBLANES, d), F32),
                        pltpu.VMEM((g8, SUBLANES, d), F32), pltpu.VMEM((g8, SUBLANES, d), F32),
                        pltpu.VMEM((tm, d), F32)],
        compiler_params=_params(2),
        name="inproj",
    )(x2d, g.reshape(1, d), w_bf, spad, h0b, cw, vec(cb), wa_bf, vec(ba), wx_bf, vec(bx), vec(ap))


def _shifted(x3, prev3, d, t_idx):
    return jnp.where(t_idx >= d, pltpu.roll(x3, d, 1), pltpu.roll(prev3, d, 1))


def _causal_conv(x3, prev3, w, b):
    width = w.shape[0]
    t_idx = lax.broadcasted_iota(jnp.int32, x3.shape, 1)
    y = b + w[width - 1:width] * x3
    for dd in range(1, width):
        y = y + w[width - 1 - dd:width - dd] * _shifted(x3, prev3, dd, t_idx)
    return y


def _rglru_group_scan(xr, ti, cols, blocks, groups, spad_ref, cw_ref, cb_ref, wa_ref, ba_ref, wx_ref,
                      bx_ref, ap_ref, xtail_ref, carry_x, *, group_mode, pos_base, first_rows):
    tt, c = xr.shape
    g = tt // SUBLANES
    blk = c // len(blocks)
    x3 = xr.reshape(g, SUBLANES, c)

    if group_mode:
        prev3 = spad_ref[groups, :, cols]
        xtail_ref[groups, :, cols] = x3
    else:
        first = carry_x[:, cols][None]
        prev3 = jnp.concatenate([first, x3[:g - 1]], axis=0) if g > 1 else first

    xc3 = _causal_conv(x3, prev3, cw_ref[:, cols], cb_ref[:, cols])
    if not group_mode:
        carry_x[:, cols] = x3[g - 1]
        xtail_ref[0, :, cols] = x3[g - 1]

    xc = xc3.reshape(tt, c)
    xcb = xc.astype(BF16)

    def gate(w_ref, b_ref):
        parts = [jnp.dot(xcb[:, i * blk:(i + 1) * blk], w_ref[n], preferred_element_type=F32)
                 for i, n in enumerate(blocks)]
        return jnp.concatenate(parts, axis=1) + b_ref[:, cols]

    r = _sigmoid(gate(wa_ref, ba_ref))
    i = _sigmoid(gate(wx_ref, bx_ref))
    z = -ap_ref[:, cols]
    softplus = jnp.maximum(z, 0.0) + jnp.log(1.0 + jnp.exp(-jnp.abs(z)))
    log_a = -RG_C * r * softplus
    a = jnp.exp(log_a)
    th = jnp.tanh(log_a)
    mult = jnp.sqrt(-2.0 * th / (1.0 - th))
    ix = i * xc
    a3 = a.reshape(g, SUBLANES, c)
    b3 = (mult * ix).reshape(g, SUBLANES, c)
    t_idx = lax.broadcasted_iota(jnp.int32, a3.shape, 1)
    if pos_base == 0 and (group_mode or first_rows):
        ix3 = ix.reshape(g, SUBLANES, c)
        if group_mode:
            reset = t_idx == 0
            a3 = jnp.where(reset, 0.0, a3)
            b3 = jnp.where(reset, ix3, b3)
        else:
            reset = (lax.broadcasted_iota(jnp.int32, (SUBLANES, c), 0) + ti) == 0
            rest = lambda x3: [x3[1:]] if g > 1 else []
            a3 = jnp.concatenate([jnp.where(reset, 0.0, a3[0])[None]] + rest(a3), axis=0)
            b3 = jnp.concatenate([jnp.where(reset, ix3[0], b3[0])[None]] + rest(b3), axis=0)

    for dd in (1, 2, 4):
        valid = t_idx >= dd
        b3 = jnp.where(valid, a3 * pltpu.roll(b3, dd, 1) + b3, b3)
        a3 = jnp.where(valid, a3 * pltpu.roll(a3, dd, 1), a3)
    return a3, b3


def _rglru_carry(h0_ref, hlast_ref, carry_h, a_scr, b_scr, *, group_mode):
    if group_mode:
        h3 = a_scr[...] * h0_ref[...] + b_scr[...]
        hlast_ref[...] = h3
        return h3
    g, _, c = a_scr.shape

    def body(gi, carry):
        hg = a_scr[gi] * carry + b_scr[gi]
        b_scr[gi] = hg
        return jnp.broadcast_to(hg[SUBLANES - 1:SUBLANES, :], (SUBLANES, c))

    last = lax.fori_loop(0, g, body, carry_h[...])
    carry_h[...] = last
    hlast_ref[0] = last
    return b_scr[...]


def _bias_of(dist, valid, rb_ref, h):
    n = jnp.maximum(dist, 0)
    max_exact = N_BUCKETS // 2
    nf = jnp.maximum(n, 1).astype(F32)
    large = max_exact + (jnp.log(nf / max_exact) / math.log(MAX_DISTANCE / max_exact)
                         * (N_BUCKETS - max_exact)).astype(jnp.int32)
    large = jnp.minimum(large, N_BUCKETS - 1)
    bucket = jnp.where(n < max_exact, n, large)
    base = rb_ref[N_BUCKETS - 1, h]
    out = jnp.zeros(dist.shape, F32)
    for b in range(N_BUCKETS - 1):
        out = jnp.where(bucket == b, (rb_ref[b, h] - base) * LOG2E, out)
    return jnp.where(valid, out, NEG)


def _bias_kernel(rb_ref, bp_ref, blp_ref, bn_ref, *, tb, n_heads, dec_seq):
    h = pl.program_id(0)
    j = lax.broadcasted_iota(jnp.int32, (LANES, LANES), 0)
    i = lax.broadcasted_iota(jnp.int32, (LANES, LANES), 1)
    for jb in range(2 * tb // LANES):
        for ib in range(tb // LANES):
            base = (ib - jb) * LANES + tb
            rows, cols = slice(jb * LANES, (jb + 1) * LANES), slice(ib * LANES, (ib + 1) * LANES)
            if base - (LANES - 1) >= MAX_DISTANCE:
                bp_ref[0, rows, cols] = jnp.zeros((LANES, LANES), F32)
            elif base + (LANES - 1) < 0:
                bp_ref[0, rows, cols] = jnp.full((LANES, LANES), NEG, F32)
            else:
                dist = base + i - j
                bp_ref[0, rows, cols] = _bias_of(dist, dist >= 0, rb_ref, h)
    rows, page = blp_ref.shape
    t = lax.broadcasted_iota(jnp.int32, (rows, page), 0) % dec_seq
    key = lax.broadcasted_iota(jnp.int32, (rows, page), 1)
    dist = page + t - key
    blp_ref[...] = _bias_of(dist, dist >= 0, rb_ref, h)
    t = lax.broadcasted_iota(jnp.int32, (rows, LANES), 0) % dec_seq
    lane = lax.broadcasted_iota(jnp.int32, (rows, LANES), 1)
    dist = t - lane // n_heads
    bn_ref[...] = _bias_of(dist, jnp.logical_and((lane % n_heads) == h, dist >= 0), rb_ref, h)


def _bias_tiles(rel_bias, n_heads, tb, page, dec_seq):
    rows = 2 * dec_seq
    return pl.pallas_call(
        functools.partial(_bias_kernel, tb=tb, n_heads=n_heads, dec_seq=dec_seq),
        grid=(n_heads,),
        in_specs=[pl.BlockSpec(memory_space=pltpu.SMEM)],
        out_specs=[pl.BlockSpec((1, 2 * tb, tb), lambda h: (h, 0, 0)),
                   pl.BlockSpec((rows, page), lambda h: (h, 0)),
                   pl.BlockSpec((rows, LANES), lambda h: (h, 0))],
        out_shape=[jax.ShapeDtypeStruct((n_heads, 2 * tb, tb), F32),
                   jax.ShapeDtypeStruct((n_heads * rows, page), F32),
                   jax.ShapeDtypeStruct((n_heads * rows, LANES), F32)],
        compiler_params=_params(1),
        name="bias_tiles",
    )(rel_bias)


def _lambda(lq1, lk1, lq2, lk2, lambda_init):
    s1 = jnp.sum(lq1 * lk1, axis=-1, keepdims=True)
    s2 = jnp.sum(lq2 * lk2, axis=-1, keepdims=True)
    return jnp.exp(s1) - jnp.exp(s2) + lambda_init


def _qk(a, b):
    return lax.dot_general(a, b, (((1,), (1,)), ((), ())), preferred_element_type=F32)


def _all_sublanes(x, op):
    for dd in (1, 2, 4):
        x = op(x, pltpu.roll(x, dd, 0))
    return x


def _prompt_attn_kernel(q2_ref, k_ref, vt_ref, gb_ref, bias_ref, lq1_ref, lk1_ref, lq2_ref, lk2_ref,
                        sg_ref, o_ref, s0_ref, s1_ref, m0_ref, m1_ref, l0_ref, l1_ref, a0_ref, a1_ref,
                        pl0_ref, pl1_ref, pa0_ref, pa1_ref, *, tb, lambda_init):
    e, t_len = vt_ref.shape
    nq = t_len // tb
    s_refs = (s0_ref, s1_ref)
    m_refs = (m0_ref, m1_ref)
    l_refs = (l0_ref, l1_ref)
    acc_refs = (a0_ref, a1_ref)
    pl_refs = (pl0_ref, pl1_ref)
    pa_refs = (pa0_ref, pa1_ref)

    def block(i):
        return pl.ds(i * tb if isinstance(i, int) else pl.multiple_of(i * tb, tb), tb)

    def init():
        for mm in range(2):
            m_refs[mm][...] = jnp.full(m_refs[mm].shape, NEG, F32)
            l_refs[mm][...] = jnp.zeros(l_refs[mm].shape, F32)
            acc_refs[mm][...] = jnp.zeros(acc_refs[mm].shape, F32)

    def scores(mm, qi, kj):
        return _qk(k_ref[block(kj), :], q2_ref[mm, block(qi), :])

    def consume(mm, kj, bias):
        s = s_refs[mm][...]
        if bias is not None:
            s = s + bias
        s3 = s.reshape(tb // SUBLANES, SUBLANES, tb)
        m_prev = m_refs[mm][...]
        m_new = jnp.maximum(m_prev, _all_sublanes(jnp.max(s3, axis=0), jnp.maximum))
        alpha = jnp.exp2(m_prev - m_new)
        p3 = jnp.exp2(s3 - m_new[None])
        l_refs[mm][...] = alpha * l_refs[mm][...] + _all_sublanes(jnp.sum(p3, axis=0), jnp.add)
        pv = jnp.dot(vt_ref[:, block(kj)], p3.reshape(tb, tb).astype(BF16),
                     preferred_element_type=F32)
        acc = acc_refs[mm][...].reshape(e // SUBLANES, SUBLANES, tb) * alpha[None]
        acc_refs[mm][...] = acc.reshape(e, tb) + pv
        m_refs[mm][...] = m_new

    half = tb // 2

    def half_block(i, hi):
        start = i * tb + hi * half
        return pl.ds(start if isinstance(i, int) else pl.multiple_of(start, half), half)

    def diag_scores(mm, qi):
        s_refs[mm][:half, :] = _qk(k_ref[half_block(qi, 0), :], q2_ref[mm, block(qi), :])
        s_refs[mm][half:, half:] = _qk(k_ref[half_block(qi, 1), :], q2_ref[mm, half_block(qi, 1), :])

    def diag_consume(mm, qi, bias):
        sa3 = (s_refs[mm][:half, :] + bias[:half, :]).reshape(half // SUBLANES, SUBLANES, tb)
        sb3 = (s_refs[mm][half:, half:] + bias[half:, half:]).reshape(half // SUBLANES, SUBLANES, half)

        def late(x_all, x_late, op):
            return jnp.concatenate([x_all[:, :half], op(x_all[:, half:], x_late)], axis=1)

        m_prev = m_refs[mm][...]
        m_new = late(jnp.maximum(m_prev, _all_sublanes(jnp.max(sa3, axis=0), jnp.maximum)),
                     _all_sublanes(jnp.max(sb3, axis=0), jnp.maximum), jnp.maximum)
        alpha = jnp.exp2(m_prev - m_new)
        pa3 = jnp.exp2(sa3 - m_new[None])
        pb3 = jnp.exp2(sb3 - m_new[None, :, half:])
        sums = late(_all_sublanes(jnp.sum(pa3, axis=0), jnp.add),
                    _all_sublanes(jnp.sum(pb3, axis=0), jnp.add), jnp.add)
        l_refs[mm][...] = alpha * l_refs[mm][...] + sums
        pv = late(jnp.dot(vt_ref[:, half_block(qi, 0)], pa3.reshape(half, tb).astype(BF16),
                          preferred_element_type=F32),
                  jnp.dot(vt_ref[:, half_block(qi, 1)], pb3.reshape(half, half).astype(BF16),
                          preferred_element_type=F32), jnp.add)
        acc = acc_refs[mm][...].reshape(e // SUBLANES, SUBLANES, tb) * alpha[None]
        acc_refs[mm][...] = acc.reshape(e, tb) + pv
        m_refs[mm][...] = m_new

    def step(qi, kj, bias, next_q, next_k, diag=False, next_diag=False):
        if diag:
            diag_scores(1, qi)
            diag_consume(0, qi, bias)
        else:
            s1_ref[...] = scores(1, qi, kj)
            consume(0, kj, bias)
        if next_diag:
            diag_scores(0, next_q)
        else:
            s0_ref[...] = scores(0, next_q, next_k)
        if diag:
            diag_consume(1, qi, bias)
        else:
            consume(1, kj, bias)

    lam = _lambda(lq1_ref[...], lk1_ref[...], lq2_ref[...], lk2_ref[...], lambda_init)

    def park():
        for mm in range(2):
            pa_refs[mm][...] = acc_refs[mm][...]
            pl_refs[mm][...] = l_refs[mm][...]

    def finalize_parked(qi):
        def normalised(mm):
            acc = pa_refs[mm][...].reshape(e // SUBLANES, SUBLANES, tb)
            return (acc / pl_refs[mm][...][None]).reshape(e, tb)

        o = (normalised(0) - lam * normalised(1)).T
        y = _rms(o, sg_ref[...]) * (1.0 - lambda_init)
        o_ref[block(qi), :] = (gb_ref[block(qi), :] * y).astype(o_ref.dtype)

    bias_prev = lambda: bias_ref[0, :tb, :]
    bias_diag = lambda: bias_ref[0, tb:, :]

    init()
    diag_scores(0, 0)
    step(0, 0, bias_diag(), 1 if nq > 1 else 0, 0, diag=True)
    park()

    def tile(qi, carry):
        init()
        n_far = qi - 1
        one = n_far & 1
        two = n_far & 2

        @pl.when(one != 0)
        def _():
            step(qi, 0, None, qi, 1)

        @pl.when(two != 0)
        def _():
            step(qi, one, None, qi, one + 1)
            step(qi, one + 1, None, qi, one + 2)

        def far_quad(jj, c):
            j = one + two + 4 * jj
            for u in range(4):
                step(qi, j + u, None, qi, j + u + 1)
            return c

        lax.fori_loop(0, lax.shift_right_logical(n_far, 2), far_quad, 0)
        finalize_parked(qi - 1)
        step(qi, qi - 1, bias_prev(), qi, qi, next_diag=True)
        step(qi, qi, bias_diag(), jnp.minimum(qi + 1, nq - 1), 0, diag=True)
        park()
        return carry

    lax.fori_loop(1, nq, tile, 0)
    finalize_parked(nq - 1)


def _prompt_attn(q2, kb, vt, gb, bias_p, lams, subln_g, *, n_seq, n_heads, lambda_init):
    _, n, d = q2.shape
    e = d // n_heads
    t_len = n // n_seq
    tb = ATTN_BLOCK
    hd = lams[0].shape[-1]
    small = pl.BlockSpec((1, hd), lambda b, h: (0, 0))
    rows = pl.BlockSpec((t_len, e), lambda b, h: (b, h))
    stat = pltpu.VMEM((SUBLANES, tb), F32)
    return pl.pallas_call(
        functools.partial(_prompt_attn_kernel, tb=tb, lambda_init=lambda_init),
        grid=(n_seq, n_heads),
        in_specs=[pl.BlockSpec((2, t_len, e), lambda b, h: (0, b, h)),
                  rows,
                  pl.BlockSpec((e, t_len), lambda b, h: (h, b)),
                  rows,
                  pl.BlockSpec((1, 2 * tb, tb), lambda b, h: (h, 0, 0)),
                  small, small, small, small,
                  pl.BlockSpec((1, e), lambda b, h: (0, 0))],
        out_specs=rows,
        out_shape=jax.ShapeDtypeStruct((n, d), BF16),
        scratch_shapes=[pltpu.VMEM((tb, tb), F32), pltpu.VMEM((tb, tb), F32),
                        stat, stat, stat, stat,
                        pltpu.VMEM((e, tb), F32), pltpu.VMEM((e, tb), F32),
                        stat, stat,
                        pltpu.VMEM((e, tb), F32), pltpu.VMEM((e, tb), F32)],
        compiler_params=_params(2),
        name="prompt_attn",
    )(q2, kb, vt, gb, bias_p, *[x.reshape(1, hd) for x in lams], subln_g.reshape(1, e))


def _as_column(stat):
    r = stat.shape[1]
    row = lax.broadcasted_iota(jnp.int32, (r, r), 0)
    col = lax.broadcasted_iota(jnp.int32, (r, r), 1)
    full = jnp.concatenate([stat] * (r // SUBLANES), axis=0)
    return jnp.sum(jnp.where(row == col, full, 0.0), axis=1, keepdims=True)


def _sample_attn_kernel(pt_ref, q2_ref, *refs, n_heads, dec_seq, pps, lambda_init):
    kp = refs[:pps]
    vp = refs[pps:2 * pps]
    (kn_ref, vn_ref, gb_ref, blp_ref, bn_ref, lq1_ref, lk1_ref, lq2_ref, lk2_ref, sg_ref,
     o_ref, q_ref, qt_ref, blpt_ref, m_ref, l_ref, acc_ref) = refs[2 * pps:]
    del pt_ref
    c = pl.program_id(1)
    nc = pl.num_programs(1)
    e = kp[0].shape[1]
    d = n_heads * e
    page = kp[0].shape[0] // n_heads
    rows = 2 * dec_seq

    @pl.when(c == 0)
    def _():
        lane_head = lax.broadcasted_iota(jnp.int32, (dec_seq, d), 1) // e
        q_ref[...] = jnp.concatenate(
            [q2_ref[mm][:, h * e:(h + 1) * e] for h in range(n_heads) for mm in range(2)], axis=0)
        qbd = jnp.concatenate(
            [jnp.where(lane_head == h, q2_ref[mm], 0.0) for h in range(n_heads) for mm in range(2)],
            axis=0)
        qt_ref[...] = qbd.T
        blpt_ref[...] = blp_ref[...].T
        m_ref[...] = jnp.full(m_ref.shape, NEG, F32)
        l_ref[...] = jnp.zeros(l_ref.shape, F32)
        acc_ref[...] = jnp.zeros(acc_ref.shape, F32)

    def head_rows(ref, h):
        return ref[pl.ds(h, page, stride=n_heads), :]

    is_last = (c == nc - 1).astype(F32)
    parts = []
    for j in range(pps):
        keys = jnp.concatenate([head_rows(kp[j], h) for h in range(n_heads)], axis=1)
        parts.append(jnp.dot(keys, qt_ref[...], preferred_element_type=F32))
    parts[-1] = parts[-1] + blpt_ref[...] * is_last
    s3 = jnp.concatenate(parts, axis=0).reshape(pps * page // SUBLANES, SUBLANES, -1)
    m_prev = m_ref[...]
    m_new = jnp.maximum(m_prev, _all_sublanes(jnp.max(s3, axis=0), jnp.maximum))
    alpha = jnp.exp2(m_prev - m_new)
    p3 = jnp.exp2(s3 - m_new[None])
    l_ref[...] = alpha * l_ref[...] + _all_sublanes(jnp.sum(p3, axis=0), jnp.add)
    m_ref[...] = m_new
    p = p3.reshape(pps * page, -1).T
    out = []
    for h in range(n_heads):
        ph = p[h * rows:(h + 1) * rows]
        pv = jnp.dot(ph[:, :page], head_rows(vp[0], h), preferred_element_type=F32)
        for j in range(1, pps):
            pv = pv + jnp.dot(ph[:, j * page:(j + 1) * page], head_rows(vp[j], h),
                              preferred_element_type=F32)
        out.append(pv)
    acc_ref[...] = _as_column(alpha) * acc_ref[...] + jnp.concatenate(out, axis=0)

    @pl.when(c == nc - 1)
    def _():
        s = _qk(q_ref[...], kn_ref[...]) + bn_ref[...]
        m_old = _as_column(m_ref[...])
        m_fin = jnp.maximum(m_old, jnp.max(s, axis=-1, keepdims=True))
        scale = jnp.exp2(m_old - m_fin)
        pn = jnp.exp2(s - m_fin)
        l_fin = scale * _as_column(l_ref[...]) + jnp.sum(pn, axis=-1, keepdims=True)
        acc = scale * acc_ref[...] + jnp.dot(pn, vn_ref[...], preferred_element_type=F32)
        lam = _lambda(lq1_ref[...], lk1_ref[...], lq2_ref[...], lk2_ref[...], lambda_init)
        o_all = acc / l_fin
        outs = []
        for h in range(n_heads):
            r0 = h * rows
            o = o_all[r0:r0 + dec_seq] - lam * o_all[r0 + dec_seq:r0 + rows]
            outs.append(_rms(o, sg_ref[...]) * (1.0 - lambda_init))
        o_ref[...] = gb_ref[...] * jnp.concatenate(outs, axis=1)


def _sample_attn(page_table, q2f, cache_k, cache_v, kn_rows, vn_rows, gb, bias_lp, bias_n, lams,
                 subln_g, *, n_seq, n_heads, lambda_init):
    _, n, d = q2f.shape
    dec_seq = n // n_seq
    e = d // n_heads
    n_pool, page = cache_k.shape[:2]
    n_pages = page_table.shape[1]
    pps = PAGES_PER_STEP
    nc = n_pages // pps
    hd = lams[0].shape[-1]
    rows = n_heads * 2 * dec_seq
    width = page * n_heads

    def page_spec(j):
        return pl.BlockSpec((None, width, e), lambda b, c, pt: (pt[b, c * pps + j], 0, 0))

    const2 = lambda b, c, pt: (0, 0)
    small = pl.BlockSpec((1, hd), const2)
    new_rows = pl.BlockSpec((LANES, e), lambda b, c, pt: (b, 0))
    seq = pl.BlockSpec((dec_seq, d), lambda b, c, pt: (b, 0))
    grid_spec = pltpu.PrefetchScalarGridSpec(
        num_scalar_prefetch=1,
        grid=(n_seq, nc),
        in_specs=[pl.BlockSpec((2, dec_seq, d), lambda b, c, pt: (0, b, 0))]
                 + [page_spec(j) for j in range(pps)] + [page_spec(j) for j in range(pps)]
                 + [new_rows, new_rows, seq,
                    pl.BlockSpec((rows, page), const2), pl.BlockSpec((rows, LANES), const2),
                    small, small, small, small, pl.BlockSpec((1, e), const2)],
        out_specs=seq,
        scratch_shapes=[pltpu.VMEM((rows, e), F32), pltpu.VMEM((d, rows), F32),
                        pltpu.VMEM((page, rows), F32), pltpu.VMEM((SUBLANES, rows), F32),
                        pltpu.VMEM((SUBLANES, rows), F32), pltpu.VMEM((rows, e), F32)],
    )
    ck = cache_k.reshape(n_pool, width, e)
    cv = cache_v.reshape(n_pool, width, e)
    return pl.pallas_call(
        functools.partial(_sample_attn_kernel, n_heads=n_heads, dec_seq=dec_seq, pps=pps,
                          lambda_init=lambda_init),
        grid_spec=grid_spec,
        out_shape=jax.ShapeDtypeStruct((n, d), F32),
        compiler_params=_params(2),
        name="sample_attn",
    )(page_table, q2f, *([ck] * pps), *([cv] * pps), kn_rows, vn_rows, gb, bias_lp, bias_n,
      *[x.reshape(1, hd) for x in lams], subln_g.reshape(1, e))


def _new_token_rows(x2d, n_seq, n_heads):
    n, d = x2d.shape
    e = d // n_heads
    per_seq = (n // n_seq) * n_heads
    x = x2d.reshape(n_seq, per_seq, e)
    x = jnp.pad(x, ((0, 0), (0, LANES - per_seq), (0, 0)))
    return x.reshape(n_seq * LANES, e)


def _gelu_tanh(x):
    return 0.5 * x * (1.0 + jnp.tanh(math.sqrt(2.0 / math.pi) * (x + 0.044715 * (x * x * x))))


def _ffn_kernel(x_ref, za_ref, zb_ref, spad_ref, wout_ref, nffn_ref, wup_ref,
                cw_ref, cb_ref, wdown_ref, nfin_ref, y_ref, tail_ref, carry_ref, act_ref,
                *, group_mode, final_norm):
    ti = pl.program_id(1)
    tm, d = x_ref.shape
    g = tm // SUBLANES
    d_ff = wdown_ref.shape[0]

    merged = (za_ref[...].astype(F32) + zb_ref[...].astype(F32)).astype(BF16)
    x2 = x_ref[...] + jnp.dot(merged, wout_ref[...], preferred_element_type=F32)
    h2 = _rms(x2, nffn_ref[...]).astype(BF16)

    if not group_mode:
        @pl.when(ti == 0)
        def _():
            carry_ref[...] = spad_ref[0]

    def up_conv(cols):
        up3 = jnp.dot(h2, wup_ref[:, cols], preferred_element_type=F32).reshape(g, SUBLANES, -1)
        if group_mode:
            prev3 = spad_ref[:, :, cols]
            tail_ref[:, :, cols] = up3
        else:
            first = carry_ref[:, cols][None]
            prev3 = jnp.concatenate([first, up3[:g - 1]], axis=0) if g > 1 else first
            carry_ref[:, cols] = up3[g - 1]
            tail_ref[0, :, cols] = up3[g - 1]
        return _causal_conv(up3, prev3, cw_ref[:, cols], cb_ref[:, cols]).reshape(tm, -1)

    for c0 in range(0, d_ff, FFN_CHUNK):
        gate = up_conv(slice(c0, c0 + FFN_CHUNK))
        val = up_conv(slice(d_ff + c0, d_ff + c0 + FFN_CHUNK))
        act_ref[:, c0:c0 + FFN_CHUNK] = (_gelu_tanh(gate) * val).astype(BF16)
    x3 = x2 + jnp.dot(act_ref[...], wdown_ref[...], preferred_element_type=F32)
    y_ref[...] = _rms(x3, nfin_ref[...]) if final_norm else x3


def _ffn(x2d, za, zb, spad, wout_bf, nffn, wup_bf, cw, cb, wdown_bf, nfin,
         *, n_seq, group_mode, final_norm):
    n, d = x2d.shape
    f2 = wup_bf.shape[1]
    if group_mode:
        tm, grid = n, (1, 1)
        row = pl.BlockSpec((n, d), lambda b, t: (0, 0))
        seq = pl.BlockSpec((n_seq, SUBLANES, f2), lambda b, t: (0, 0, 0))
    else:
        t_len = n // n_seq
        tm = min(FFN_ROW_TILE, t_len)
        nt = t_len // tm
        grid = (n_seq, nt)
        row = pl.BlockSpec((tm, d), lambda b, t: (b * nt + t, 0))
        seq = pl.BlockSpec((1, SUBLANES, f2), lambda b, t: (b, 0, 0))
    return pl.pallas_call(
        functools.partial(_ffn_kernel, group_mode=group_mode, final_norm=final_norm),
        grid=grid,
        in_specs=[row, row, row, seq, _resident(wout_bf.shape), _resident((1, d)),
                  _resident(wup_bf.shape), _resident(cw.shape), _resident((1, f2)),
                  _resident(wdown_bf.shape), _resident((1, d))],
        out_specs=[row, seq],
        out_shape=[jax.ShapeDtypeStruct((n, d), F32), jax.ShapeDtypeStruct((n_seq, SUBLANES, f2), F32)],
        scratch_shapes=[pltpu.VMEM((SUBLANES, f2), F32), pltpu.VMEM((tm, f2 // 2), BF16)],
        compiler_params=_params(2),
        name="merge_ffn",
    )(x2d, za, zb, spad, wout_bf, nffn.reshape(1, d), wup_bf, cw, cb.reshape(1, f2),
      wdown_bf, nfin.reshape(1, d))


def _pad_state(buf):
    b, w1, c = buf.shape
    return jnp.concatenate([jnp.zeros((b, SUBLANES - w1, c), buf.dtype), buf], axis=1)


def kernel(x_prompt, x_sample, cache_k, cache_v, page_table, state_rglru_h, state_rglru_conv, state_ffn_conv, w_in, conv_w, conv_b, rg_w_a, rg_b_a, rg_w_x, rg_b_x, rg_a_param, lambda_q1, lambda_k1, lambda_q2, lambda_k2, subln_g, rel_bias, w_out, norm_attn, norm_ffn, w_up, ffn_conv_w, ffn_conv_b, w_down, norm_final):
    bsz, t_len, d = x_prompt.shape
    dbsz, dec_seq, _ = x_sample.shape
    depth = w_in.shape[0]
    n_heads = cache_k.shape[3]
    kv_dim = cache_k.shape[4]
    head_dim = kv_dim // 2
    page = cache_k.shape[2]
    past = page_table.shape[1] * page
    cw_w = conv_w.shape[1]
    fw_w = ffn_conv_w.shape[1]
    assert w_in.shape[2] == 6 * d and n_heads * kv_dim == d and rg_a_param.shape[1] == d
    assert dec_seq == SUBLANES and t_len % ATTN_BLOCK == 0 and page_table.shape[1] % PAGES_PER_STEP == 0
    assert ATTN_BLOCK >= MAX_DISTANCE and page >= MAX_DISTANCE and dec_seq * n_heads <= LANES
    assert kv_dim == LANES and n_heads == SUBLANES

    bias_p, bias_lp, bias_n = _bias_tiles(rel_bias, n_heads, ATTN_BLOCK, page, dec_seq)

    xp = x_prompt.reshape(bsz * t_len, d)
    xs = x_sample.reshape(dbsz * dec_seq, d)
    outs = [[] for _ in range(10)]
    for l in range(depth):
        lambda_init = 0.8 - 0.6 * math.exp(-0.3 * l)
        last = l == depth - 1
        w_in_bf = w_in[l].astype(BF16)
        wa_bf = rg_w_a[l].astype(BF16)
        wx_bf = rg_w_x[l].astype(BF16)
        wout_bf = w_out[l].astype(BF16)
        wup_bf = w_up[l].astype(BF16)
        wdown_bf = w_down[l].astype(BF16)
        lams = (lambda_q1[l], lambda_k1[l], lambda_q2[l], lambda_k2[l])

        def project(x2d, spad, h0, n_seq, group_mode, pos_base):
            h0b = jnp.broadcast_to(h0[:, None, :], (n_seq, SUBLANES, d))
            return _inproj(x2d, norm_attn[l], w_in_bf, spad, h0b, conv_w[l], conv_b[l], wa_bf,
                           rg_b_a[l], wx_bf, rg_b_x[l], rg_a_param[l], head_dim=head_dim,
                           n_seq=n_seq, group_mode=group_mode, pos_base=pos_base)

        def channel_mix(x2d, za, zb, spad, n_seq, group_mode):
            return _ffn(x2d, za, zb, spad, wout_bf, norm_ffn[l], wup_bf, ffn_conv_w[l],
                        ffn_conv_b[l], wdown_bf, norm_final, n_seq=n_seq, group_mode=group_mode,
                        final_norm=last)

        q2, k, kb, v, vt, gb, za, hp, xtail = project(
            xp, jnp.zeros((bsz, SUBLANES, d), F32), jnp.zeros((bsz, d), F32), bsz, False, 0)
        zb = _prompt_attn(q2, kb, vt, gb, bias_p, lams, subln_g[l], n_seq=bsz, n_heads=n_heads,
                          lambda_init=lambda_init)
        xp, fp = channel_mix(xp, za, zb, jnp.zeros((bsz, SUBLANES, ffn_conv_w.shape[2]), F32),
                             bsz, False)
        outs[0].append(k.reshape(bsz, t_len, n_heads, kv_dim))
        outs[1].append(v.reshape(bsz, t_len, n_heads, kv_dim))
        outs[4].append(hp[:, SUBLANES - 1])
        outs[6].append(xtail[:, SUBLANES - (cw_w - 1):])
        outs[8].append(fp[:, SUBLANES - (fw_w - 1):])

        q2, k, _, v, _, gb, za, hs, xtail = project(
            xs, _pad_state(state_rglru_conv[l]), state_rglru_h[l], dbsz, True, past)
        zb = _sample_attn(page_table, q2.astype(F32), cache_k[l], cache_v[l],
                          _new_token_rows(k, dbsz, n_heads), _new_token_rows(v, dbsz, n_heads),
                          gb, bias_lp, bias_n, lams, subln_g[l], n_seq=dbsz, n_heads=n_heads,
                          lambda_init=lambda_init)
        xs, fs = channel_mix(xs, za, zb, _pad_state(state_ffn_conv[l]), dbsz, True)
        outs[2].append(k.reshape(dbsz, dec_seq, n_heads, kv_dim))
        outs[3].append(v.reshape(dbsz, dec_seq, n_heads, kv_dim))
        outs[5].append(hs[:, SUBLANES - 1])
        outs[7].append(xtail[:, SUBLANES - (cw_w - 1):])
        outs[9].append(fs[:, SUBLANES - (fw_w - 1):])

    return (xp.reshape(bsz, t_len, d), xs.reshape(dbsz, dec_seq, d),
            jnp.stack(outs[0]), jnp.stack(outs[1]), jnp.stack(outs[2]), jnp.stack(outs[3]),
            jnp.stack(outs[4]), jnp.stack(outs[5]), jnp.stack(outs[6]), jnp.stack(outs[7]),
            jnp.stack(outs[8]), jnp.stack(outs[9]))
```

```python
import functools
import math

import jax
import jax.numpy as jnp
from jax import lax
from jax.experimental import pallas as pl
from jax.experimental.pallas import tpu as pltpu

F32 = jnp.float32
BF16 = jnp.bfloat16

EPS = 1e-6
RG_C = 8.0
N_BUCKETS = 32
MAX_DISTANCE = 128
NEG = -1e30
LOG2E = math.log2(math.e)

SUBLANES = 8
LANES = 128
ROW_TILE = 512
FFN_ROW_TILE = 512
ATTN_BLOCK = 512
FFN_CHUNK = 512
PAGES_PER_STEP = 16
PAGE_BUFFERS = 3
VMEM_LIMIT = 52 * 1024 * 1024
SAMPLE_VMEM_LIMIT = 58 * 1024 * 1024


def _params(n_axes, vmem=VMEM_LIMIT):
    return pltpu.CompilerParams(
        dimension_semantics=("arbitrary",) * n_axes, vmem_limit_bytes=vmem)


def _resident(shape):
    nd = len(shape)
    return pl.BlockSpec(shape, lambda *_: (0,) * nd, pipeline_mode=pl.Buffered(1))


def _sigmoid(x):
    return 0.5 * (1.0 + jnp.tanh(0.5 * x))


def _rms(x, g):
    return x * lax.rsqrt(jnp.mean(x * x, axis=-1, keepdims=True) + EPS) * g


def _inproj_kernel(x_ref, g_ref, w_ref, spad_ref, h0_ref, cw_ref, cb_ref, wa_ref, ba_ref, wx_ref,
                   bx_ref, ap_ref, q2_ref, k_ref, kb_ref, v_ref, vt_ref, gb_ref, za_ref, hlast_ref,
                   xtail_ref, carry_x, carry_h, a_scr, b_scr, ga_scr,
                   *, head_dim, group_mode, pos_base, n_blocks):
    ti = pl.program_id(1)
    tm, d = x_ref.shape
    h = _rms(x_ref[...], g_ref[...]).astype(BF16)

    if not group_mode:
        @pl.when(ti == 0)
        def _():
            carry_x[...] = spad_ref[0]
            carry_h[...] = h0_ref[0]

    pieces = 4
    pc = d // pieces
    assert n_blocks % pieces == 0 and tm % (pieces * SUBLANES) == 0 and pc % LANES == 0

    def cols_of(j, c):
        return slice(j * d + c * pc, j * d + (c + 1) * pc)

    def part(j, c):
        return jnp.dot(h, w_ref[:, cols_of(j, c)], preferred_element_type=F32)

    def emit_q(c):
        q = part(1, c) * (head_dim ** -0.5 * LOG2E)
        lane = lax.broadcasted_iota(jnp.int32, q.shape, 1)
        first = (lane % (2 * head_dim)) < head_dim
        q2_ref[0, :, cols_of(0, c)] = jnp.where(first, q, 0.0).astype(BF16)
        q2_ref[1, :, cols_of(0, c)] = jnp.where(first, 0.0, q).astype(BF16)

    def emit_k(c):
        k = part(2, c)
        k_ref[:, cols_of(0, c)] = k
        kb_ref[:, cols_of(0, c)] = k.astype(BF16)

    def emit_v(c):
        v = part(3, c)
        v_ref[:, cols_of(0, c)] = v
        vt_ref[cols_of(0, c), :] = v.T.astype(BF16)

    def emit_gb(c):
        gb_ref[:, cols_of(0, c)] = _sigmoid(part(5, c))

    emitters = (emit_q, emit_k, emit_v, emit_gb)
    bpp = n_blocks // pieces
    rq = tm // pieces
    for p, emit in enumerate(emitters):
        xr = part(0, p)
        ga_scr[:, cols_of(0, p)] = _sigmoid(part(4, p))
        for r in range(pieces):
            groups = slice(r * rq // SUBLANES, (r + 1) * rq // SUBLANES)
            a3, b3 = _rglru_group_scan(
                xr[r * rq:(r + 1) * rq], ti, cols_of(0, p), range(p * bpp, (p + 1) * bpp), groups,
                spad_ref, cw_ref, cb_ref, wa_ref, ba_ref, wx_ref, bx_ref, ap_ref, xtail_ref, carry_x,
                group_mode=group_mode, pos_base=pos_base, first_rows=r == 0)
            a_scr[groups, :, cols_of(0, p)] = a3
            b_scr[groups, :, cols_of(0, p)] = b3
            emit(r)

    h3 = _rglru_carry(h0_ref, hlast_ref, carry_h, a_scr, b_scr, group_mode=group_mode)
    za_ref[...] = (ga_scr[...] * h3.reshape(tm, d)).astype(za_ref.dtype)


def _inproj(x2d, g, w_bf, spad, h0b, cw, cb, wa_bf, ba, wx_bf, bx, ap,
            *, head_dim, n_seq, group_mode, pos_base):
    n, d = x2d.shape
    n_blocks = wa_bf.shape[0]
    if group_mode:
        tm, nt, grid = n, 1, (1, 1)
        seq = pl.BlockSpec((n_seq, SUBLANES, d), lambda b, t: (0, 0, 0))
    else:
        t_len = n // n_seq
        tm = min(ROW_TILE, t_len)
        nt = t_len // tm
        grid = (n_seq, nt)
        seq = pl.BlockSpec((1, SUBLANES, d), lambda b, t: (b, 0, 0))
    blk = pl.BlockSpec((tm, d), lambda b, t: (b * nt + t, 0))
    g8 = tm // SUBLANES
    vec = lambda a: a.reshape(1, d)
    f32 = jax.ShapeDtypeStruct((n, d), F32)
    b16 = jax.ShapeDtypeStruct((n, d), BF16)
    state = jax.ShapeDtypeStruct((n_seq, SUBLANES, d), F32)
    return pl.pallas_call(
        functools.partial(_inproj_kernel, head_dim=head_dim, group_mode=group_mode, pos_base=pos_base,
                          n_blocks=n_blocks),
        grid=grid,
        in_specs=[blk, _resident((1, d)), _resident(w_bf.shape), seq, seq, _resident(cw.shape),
                  _resident((1, d)), _resident(wa_bf.shape), _resident((1, d)), _resident(wx_bf.shape),
                  _resident((1, d)), _resident((1, d))],
        out_specs=[pl.BlockSpec((2, tm, d), lambda b, t: (0, b * nt + t, 0)), blk, blk, blk,
                   pl.BlockSpec((d, tm), lambda b, t: (0, b * nt + t)), blk, blk, seq, seq],
        out_shape=[jax.ShapeDtypeStruct((2, n, d), BF16), f32, b16, f32,
                   jax.ShapeDtypeStruct((d, n), BF16), f32, b16, state, state],
        scratch_shapes=[pltpu.VMEM((SUBLANES, d), F32), pltpu.VMEM((SUBLANES, d), F32),
                        pltpu.VMEM((g8, SUBLANES, d), F32), pltpu.VMEM((g8, SUBLANES, d), F32),
                        pltpu.VMEM((tm, d), F32)],
        compiler_params=_params(2),
        name="inproj",
    )(x2d, g.reshape(1, d), w_bf, spad, h0b, cw, vec(cb), wa_bf, vec(ba), wx_bf, vec(bx), vec(ap))


def _shifted(x3, prev3, d, t_idx):
    return jnp.where(t_idx >= d, pltpu.roll(x3, d, 1), pltpu.roll(prev3, d, 1))


def _causal_conv(x3, prev3, w, b):
    width = w.shape[0]
    t_idx = lax.broadcasted_iota(jnp.int32, x3.shape, 1)
    y = b + w[width - 1:width] * x3
    for dd in range(1, width):
        y = y + w[width - 1 - dd:width - dd] * _shifted(x3, prev3, dd, t_idx)
    return y


def _rglru_group_scan(xr, ti, cols, blocks, groups, spad_ref, cw_ref, cb_ref, wa_ref, ba_ref, wx_ref,
                      bx_ref, ap_ref, xtail_ref, carry_x, *, group_mode, pos_base, first_rows):
    tt, c = xr.shape
    g = tt // SUBLANES
    blk = c // len(blocks)
    x3 = xr.reshape(g, SUBLANES, c)

    if group_mode:
        prev3 = spad_ref[groups, :, cols]
        xtail_ref[groups, :, cols] = x3
    else:
        first = carry_x[:, cols][None]
        prev3 = jnp.concatenate([first, x3[:g - 1]], axis=0) if g > 1 else first

    xc3 = _causal_conv(x3, prev3, cw_ref[:, cols], cb_ref[:, cols])
    if not group_mode:
        carry_x[:, cols] = x3[g - 1]
        xtail_ref[0, :, cols] = x3[g - 1]

    xc = xc3.reshape(tt, c)
    xcb = xc.astype(BF16)

    def gate(w_ref, b_ref):
        parts = [jnp.dot(xcb[:, i * blk:(i + 1) * blk], w_ref[n], preferred_element_type=F32)
                 for i, n in enumerate(blocks)]
        return jnp.concatenate(parts, axis=1) + b_ref[:, cols]

    r = _sigmoid(gate(wa_ref, ba_ref))
    i = _sigmoid(gate(wx_ref, bx_ref))
    z = -ap_ref[:, cols]
    softplus = jnp.maximum(z, 0.0) + jnp.log(1.0 + jnp.exp(-jnp.abs(z)))
    log_a = -RG_C * r * softplus
    a = jnp.exp(log_a)
    th = jnp.tanh(log_a)
    mult = jnp.sqrt(-2.0 * th / (1.0 - th))
    ix = i * xc
    a3 = a.reshape(g, SUBLANES, c)
    b3 = (mult * ix).reshape(g, SUBLANES, c)
    t_idx = lax.broadcasted_iota(jnp.int32, a3.shape, 1)
    if pos_base == 0 and (group_mode or first_rows):
        ix3 = ix.reshape(g, SUBLANES, c)
        if group_mode:
            reset = t_idx == 0
            a3 = jnp.where(reset, 0.0, a3)
            b3 = jnp.where(reset, ix3, b3)
        else:
            reset = (lax.broadcasted_iota(jnp.int32, (SUBLANES, c), 0) + ti) == 0
            rest = lambda x3: [x3[1:]] if g > 1 else []
            a3 = jnp.concatenate([jnp.where(reset, 0.0, a3[0])[None]] + rest(a3), axis=0)
            b3 = jnp.concatenate([jnp.where(reset, ix3[0], b3[0])[None]] + rest(b3), axis=0)

    for dd in (1, 2, 4):
        valid = t_idx >= dd
        b3 = jnp.where(valid, a3 * pltpu.roll(b3, dd, 1) + b3, b3)
        a3 = jnp.where(valid, a3 * pltpu.roll(a3, dd, 1), a3)
    return a3, b3


def _rglru_carry(h0_ref, hlast_ref, carry_h, a_scr, b_scr, *, group_mode):
    if group_mode:
        h3 = a_scr[...] * h0_ref[...] + b_scr[...]
        hlast_ref[...] = h3
        return h3
    g, _, c = a_scr.shape

    def body(gi, carry):
        hg = a_scr[gi] * carry + b_scr[gi]
        b_scr[gi] = hg
        return jnp.broadcast_to(hg[SUBLANES - 1:SUBLANES, :], (SUBLANES, c))

    last = lax.fori_loop(0, g, body, carry_h[...])
    carry_h[...] = last
    hlast_ref[0] = last
    return b_scr[...]


def _bias_of(dist, valid, rb_ref, h):
    n = jnp.maximum(dist, 0)
    max_exact = N_BUCKETS // 2
    nf = jnp.maximum(n, 1).astype(F32)
    large = max_exact + (jnp.log(nf / max_exact) / math.log(MAX_DISTANCE / max_exact)
                         * (N_BUCKETS - max_exact)).astype(jnp.int32)
    large = jnp.minimum(large, N_BUCKETS - 1)
    bucket = jnp.where(n < max_exact, n, large)
    base = rb_ref[N_BUCKETS - 1, h]
    out = jnp.zeros(dist.shape, F32)
    for b in range(N_BUCKETS - 1):
        out = jnp.where(bucket == b, (rb_ref[b, h] - base) * LOG2E, out)
    return jnp.where(valid, out, NEG)


def _bias_kernel(rb_ref, bp_ref, blp_ref, bn_ref, *, tb, n_heads, dec_seq):
    h = pl.program_id(0)
    j = lax.broadcasted_iota(jnp.int32, (LANES, LANES), 0)
    i = lax.broadcasted_iota(jnp.int32, (LANES, LANES), 1)
    for jb in range(2 * tb // LANES):
        for ib in range(tb // LANES):
            base = (ib - jb) * LANES + tb
            rows, cols = slice(jb * LANES, (jb + 1) * LANES), slice(ib * LANES, (ib + 1) * LANES)
            if base - (LANES - 1) >= MAX_DISTANCE:
                bp_ref[0, rows, cols] = jnp.zeros((LANES, LANES), F32)
            elif base + (LANES - 1) < 0:
                bp_ref[0, rows, cols] = jnp.full((LANES, LANES), NEG, F32)
            else:
                dist = base + i - j
                bp_ref[0, rows, cols] = _bias_of(dist, dist >= 0, rb_ref, h)
    rows, page = blp_ref.shape
    t = lax.broadcasted_iota(jnp.int32, (rows, page), 0) % dec_seq
    key = lax.broadcasted_iota(jnp.int32, (rows, page), 1)
    dist = page + t - key
    blp_ref[...] = _bias_of(dist, dist >= 0, rb_ref, h)
    t = lax.broadcasted_iota(jnp.int32, (rows, LANES), 0) % dec_seq
    lane = lax.broadcasted_iota(jnp.int32, (rows, LANES), 1)
    dist = t - lane // n_heads
    bn_ref[...] = _bias_of(dist, jnp.logical_and((lane % n_heads) == h, dist >= 0), rb_ref, h)


def _bias_tiles(rel_bias, n_heads, tb, page, dec_seq):
    rows = 2 * dec_seq
    return pl.pallas_call(
        functools.partial(_bias_kernel, tb=tb, n_heads=n_heads, dec_seq=dec_seq),
        grid=(n_heads,),
        in_specs=[pl.BlockSpec(memory_space=pltpu.SMEM)],
        out_specs=[pl.BlockSpec((1, 2 * tb, tb), lambda h: (h, 0, 0)),
                   pl.BlockSpec((rows, page), lambda h: (h, 0)),
                   pl.BlockSpec((rows, LANES), lambda h: (h, 0))],
        out_shape=[jax.ShapeDtypeStruct((n_heads, 2 * tb, tb), F32),
                   jax.ShapeDtypeStruct((n_heads * rows, page), F32),
                   jax.ShapeDtypeStruct((n_heads * rows, LANES), F32)],
        compiler_params=_params(1),
        name="bias_tiles",
    )(rel_bias)


def _lambda(lq1, lk1, lq2, lk2, lambda_init):
    s1 = jnp.sum(lq1 * lk1, axis=-1, keepdims=True)
    s2 = jnp.sum(lq2 * lk2, axis=-1, keepdims=True)
    return jnp.exp(s1) - jnp.exp(s2) + lambda_init


def _qk(a, b):
    return lax.dot_general(a, b, (((1,), (1,)), ((), ())), preferred_element_type=F32)


def _all_sublanes(x, op):
    for dd in (1, 2, 4):
        x = op(x, pltpu.roll(x, dd, 0))
    return x


def _prompt_attn_kernel(q2_ref, k_ref, vt_ref, gb_ref, bias_ref, lq1_ref, lk1_ref, lq2_ref, lk2_ref,
                        sg_ref, o_ref, s0_ref, s1_ref, m0_ref, m1_ref, l0_ref, l1_ref, a0_ref, a1_ref,
                        pl0_ref, pl1_ref, pa0_ref, pa1_ref, *, tb, lambda_init):
    e, t_len = vt_ref.shape
    nq = t_len // tb
    s_refs = (s0_ref, s1_ref)
    m_refs = (m0_ref, m1_ref)
    l_refs = (l0_ref, l1_ref)
    acc_refs = (a0_ref, a1_ref)
    pl_refs = (pl0_ref, pl1_ref)
    pa_refs = (pa0_ref, pa1_ref)

    def block(i):
        return pl.ds(i * tb if isinstance(i, int) else pl.multiple_of(i * tb, tb), tb)

    def init():
        for mm in range(2):
            m_refs[mm][...] = jnp.full(m_refs[mm].shape, NEG, F32)
            l_refs[mm][...] = jnp.zeros(l_refs[mm].shape, F32)
            acc_refs[mm][...] = jnp.zeros(acc_refs[mm].shape, F32)

    def scores(mm, qi, kj):
        return _qk(k_ref[block(kj), :], q2_ref[mm, block(qi), :])

    def consume(mm, kj, bias):
        s = s_refs[mm][...]
        if bias is not None:
            s = s + bias
        s3 = s.reshape(tb // SUBLANES, SUBLANES, tb)
        m_prev = m_refs[mm][...]
        m_new = jnp.maximum(m_prev, _all_sublanes(jnp.max(s3, axis=0), jnp.maximum))
        alpha = jnp.exp2(m_prev - m_new)
        p3 = jnp.exp2(s3 - m_new[None])
        l_refs[mm][...] = alpha * l_refs[mm][...] + _all_sublanes(jnp.sum(p3, axis=0), jnp.add)
        pv = jnp.dot(vt_ref[:, block(kj)], p3.reshape(tb, tb).astype(BF16),
                     preferred_element_type=F32)
        acc = acc_refs[mm][...].reshape(e // SUBLANES, SUBLANES, tb) * alpha[None]
        acc_refs[mm][...] = acc.reshape(e, tb) + pv
        m_refs[mm][...] = m_new

    half = tb // 2

    def half_block(i, hi):
        start = i * tb + hi * half
        return pl.ds(start if isinstance(i, int) else pl.multiple_of(start, half), half)

    def diag_scores(mm, qi):
        s_refs[mm][:half, :] = _qk(k_ref[half_block(qi, 0), :], q2_ref[mm, block(qi), :])
        s_refs[mm][half:, half:] = _qk(k_ref[half_block(qi, 1), :], q2_ref[mm, half_block(qi, 1), :])

    def diag_consume(mm, qi, bias):
        sa3 = (s_refs[mm][:half, :] + bias[:half, :]).reshape(half // SUBLANES, SUBLANES, tb)
        sb3 = (s_refs[mm][half:, half:] + bias[half:, half:]).reshape(half // SUBLANES, SUBLANES, half)

        def late(x_all, x_late, op):
            return jnp.concatenate([x_all[:, :half], op(x_all[:, half:], x_late)], axis=1)

        m_prev = m_refs[mm][...]
        m_new = late(jnp.maximum(m_prev, _all_sublanes(jnp.max(sa3, axis=0), jnp.maximum)),
                     _all_sublanes(jnp.max(sb3, axis=0), jnp.maximum), jnp.maximum)
        alpha = jnp.exp2(m_prev - m_new)
        pa3 = jnp.exp2(sa3 - m_new[None])
        pb3 = jnp.exp2(sb3 - m_new[None, :, half:])
        sums = late(_all_sublanes(jnp.sum(pa3, axis=0), jnp.add),
                    _all_sublanes(jnp.sum(pb3, axis=0), jnp.add), jnp.add)
        l_refs[mm][...] = alpha * l_refs[mm][...] + sums
        pv = late(jnp.dot(vt_ref[:, half_block(qi, 0)], pa3.reshape(half, tb).astype(BF16),
                          preferred_element_type=F32),
                  jnp.dot(vt_ref[:, half_block(qi, 1)], pb3.reshape(half, half).astype(BF16),
                          preferred_element_type=F32), jnp.add)
        acc = acc_refs[mm][...].reshape(e // SUBLANES, SUBLANES, tb) * alpha[None]
        acc_refs[mm][...] = acc.reshape(e, tb) + pv
        m_refs[mm][...] = m_new

    def step(qi, kj, bias, next_q, next_k, diag=False, next_diag=False):
        if diag:
            diag_scores(1, qi)
            diag_consume(0, qi, bias)
        else:
            s1_ref[...] = scores(1, qi, kj)
            consume(0, kj, bias)
        if next_diag:
            diag_scores(0, next_q)
        else:
            s0_ref[...] = scores(0, next_q, next_k)
        if diag:
            diag_consume(1, qi, bias)
        else:
            consume(1, kj, bias)

    lam = _lambda(lq1_ref[...], lk1_ref[...], lq2_ref[...], lk2_ref[...], lambda_init)

    def park():
        for mm in range(2):
            pa_refs[mm][...] = acc_refs[mm][...]
            pl_refs[mm][...] = l_refs[mm][...]

    def finalize_parked(qi):
        def normalised(mm):
            acc = pa_refs[mm][...].reshape(e // SUBLANES, SUBLANES, tb)
            return (acc / pl_refs[mm][...][None]).reshape(e, tb)

        o = (normalised(0) - lam * normalised(1)).T
        y = _rms(o, sg_ref[...]) * (1.0 - lambda_init)
        o_ref[block(qi), :] = (gb_ref[block(qi), :] * y).astype(o_ref.dtype)

    bias_prev = lambda: bias_ref[0, :tb, :]
    bias_diag = lambda: bias_ref[0, tb:, :]

    init()
    diag_scores(0, 0)
    step(0, 0, bias_diag(), 1 if nq > 1 else 0, 0, diag=True)
    park()

    def tile(qi, carry):
        init()
        n_far = qi - 1
        one = n_far & 1
        two = n_far & 2

        @pl.when(one != 0)
        def _():
            step(qi, 0, None, qi, 1)

        @pl.when(two != 0)
        def _():
            step(qi, one, None, qi, one + 1)
            step(qi, one + 1, None, qi, one + 2)

        def far_quad(jj, c):
            j = one + two + 4 * jj
            for u in range(4):
                step(qi, j + u, None, qi, j + u + 1)
            return c

        lax.fori_loop(0, lax.shift_right_logical(n_far, 2), far_quad, 0)
        finalize_parked(qi - 1)
        step(qi, qi - 1, bias_prev(), qi, qi, next_diag=True)
        step(qi, qi, bias_diag(), jnp.minimum(qi + 1, nq - 1), 0, diag=True)
        park()
        return carry

    lax.fori_loop(1, nq, tile, 0)
    finalize_parked(nq - 1)


def _prompt_attn(q2, kb, vt, gb, bias_p, lams, subln_g, *, n_seq, n_heads, lambda_init):
    _, n, d = q2.shape
    e = d // n_heads
    t_len = n // n_seq
    tb = ATTN_BLOCK
    hd = lams[0].shape[-1]
    small = pl.BlockSpec((1, hd), lambda b, h: (0, 0))
    rows = pl.BlockSpec((t_len, e), lambda b, h: (b, h))
    stat = pltpu.VMEM((SUBLANES, tb), F32)
    return pl.pallas_call(
        functools.partial(_prompt_attn_kernel, tb=tb, lambda_init=lambda_init),
        grid=(n_seq, n_heads),
        in_specs=[pl.BlockSpec((2, t_len, e), lambda b, h: (0, b, h)),
                  rows,
                  pl.BlockSpec((e, t_len), lambda b, h: (h, b)),
                  rows,
                  pl.BlockSpec((1, 2 * tb, tb), lambda b, h: (h, 0, 0)),
                  small, small, small, small,
                  pl.BlockSpec((1, e), lambda b, h: (0, 0))],
        out_specs=rows,
        out_shape=jax.ShapeDtypeStruct((n, d), BF16),
        scratch_shapes=[pltpu.VMEM((tb, tb), F32), pltpu.VMEM((tb, tb), F32),
                        stat, stat, stat, stat,
                        pltpu.VMEM((e, tb), F32), pltpu.VMEM((e, tb), F32),
                        stat, stat,
                        pltpu.VMEM((e, tb), F32), pltpu.VMEM((e, tb), F32)],
        compiler_params=_params(2),
        name="prompt_attn",
    )(q2, kb, vt, gb, bias_p, *[x.reshape(1, hd) for x in lams], subln_g.reshape(1, e))


def _as_column(stat):
    r = stat.shape[1]
    row = lax.broadcasted_iota(jnp.int32, (r, r), 0)
    col = lax.broadcasted_iota(jnp.int32, (r, r), 1)
    full = jnp.concatenate([stat] * (r // SUBLANES), axis=0)
    return jnp.sum(jnp.where(row == col, full, 0.0), axis=1, keepdims=True)


def _sample_attn_kernel(pt_ref, q2_ref, ck_ref, cv_ref, kn_ref, vn_ref, gb_ref, blp_ref, bn_ref,
                        lq1_ref, lk1_ref, lq2_ref, lk2_ref, sg_ref, o_ref, q_ref, qt_ref, blpt_ref,
                        m_ref, l_ref, acc_ref, kbuf, vbuf, sems,
                        *, n_heads, dec_seq, pps, lambda_init):
    c = pl.program_id(1)
    nc = pl.num_programs(1)
    n_buf = kbuf.shape[0]
    e = kbuf.shape[3]
    d = n_heads * e
    page = kbuf.shape[2] // n_heads
    rows = 2 * dec_seq

    step = pl.program_id(0) * nc + c
    total = pl.num_programs(0) * nc

    def page_copies(s, slot):
        bb = s // nc
        cc = s - bb * nc
        out = []
        for j in range(pps):
            pg = pt_ref[bb, cc * pps + j]
            out.append(pltpu.make_async_copy(ck_ref.at[pg], kbuf.at[slot, j], sems.at[slot, 0]))
            out.append(pltpu.make_async_copy(cv_ref.at[pg], vbuf.at[slot, j], sems.at[slot, 1]))
        return out

    @pl.when(step == 0)
    def _():
        for s0 in range(n_buf - 1):
            for cp in page_copies(s0, s0):
                cp.start()

    ahead = step + (n_buf - 1)

    @pl.when(ahead < total)
    def _():
        for cp in page_copies(ahead, ahead % n_buf):
            cp.start()

    slot = step % n_buf
    for cp in page_copies(step, slot):
        cp.wait()
    kp = [kbuf.at[slot, j] for j in range(pps)]
    vp = [vbuf.at[slot, j] for j in range(pps)]

    @pl.when(c == 0)
    def _():
        lane_head = lax.broadcasted_iota(jnp.int32, (dec_seq, d), 1) // e
        q_ref[...] = jnp.concatenate(
            [q2_ref[mm][:, h * e:(h + 1) * e] for h in range(n_heads) for mm in range(2)], axis=0)
        qbd = jnp.concatenate(
            [jnp.where(lane_head == h, q2_ref[mm], 0.0) for h in range(n_heads) for mm in range(2)],
            axis=0)
        qt_ref[...] = qbd.T
        blpt_ref[...] = blp_ref[...].T
        m_ref[...] = jnp.full(m_ref.shape, NEG, F32)
        l_ref[...] = jnp.zeros(l_ref.shape, F32)
        acc_ref[...] = jnp.zeros(acc_ref.shape, F32)

    def head_rows(ref, h):
        return ref[pl.ds(h, page, stride=n_heads), :]

    is_last = (c == nc - 1).astype(F32)
    parts = []
    for j in range(pps):
        keys = jnp.concatenate([head_rows(kp[j], h) for h in range(n_heads)], axis=1)
        parts.append(jnp.dot(keys, qt_ref[...], preferred_element_type=F32))
    parts[-1] = parts[-1] + blpt_ref[...] * is_last
    s3 = jnp.concatenate(parts, axis=0).reshape(pps * page // SUBLANES, SUBLANES, -1)
    m_prev = m_ref[...]
    m_new = jnp.maximum(m_prev, _all_sublanes(jnp.max(s3, axis=0), jnp.maximum))
    alpha = jnp.exp2(m_prev - m_new)
    p3 = jnp.exp2(s3 - m_new[None])
    l_ref[...] = alpha * l_ref[...] + _all_sublanes(jnp.sum(p3, axis=0), jnp.add)
    m_ref[...] = m_new
    p = p3.reshape(pps * page, -1).T
    out = []
    for h in range(n_heads):
        ph = p[h * rows:(h + 1) * rows]
        pv = jnp.dot(ph[:, :page], head_rows(vp[0], h), preferred_element_type=F32)
        for j in range(1, pps):
            pv = pv + jnp.dot(ph[:, j * page:(j + 1) * page], head_rows(vp[j], h),
                              preferred_element_type=F32)
        out.append(pv)
    acc_ref[...] = _as_column(alpha) * acc_ref[...] + jnp.concatenate(out, axis=0)

    @pl.when(c == nc - 1)
    def _():
        s = _qk(q_ref[...], kn_ref[...]) + bn_ref[...]
        m_old = _as_column(m_ref[...])
        m_fin = jnp.maximum(m_old, jnp.max(s, axis=-1, keepdims=True))
        scale = jnp.exp2(m_old - m_fin)
        pn = jnp.exp2(s - m_fin)
        l_fin = scale * _as_column(l_ref[...]) + jnp.sum(pn, axis=-1, keepdims=True)
        acc = scale * acc_ref[...] + jnp.dot(pn, vn_ref[...], preferred_element_type=F32)
        lam = _lambda(lq1_ref[...], lk1_ref[...], lq2_ref[...], lk2_ref[...], lambda_init)
        o_all = acc / l_fin
        outs = []
        for h in range(n_heads):
            r0 = h * rows
            o = o_all[r0:r0 + dec_seq] - lam * o_all[r0 + dec_seq:r0 + rows]
            outs.append(_rms(o, sg_ref[...]) * (1.0 - lambda_init))
        o_ref[...] = gb_ref[...] * jnp.concatenate(outs, axis=1)


def _sample_attn(page_table, q2f, cache_k, cache_v, kn_rows, vn_rows, gb, bias_lp, bias_n, lams,
                 subln_g, *, n_seq, n_heads, lambda_init):
    _, n, d = q2f.shape
    dec_seq = n // n_seq
    e = d // n_heads
    n_pool, page = cache_k.shape[:2]
    n_pages = page_table.shape[1]
    pps = PAGES_PER_STEP
    nc = n_pages // pps
    hd = lams[0].shape[-1]
    rows = n_heads * 2 * dec_seq
    width = page * n_heads

    assert n_seq * nc >= PAGE_BUFFERS - 1
    whole = pl.BlockSpec(memory_space=pl.ANY)
    const2 = lambda b, c, pt: (0, 0)
    small = pl.BlockSpec((1, hd), const2)
    new_rows = pl.BlockSpec((LANES, e), lambda b, c, pt: (b, 0))
    seq = pl.BlockSpec((dec_seq, d), lambda b, c, pt: (b, 0))
    grid_spec = pltpu.PrefetchScalarGridSpec(
        num_scalar_prefetch=1,
        grid=(n_seq, nc),
        in_specs=[pl.BlockSpec((2, dec_seq, d), lambda b, c, pt: (0, b, 0)), whole, whole,
                  new_rows, new_rows, seq,
                  pl.BlockSpec((rows, page), const2), pl.BlockSpec((rows, LANES), const2),
                  small, small, small, small, pl.BlockSpec((1, e), const2)],
        out_specs=seq,
        scratch_shapes=[pltpu.VMEM((rows, e), F32), pltpu.VMEM((d, rows), F32),
                        pltpu.VMEM((page, rows), F32), pltpu.VMEM((SUBLANES, rows), F32),
                        pltpu.VMEM((SUBLANES, rows), F32), pltpu.VMEM((rows, e), F32),
                        pltpu.VMEM((PAGE_BUFFERS, pps, width, e), F32),
                        pltpu.VMEM((PAGE_BUFFERS, pps, width, e), F32),
                        pltpu.SemaphoreType.DMA((PAGE_BUFFERS, 2))],
    )
    ck = cache_k.reshape(n_pool, width, e)
    cv = cache_v.reshape(n_pool, width, e)
    return pl.pallas_call(
        functools.partial(_sample_attn_kernel, n_heads=n_heads, dec_seq=dec_seq, pps=pps,
                          lambda_init=lambda_init),
        grid_spec=grid_spec,
        out_shape=jax.ShapeDtypeStruct((n, d), F32),
        compiler_params=_params(2, vmem=SAMPLE_VMEM_LIMIT),
        name="sample_attn",
    )(page_table, q2f, ck, cv, kn_rows, vn_rows, gb, bias_lp, bias_n,
      *[x.reshape(1, hd) for x in lams], subln_g.reshape(1, e))


def _new_token_rows(x2d, n_seq, n_heads):
    n, d = x2d.shape
    e = d // n_heads
    per_seq = (n // n_seq) * n_heads
    x = x2d.reshape(n_seq, per_seq, e)
    x = jnp.pad(x, ((0, 0), (0, LANES - per_seq), (0, 0)))
    return x.reshape(n_seq * LANES, e)


def _gelu_tanh(x):
    return 0.5 * x * (1.0 + jnp.tanh(math.sqrt(2.0 / math.pi) * (x + 0.044715 * (x * x * x))))


def _ffn_kernel(x_ref, za_ref, zb_ref, spad_ref, wout_ref, nffn_ref, wup_ref,
                cw_ref, cb_ref, wdown_ref, nfin_ref, y_ref, tail_ref, carry_ref, act_ref,
                *, group_mode, final_norm):
    ti = pl.program_id(1)
    tm, d = x_ref.shape
    g = tm // SUBLANES
    d_ff = wdown_ref.shape[0]

    merged = (za_ref[...].astype(F32) + zb_ref[...].astype(F32)).astype(BF16)
    x2 = x_ref[...] + jnp.dot(merged, wout_ref[...], preferred_element_type=F32)
    h2 = _rms(x2, nffn_ref[...]).astype(BF16)

    if not group_mode:
        @pl.when(ti == 0)
        def _():
            carry_ref[...] = spad_ref[0]

    def up_conv(cols):
        up3 = jnp.dot(h2, wup_ref[:, cols], preferred_element_type=F32).reshape(g, SUBLANES, -1)
        if group_mode:
            prev3 = spad_ref[:, :, cols]
            tail_ref[:, :, cols] = up3
        else:
            first = carry_ref[:, cols][None]
            prev3 = jnp.concatenate([first, up3[:g - 1]], axis=0) if g > 1 else first
            carry_ref[:, cols] = up3[g - 1]
            tail_ref[0, :, cols] = up3[g - 1]
        return _causal_conv(up3, prev3, cw_ref[:, cols], cb_ref[:, cols]).reshape(tm, -1)

    for c0 in range(0, d_ff, FFN_CHUNK):
        gate = up_conv(slice(c0, c0 + FFN_CHUNK))
        val = up_conv(slice(d_ff + c0, d_ff + c0 + FFN_CHUNK))
        act_ref[:, c0:c0 + FFN_CHUNK] = (_gelu_tanh(gate) * val).astype(BF16)
    x3 = x2 + jnp.dot(act_ref[...], wdown_ref[...], preferred_element_type=F32)
    y_ref[...] = _rms(x3, nfin_ref[...]) if final_norm else x3


def _ffn(x2d, za, zb, spad, wout_bf, nffn, wup_bf, cw, cb, wdown_bf, nfin,
         *, n_seq, group_mode, final_norm):
    n, d = x2d.shape
    f2 = wup_bf.shape[1]
    if group_mode:
        tm, grid = n, (1, 1)
        row = pl.BlockSpec((n, d), lambda b, t: (0, 0))
        seq = pl.BlockSpec((n_seq, SUBLANES, f2), lambda b, t: (0, 0, 0))
    else:
        t_len = n // n_seq
        tm = min(FFN_ROW_TILE, t_len)
        nt = t_len // tm
        grid = (n_seq, nt)
        row = pl.BlockSpec((tm, d), lambda b, t: (b * nt + t, 0))
        seq = pl.BlockSpec((1, SUBLANES, f2), lambda b, t: (b, 0, 0))
    return pl.pallas_call(
        functools.partial(_ffn_kernel, group_mode=group_mode, final_norm=final_norm),
        grid=grid,
        in_specs=[row, row, row, seq, _resident(wout_bf.shape), _resident((1, d)),
                  _resident(wup_bf.shape), _resident(cw.shape), _resident((1, f2)),
                  _resident(wdown_bf.shape), _resident((1, d))],
        out_specs=[row, seq],
        out_shape=[jax.ShapeDtypeStruct((n, d), F32), jax.ShapeDtypeStruct((n_seq, SUBLANES, f2), F32)],
        scratch_shapes=[pltpu.VMEM((SUBLANES, f2), F32), pltpu.VMEM((tm, f2 // 2), BF16)],
        compiler_params=_params(2),
        name="merge_ffn",
    )(x2d, za, zb, spad, wout_bf, nffn.reshape(1, d), wup_bf, cw, cb.reshape(1, f2),
      wdown_bf, nfin.reshape(1, d))


def _pad_state(buf):
    b, w1, c = buf.shape
    return jnp.concatenate([jnp.zeros((b, SUBLANES - w1, c), buf.dtype), buf], axis=1)


def kernel(x_prompt, x_sample, cache_k, cache_v, page_table, state_rglru_h, state_rglru_conv, state_ffn_conv, w_in, conv_w, conv_b, rg_w_a, rg_b_a, rg_w_x, rg_b_x, rg_a_param, lambda_q1, lambda_k1, lambda_q2, lambda_k2, subln_g, rel_bias, w_out, norm_attn, norm_ffn, w_up, ffn_conv_w, ffn_conv_b, w_down, norm_final):
    bsz, t_len, d = x_prompt.shape
    dbsz, dec_seq, _ = x_sample.shape
    depth = w_in.shape[0]
    n_heads = cache_k.shape[3]
    kv_dim = cache_k.shape[4]
    head_dim = kv_dim // 2
    page = cache_k.shape[2]
    past = page_table.shape[1] * page
    cw_w = conv_w.shape[1]
    fw_w = ffn_conv_w.shape[1]
    assert w_in.shape[2] == 6 * d and n_heads * kv_dim == d and rg_a_param.shape[1] == d
    assert dec_seq == SUBLANES and t_len % ATTN_BLOCK == 0 and page_table.shape[1] % PAGES_PER_STEP == 0
    assert ATTN_BLOCK >= MAX_DISTANCE and page >= MAX_DISTANCE and dec_seq * n_heads <= LANES
    assert kv_dim == LANES and n_heads == SUBLANES

    bias_p, bias_lp, bias_n = _bias_tiles(rel_bias, n_heads, ATTN_BLOCK, page, dec_seq)

    xp = x_prompt.reshape(bsz * t_len, d)
    xs = x_sample.reshape(dbsz * dec_seq, d)
    outs = [[] for _ in range(10)]
    for l in range(depth):
        lambda_init = 0.8 - 0.6 * math.exp(-0.3 * l)
        last = l == depth - 1
        w_in_bf = w_in[l].astype(BF16)
        wa_bf = rg_w_a[l].astype(BF16)
        wx_bf = rg_w_x[l].astype(BF16)
        wout_bf = w_out[l].astype(BF16)
        wup_bf = w_up[l].astype(BF16)
        wdown_bf = w_down[l].astype(BF16)
        lams = (lambda_q1[l], lambda_k1[l], lambda_q2[l], lambda_k2[l])

        def project(x2d, spad, h0, n_seq, group_mode, pos_base):
            h0b = jnp.broadcast_to(h0[:, None, :], (n_seq, SUBLANES, d))
            return _inproj(x2d, norm_attn[l], w_in_bf, spad, h0b, conv_w[l], conv_b[l], wa_bf,
                           rg_b_a[l], wx_bf, rg_b_x[l], rg_a_param[l], head_dim=head_dim,
                           n_seq=n_seq, group_mode=group_mode, pos_base=pos_base)

        def channel_mix(x2d, za, zb, spad, n_seq, group_mode):
            return _ffn(x2d, za, zb, spad, wout_bf, norm_ffn[l], wup_bf, ffn_conv_w[l],
                        ffn_conv_b[l], wdown_bf, norm_final, n_seq=n_seq, group_mode=group_mode,
                        final_norm=last)

        q2, k, kb, v, vt, gb, za, hp, xtail = project(
            xp, jnp.zeros((bsz, SUBLANES, d), F32), jnp.zeros((bsz, d), F32), bsz, False, 0)
        zb = _prompt_attn(q2, kb, vt, gb, bias_p, lams, subln_g[l], n_seq=bsz, n_heads=n_heads,
                          lambda_init=lambda_init)
        xp, fp = channel_mix(xp, za, zb, jnp.zeros((bsz, SUBLANES, ffn_conv_w.shape[2]), F32),
                             bsz, False)
        outs[0].append(k.reshape(bsz, t_len, n_heads, kv_dim))
        outs[1].append(v.reshape(bsz, t_len, n_heads, kv_dim))
        outs[4].append(hp[:, SUBLANES - 1])
        outs[6].append(xtail[:, SUBLANES - (cw_w - 1):])
        outs[8].append(fp[:, SUBLANES - (fw_w - 1):])

        q2, k, _, v, _, gb, za, hs, xtail = project(
            xs, _pad_state(state_rglru_conv[l]), state_rglru_h[l], dbsz, True, past)
        zb = _sample_attn(page_table, q2.astype(F32), cache_k[l], cache_v[l],
                          _new_token_rows(k, dbsz, n_heads), _new_token_rows(v, dbsz, n_heads),
                          gb, bias_lp, bias_n, lams, subln_g[l], n_seq=dbsz, n_heads=n_heads,
                          lambda_init=lambda_init)
        xs, fs = channel_mix(xs, za, zb, _pad_state(state_ffn_conv[l]), dbsz, True)
        outs[2].append(k.reshape(dbsz, dec_seq, n_heads, kv_dim))
        outs[3].append(v.reshape(dbsz, dec_seq, n_heads, kv_dim))
        outs[5].append(hs[:, SUBLANES - 1])
        outs[7].append(xtail[:, SUBLANES - (cw_w - 1):])
        outs[9].append(fs[:, SUBLANES - (fw_w - 1):])

    return (xp.reshape(bsz, t_len, d), xs.reshape(dbsz, dec_seq, d),
            jnp.stack(outs[0]), jnp.stack(outs[1]), jnp.stack(outs[2]), jnp.stack(outs[3]),
            jnp.stack(outs[4]), jnp.stack(outs[5]), jnp.stack(outs[6]), jnp.stack(outs[7]),
            jnp.stack(outs[8]), jnp.stack(outs[9]))
```

```python
import functools
import math

import jax
import jax.numpy as jnp
from jax import lax
from jax.experimental import pallas as pl
from jax.experimental.pallas import tpu as pltpu

F32 = jnp.float32
BF16 = jnp.bfloat16

EPS = 1e-6
RG_C = 8.0
N_BUCKETS = 32
MAX_DISTANCE = 128
NEG = -1e30
LOG2E = math.log2(math.e)

SUBLANES = 8
LANES = 128
ROW_TILE = 512
FFN_ROW_TILE = 512
ATTN_BLOCK = 512
FFN_CHUNK = 512
PAGES_PER_STEP = 8
PAGE_BUFFERS = 6
VMEM_LIMIT = 52 * 1024 * 1024
SAMPLE_VMEM_LIMIT = 58 * 1024 * 1024


def _params(n_axes, vmem=VMEM_LIMIT):
    return pltpu.CompilerParams(
        dimension_semantics=("arbitrary",) * n_axes, vmem_limit_bytes=vmem)


def _resident(shape):
    nd = len(shape)
    return pl.BlockSpec(shape, lambda *_: (0,) * nd, pipeline_mode=pl.Buffered(1))


def _sigmoid(x):
    return 0.5 * (1.0 + jnp.tanh(0.5 * x))


def _rms(x, g):
    return x * lax.rsqrt(jnp.mean(x * x, axis=-1, keepdims=True) + EPS) * g


def _inproj_kernel(x_ref, g_ref, w_ref, spad_ref, h0_ref, cw_ref, cb_ref, wa_ref, ba_ref, wx_ref,
                   bx_ref, ap_ref, q2_ref, k_ref, kb_ref, v_ref, vt_ref, gb_ref, za_ref, hlast_ref,
                   xtail_ref, carry_x, carry_h, a_scr, b_scr, ga_scr,
                   *, head_dim, group_mode, pos_base, n_blocks):
    ti = pl.program_id(1)
    tm, d = x_ref.shape
    h = _rms(x_ref[...], g_ref[...]).astype(BF16)

    if not group_mode:
        @pl.when(ti == 0)
        def _():
            carry_x[...] = spad_ref[0]
            carry_h[...] = h0_ref[0]

    pieces = 4
    pc = d // pieces
    assert n_blocks % pieces == 0 and tm % (pieces * SUBLANES) == 0 and pc % LANES == 0

    def cols_of(j, c):
        return slice(j * d + c * pc, j * d + (c + 1) * pc)

    def part(j, c):
        return jnp.dot(h, w_ref[:, cols_of(j, c)], preferred_element_type=F32)

    def emit_q(c):
        q = part(1, c) * (head_dim ** -0.5 * LOG2E)
        lane = lax.broadcasted_iota(jnp.int32, q.shape, 1)
        first = (lane % (2 * head_dim)) < head_dim
        q2_ref[0, :, cols_of(0, c)] = jnp.where(first, q, 0.0).astype(BF16)
        q2_ref[1, :, cols_of(0, c)] = jnp.where(first, 0.0, q).astype(BF16)

    def emit_k(c):
        k = part(2, c)
        k_ref[:, cols_of(0, c)] = k
        kb_ref[:, cols_of(0, c)] = k.astype(BF16)

    def emit_v(c):
        v = part(3, c)
        v_ref[:, cols_of(0, c)] = v
        vt_ref[cols_of(0, c), :] = v.T.astype(BF16)

    def emit_gb(c):
        gb_ref[:, cols_of(0, c)] = _sigmoid(part(5, c))

    emitters = (emit_q, emit_k, emit_v, emit_gb)
    bpp = n_blocks // pieces
    rq = tm // pieces
    for p, emit in enumerate(emitters):
        xr = part(0, p)
        ga_scr[:, cols_of(0, p)] = _sigmoid(part(4, p))
        for r in range(pieces):
            groups = slice(r * rq // SUBLANES, (r + 1) * rq // SUBLANES)
            a3, b3 = _rglru_group_scan(
                xr[r * rq:(r + 1) * rq], ti, cols_of(0, p), range(p * bpp, (p + 1) * bpp), groups,
                spad_ref, cw_ref, cb_ref, wa_ref, ba_ref, wx_ref, bx_ref, ap_ref, xtail_ref, carry_x,
                group_mode=group_mode, pos_base=pos_base, first_rows=r == 0)
            a_scr[groups, :, cols_of(0, p)] = a3
            b_scr[groups, :, cols_of(0, p)] = b3
            emit(r)

    h3 = _rglru_carry(h0_ref, hlast_ref, carry_h, a_scr, b_scr, group_mode=group_mode)
    za_ref[...] = (ga_scr[...] * h3.reshape(tm, d)).astype(za_ref.dtype)


def _inproj(x2d, g, w_bf, spad, h0b, cw, cb, wa_bf, ba, wx_bf, bx, ap,
            *, head_dim, n_seq, group_mode, pos_base):
    n, d = x2d.shape
    n_blocks = wa_bf.shape[0]
    if group_mode:
        tm, nt, grid = n, 1, (1, 1)
        seq = pl.BlockSpec((n_seq, SUBLANES, d), lambda b, t: (0, 0, 0))
    else:
        t_len = n // n_seq
        tm = min(ROW_TILE, t_len)
        nt = t_len // tm
        grid = (n_seq, nt)
        seq = pl.BlockSpec((1, SUBLANES, d), lambda b, t: (b, 0, 0))
    blk = pl.BlockSpec((tm, d), lambda b, t: (b * nt + t, 0))
    g8 = tm // SUBLANES
    vec = lambda a: a.reshape(1, d)
    f32 = jax.ShapeDtypeStruct((n, d), F32)
    b16 = jax.ShapeDtypeStruct((n, d), BF16)
    state = jax.ShapeDtypeStruct((n_seq, SUBLANES, d), F32)
    return pl.pallas_call(
        functools.partial(_inproj_kernel, head_dim=head_dim, group_mode=group_mode, pos_base=pos_base,
                          n_blocks=n_blocks),
        grid=grid,
        in_specs=[blk, _resident((1, d)), _resident(w_bf.shape), seq, seq, _resident(cw.shape),
                  _resident((1, d)), _resident(wa_bf.shape), _resident((1, d)), _resident(wx_bf.shape),
                  _resident((1, d)), _resident((1, d))],
        out_specs=[pl.BlockSpec((2, tm, d), lambda b, t: (0, b * nt + t, 0)), blk, blk, blk,
                   pl.BlockSpec((d, tm), lambda b, t: (0, b * nt + t)), blk, blk, seq, seq],
        out_shape=[jax.ShapeDtypeStruct((2, n, d), BF16), f32, b16, f32,
                   jax.ShapeDtypeStruct((d, n), BF16), f32, b16, state, state],
        scratch_shapes=[pltpu.VMEM((SUBLANES, d), F32), pltpu.VMEM((SUBLANES, d), F32),
                        pltpu.VMEM((g8, SUBLANES, d), F32), pltpu.VMEM((g8, SUBLANES, d), F32),
                        pltpu.VMEM((tm, d), F32)],
        compiler_params=_params(2),
        name="inproj",
    )(x2d, g.reshape(1, d), w_bf, spad, h0b, cw, vec(cb), wa_bf, vec(ba), wx_bf, vec(bx), vec(ap))


def _shifted(x3, prev3, d, t_idx):
    return jnp.where(t_idx >= d, pltpu.roll(x3, d, 1), pltpu.roll(prev3, d, 1))


def _causal_conv(x3, prev3, w, b):
    width = w.shape[0]
    t_idx = lax.broadcasted_iota(jnp.int32, x3.shape, 1)
    y = b + w[width - 1:width] * x3
    for dd in range(1, width):
        y = y + w[width - 1 - dd:width - dd] * _shifted(x3, prev3, dd, t_idx)
    return y


def _rglru_group_scan(xr, ti, cols, blocks, groups, spad_ref, cw_ref, cb_ref, wa_ref, ba_ref, wx_ref,
                      bx_ref, ap_ref, xtail_ref, carry_x, *, group_mode, pos_base, first_rows):
    tt, c = xr.shape
    g = tt // SUBLANES
    blk = c // len(blocks)
    x3 = xr.reshape(g, SUBLANES, c)

    if group_mode:
        prev3 = spad_ref[groups, :, cols]
        xtail_ref[groups, :, cols] = x3
    else:
        first = carry_x[:, cols][None]
        prev3 = jnp.concatenate([first, x3[:g - 1]], axis=0) if g > 1 else first

    xc3 = _causal_conv(x3, prev3, cw_ref[:, cols], cb_ref[:, cols])
    if not group_mode:
        carry_x[:, cols] = x3[g - 1]
        xtail_ref[0, :, cols] = x3[g - 1]

    xc = xc3.reshape(tt, c)
    xcb = xc.astype(BF16)

    def gate(w_ref, b_ref):
        parts = [jnp.dot(xcb[:, i * blk:(i + 1) * blk], w_ref[n], preferred_element_type=F32)
                 for i, n in enumerate(blocks)]
        return jnp.concatenate(parts, axis=1) + b_ref[:, cols]

    r = _sigmoid(gate(wa_ref, ba_ref))
    i = _sigmoid(gate(wx_ref, bx_ref))
    z = -ap_ref[:, cols]
    softplus = jnp.maximum(z, 0.0) + jnp.log(1.0 + jnp.exp(-jnp.abs(z)))
    log_a = -RG_C * r * softplus
    a = jnp.exp(log_a)
    th = jnp.tanh(log_a)
    mult = jnp.sqrt(-2.0 * th / (1.0 - th))
    ix = i * xc
    a3 = a.reshape(g, SUBLANES, c)
    b3 = (mult * ix).reshape(g, SUBLANES, c)
    t_idx = lax.broadcasted_iota(jnp.int32, a3.shape, 1)
    if pos_base == 0 and (group_mode or first_rows):
        ix3 = ix.reshape(g, SUBLANES, c)
        if group_mode:
            reset = t_idx == 0
            a3 = jnp.where(reset, 0.0, a3)
            b3 = jnp.where(reset, ix3, b3)
        else:
            reset = (lax.broadcasted_iota(jnp.int32, (SUBLANES, c), 0) + ti) == 0
            rest = lambda x3: [x3[1:]] if g > 1 else []
            a3 = jnp.concatenate([jnp.where(reset, 0.0, a3[0])[None]] + rest(a3), axis=0)
            b3 = jnp.concatenate([jnp.where(reset, ix3[0], b3[0])[None]] + rest(b3), axis=0)

    for dd in (1, 2, 4):
        valid = t_idx >= dd
        b3 = jnp.where(valid, a3 * pltpu.roll(b3, dd, 1) + b3, b3)
        a3 = jnp.where(valid, a3 * pltpu.roll(a3, dd, 1), a3)
    return a3, b3


def _rglru_carry(h0_ref, hlast_ref, carry_h, a_scr, b_scr, *, group_mode):
    if group_mode:
        h3 = a_scr[...] * h0_ref[...] + b_scr[...]
        hlast_ref[...] = h3
        return h3
    g, _, c = a_scr.shape

    def body(gi, carry):
        hg = a_scr[gi] * carry + b_scr[gi]
        b_scr[gi] = hg
        return jnp.broadcast_to(hg[SUBLANES - 1:SUBLANES, :], (SUBLANES, c))

    last = lax.fori_loop(0, g, body, carry_h[...])
    carry_h[...] = last
    hlast_ref[0] = last
    return b_scr[...]


def _bias_of(dist, valid, rb_ref, h):
    n = jnp.maximum(dist, 0)
    max_exact = N_BUCKETS // 2
    nf = jnp.maximum(n, 1).astype(F32)
    large = max_exact + (jnp.log(nf / max_exact) / math.log(MAX_DISTANCE / max_exact)
                         * (N_BUCKETS - max_exact)).astype(jnp.int32)
    large = jnp.minimum(large, N_BUCKETS - 1)
    bucket = jnp.where(n < max_exact, n, large)
    base = rb_ref[N_BUCKETS - 1, h]
    out = jnp.zeros(dist.shape, F32)
    for b in range(N_BUCKETS - 1):
        out = jnp.where(bucket == b, (rb_ref[b, h] - base) * LOG2E, out)
    return jnp.where(valid, out, NEG)


def _bias_kernel(rb_ref, bp_ref, blp_ref, bn_ref, *, tb, n_heads, dec_seq):
    h = pl.program_id(0)
    j = lax.broadcasted_iota(jnp.int32, (LANES, LANES), 0)
    i = lax.broadcasted_iota(jnp.int32, (LANES, LANES), 1)
    for jb in range(2 * tb // LANES):
        for ib in range(tb // LANES):
            base = (ib - jb) * LANES + tb
            rows, cols = slice(jb * LANES, (jb + 1) * LANES), slice(ib * LANES, (ib + 1) * LANES)
            if base - (LANES - 1) >= MAX_DISTANCE:
                bp_ref[0, rows, cols] = jnp.zeros((LANES, LANES), F32)
            elif base + (LANES - 1) < 0:
                bp_ref[0, rows, cols] = jnp.full((LANES, LANES), NEG, F32)
            else:
                dist = base + i - j
                bp_ref[0, rows, cols] = _bias_of(dist, dist >= 0, rb_ref, h)
    rows, page = blp_ref.shape
    t = lax.broadcasted_iota(jnp.int32, (rows, page), 0) % dec_seq
    key = lax.broadcasted_iota(jnp.int32, (rows, page), 1)
    dist = page + t - key
    blp_ref[...] = _bias_of(dist, dist >= 0, rb_ref, h)
    t = lax.broadcasted_iota(jnp.int32, (rows, LANES), 0) % dec_seq
    lane = lax.broadcasted_iota(jnp.int32, (rows, LANES), 1)
    dist = t - lane // n_heads
    bn_ref[...] = _bias_of(dist, jnp.logical_and((lane % n_heads) == h, dist >= 0), rb_ref, h)


def _bias_tiles(rel_bias, n_heads, tb, page, dec_seq):
    rows = 2 * dec_seq
    return pl.pallas_call(
        functools.partial(_bias_kernel, tb=tb, n_heads=n_heads, dec_seq=dec_seq),
        grid=(n_heads,),
        in_specs=[pl.BlockSpec(memory_space=pltpu.SMEM)],
        out_specs=[pl.BlockSpec((1, 2 * tb, tb), lambda h: (h, 0, 0)),
                   pl.BlockSpec((rows, page), lambda h: (h, 0)),
                   pl.BlockSpec((rows, LANES), lambda h: (h, 0))],
        out_shape=[jax.ShapeDtypeStruct((n_heads, 2 * tb, tb), F32),
                   jax.ShapeDtypeStruct((n_heads * rows, page), F32),
                   jax.ShapeDtypeStruct((n_heads * rows, LANES), F32)],
        compiler_params=_params(1),
        name="bias_tiles",
    )(rel_bias)


def _lambda(lq1, lk1, lq2, lk2, lambda_init):
    s1 = jnp.sum(lq1 * lk1, axis=-1, keepdims=True)
    s2 = jnp.sum(lq2 * lk2, axis=-1, keepdims=True)
    return jnp.exp(s1) - jnp.exp(s2) + lambda_init


def _qk(a, b):
    return lax.dot_general(a, b, (((1,), (1,)), ((), ())), preferred_element_type=F32)


def _all_sublanes(x, op):
    for dd in (1, 2, 4):
        x = op(x, pltpu.roll(x, dd, 0))
    return x


def _prompt_attn_kernel(q2_ref, k_ref, vt_ref, gb_ref, bias_ref, lq1_ref, lk1_ref, lq2_ref, lk2_ref,
                        sg_ref, o_ref, s0_ref, s1_ref, m0_ref, m1_ref, l0_ref, l1_ref, a0_ref, a1_ref,
                        pl0_ref, pl1_ref, pa0_ref, pa1_ref, *, tb, lambda_init):
    e, t_len = vt_ref.shape
    nq = t_len // tb
    s_refs = (s0_ref, s1_ref)
    m_refs = (m0_ref, m1_ref)
    l_refs = (l0_ref, l1_ref)
    acc_refs = (a0_ref, a1_ref)
    pl_refs = (pl0_ref, pl1_ref)
    pa_refs = (pa0_ref, pa1_ref)

    def block(i):
        return pl.ds(i * tb if isinstance(i, int) else pl.multiple_of(i * tb, tb), tb)

    def init():
        for mm in range(2):
            m_refs[mm][...] = jnp.full(m_refs[mm].shape, NEG, F32)
            l_refs[mm][...] = jnp.zeros(l_refs[mm].shape, F32)
            acc_refs[mm][...] = jnp.zeros(acc_refs[mm].shape, F32)

    def scores(mm, qi, kj):
        return _qk(k_ref[block(kj), :], q2_ref[mm, block(qi), :])

    def consume(mm, kj, bias):
        s = s_refs[mm][...]
        if bias is not None:
            s = s + bias
        s3 = s.reshape(tb // SUBLANES, SUBLANES, tb)
        m_prev = m_refs[mm][...]
        m_new = jnp.maximum(m_prev, _all_sublanes(jnp.max(s3, axis=0), jnp.maximum))
        alpha = jnp.exp2(m_prev - m_new)
        p3 = jnp.exp2(s3 - m_new[None])
        l_refs[mm][...] = alpha * l_refs[mm][...] + _all_sublanes(jnp.sum(p3, axis=0), jnp.add)
        pv = jnp.dot(vt_ref[:, block(kj)], p3.reshape(tb, tb).astype(BF16),
                     preferred_element_type=F32)
        acc = acc_refs[mm][...].reshape(e // SUBLANES, SUBLANES, tb) * alpha[None]
        acc_refs[mm][...] = acc.reshape(e, tb) + pv
        m_refs[mm][...] = m_new

    half = tb // 2

    def half_block(i, hi):
        start = i * tb + hi * half
        return pl.ds(start if isinstance(i, int) else pl.multiple_of(start, half), half)

    def diag_scores(mm, qi):
        s_refs[mm][:half, :] = _qk(k_ref[half_block(qi, 0), :], q2_ref[mm, block(qi), :])
        s_refs[mm][half:, half:] = _qk(k_ref[half_block(qi, 1), :], q2_ref[mm, half_block(qi, 1), :])

    def diag_consume(mm, qi, bias):
        sa3 = (s_refs[mm][:half, :] + bias[:half, :]).reshape(half // SUBLANES, SUBLANES, tb)
        sb3 = (s_refs[mm][half:, half:] + bias[half:, half:]).reshape(half // SUBLANES, SUBLANES, half)

        def late(x_all, x_late, op):
            return jnp.concatenate([x_all[:, :half], op(x_all[:, half:], x_late)], axis=1)

        m_prev = m_refs[mm][...]
        m_new = late(jnp.maximum(m_prev, _all_sublanes(jnp.max(sa3, axis=0), jnp.maximum)),
                     _all_sublanes(jnp.max(sb3, axis=0), jnp.maximum), jnp.maximum)
        alpha = jnp.exp2(m_prev - m_new)
        pa3 = jnp.exp2(sa3 - m_new[None])
        pb3 = jnp.exp2(sb3 - m_new[None, :, half:])
        sums = late(_all_sublanes(jnp.sum(pa3, axis=0), jnp.add),
                    _all_sublanes(jnp.sum(pb3, axis=0), jnp.add), jnp.add)
        l_refs[mm][...] = alpha * l_refs[mm][...] + sums
        pv = late(jnp.dot(vt_ref[:, half_block(qi, 0)], pa3.reshape(half, tb).astype(BF16),
                          preferred_element_type=F32),
                  jnp.dot(vt_ref[:, half_block(qi, 1)], pb3.reshape(half, half).astype(BF16),
                          preferred_element_type=F32), jnp.add)
        acc = acc_refs[mm][...].reshape(e // SUBLANES, SUBLANES, tb) * alpha[None]
        acc_refs[mm][...] = acc.reshape(e, tb) + pv
        m_refs[mm][...] = m_new

    def step(qi, kj, bias, next_q, next_k, diag=False, next_diag=False):
        if diag:
            diag_scores(1, qi)
            diag_consume(0, qi, bias)
        else:
            s1_ref[...] = scores(1, qi, kj)
            consume(0, kj, bias)
        if next_diag:
            diag_scores(0, next_q)
        else:
            s0_ref[...] = scores(0, next_q, next_k)
        if diag:
            diag_consume(1, qi, bias)
        else:
            consume(1, kj, bias)

    lam = _lambda(lq1_ref[...], lk1_ref[...], lq2_ref[...], lk2_ref[...], lambda_init)

    def park():
        for mm in range(2):
            pa_refs[mm][...] = acc_refs[mm][...]
            pl_refs[mm][...] = l_refs[mm][...]

    def finalize_parked(qi):
        def normalised(mm):
            acc = pa_refs[mm][...].reshape(e // SUBLANES, SUBLANES, tb)
            return (acc / pl_refs[mm][...][None]).reshape(e, tb)

        o = (normalised(0) - lam * normalised(1)).T
        y = _rms(o, sg_ref[...]) * (1.0 - lambda_init)
        o_ref[block(qi), :] = (gb_ref[block(qi), :] * y).astype(o_ref.dtype)

    bias_prev = lambda: bias_ref[0, :tb, :]
    bias_diag = lambda: bias_ref[0, tb:, :]

    init()
    diag_scores(0, 0)
    step(0, 0, bias_diag(), 1 if nq > 1 else 0, 0, diag=True)
    park()

    def tile(qi, carry):
        init()
        n_far = qi - 1
        one = n_far & 1
        two = n_far & 2

        @pl.when(one != 0)
        def _():
            step(qi, 0, None, qi, 1)

        @pl.when(two != 0)
        def _():
            step(qi, one, None, qi, one + 1)
            step(qi, one + 1, None, qi, one + 2)

        def far_quad(jj, c):
            j = one + two + 4 * jj
            for u in range(4):
                step(qi, j + u, None, qi, j + u + 1)
            return c

        lax.fori_loop(0, lax.shift_right_logical(n_far, 2), far_quad, 0)
        finalize_parked(qi - 1)
        step(qi, qi - 1, bias_prev(), qi, qi, next_diag=True)
        step(qi, qi, bias_diag(), jnp.minimum(qi + 1, nq - 1), 0, diag=True)
        park()
        return carry

    lax.fori_loop(1, nq, tile, 0)
    finalize_parked(nq - 1)


def _prompt_attn(q2, kb, vt, gb, bias_p, lams, subln_g, *, n_seq, n_heads, lambda_init):
    _, n, d = q2.shape
    e = d // n_heads
    t_len = n // n_seq
    tb = ATTN_BLOCK
    hd = lams[0].shape[-1]
    small = pl.BlockSpec((1, hd), lambda b, h: (0, 0))
    rows = pl.BlockSpec((t_len, e), lambda b, h: (b, h))
    stat = pltpu.VMEM((SUBLANES, tb), F32)
    return pl.pallas_call(
        functools.partial(_prompt_attn_kernel, tb=tb, lambda_init=lambda_init),
        grid=(n_seq, n_heads),
        in_specs=[pl.BlockSpec((2, t_len, e), lambda b, h: (0, b, h)),
                  rows,
                  pl.BlockSpec((e, t_len), lambda b, h: (h, b)),
                  rows,
                  pl.BlockSpec((1, 2 * tb, tb), lambda b, h: (h, 0, 0)),
                  small, small, small, small,
                  pl.BlockSpec((1, e), lambda b, h: (0, 0))],
        out_specs=rows,
        out_shape=jax.ShapeDtypeStruct((n, d), BF16),
        scratch_shapes=[pltpu.VMEM((tb, tb), F32), pltpu.VMEM((tb, tb), F32),
                        stat, stat, stat, stat,
                        pltpu.VMEM((e, tb), F32), pltpu.VMEM((e, tb), F32),
                        stat, stat,
                        pltpu.VMEM((e, tb), F32), pltpu.VMEM((e, tb), F32)],
        compiler_params=_params(2),
        name="prompt_attn",
    )(q2, kb, vt, gb, bias_p, *[x.reshape(1, hd) for x in lams], subln_g.reshape(1, e))


def _as_column(stat):
    r = stat.shape[1]
    row = lax.broadcasted_iota(jnp.int32, (r, r), 0)
    col = lax.broadcasted_iota(jnp.int32, (r, r), 1)
    full = jnp.concatenate([stat] * (r // SUBLANES), axis=0)
    return jnp.sum(jnp.where(row == col, full, 0.0), axis=1, keepdims=True)


def _sample_attn_kernel(pt_ref, q2_ref, ck_ref, cv_ref, kn_ref, vn_ref, gb_ref, blp_ref, bn_ref,
                        lq1_ref, lk1_ref, lq2_ref, lk2_ref, sg_ref, o_ref, q_ref, qt_ref, blpt_ref,
                        m_ref, l_ref, acc_ref, kbuf, vbuf, sems,
                        *, n_heads, dec_seq, pps, lambda_init):
    c = pl.program_id(1)
    nc = pl.num_programs(1)
    n_buf = kbuf.shape[0]
    e = kbuf.shape[3]
    d = n_heads * e
    page = kbuf.shape[2] // n_heads
    rows = 2 * dec_seq

    step = pl.program_id(0) * nc + c
    total = pl.num_programs(0) * nc

    def page_copies(s, slot):
        bb = s // nc
        cc = s - bb * nc
        out = []
        for j in range(pps):
            pg = pt_ref[bb, cc * pps + j]
            out.append(pltpu.make_async_copy(ck_ref.at[pg], kbuf.at[slot, j], sems.at[slot, 0]))
            out.append(pltpu.make_async_copy(cv_ref.at[pg], vbuf.at[slot, j], sems.at[slot, 1]))
        return out

    @pl.when(step == 0)
    def _():
        for s0 in range(n_buf - 1):
            for cp in page_copies(s0, s0):
                cp.start()

    ahead = step + (n_buf - 1)

    @pl.when(ahead < total)
    def _():
        for cp in page_copies(ahead, ahead % n_buf):
            cp.start()

    slot = step % n_buf
    for cp in page_copies(step, slot):
        cp.wait()
    kp = [kbuf.at[slot, j] for j in range(pps)]
    vp = [vbuf.at[slot, j] for j in range(pps)]

    @pl.when(c == 0)
    def _():
        lane_head = lax.broadcasted_iota(jnp.int32, (dec_seq, d), 1) // e
        q_ref[...] = jnp.concatenate(
            [q2_ref[mm][:, h * e:(h + 1) * e] for h in range(n_heads) for mm in range(2)], axis=0)
        qbd = jnp.concatenate(
            [jnp.where(lane_head == h, q2_ref[mm], 0.0) for h in range(n_heads) for mm in range(2)],
            axis=0)
        qt_ref[...] = qbd.T
        blpt_ref[...] = blp_ref[...].T
        m_ref[...] = jnp.full(m_ref.shape, NEG, F32)
        l_ref[...] = jnp.zeros(l_ref.shape, F32)
        acc_ref[...] = jnp.zeros(acc_ref.shape, F32)

    def head_rows(ref, h):
        return ref[pl.ds(h, page, stride=n_heads), :]

    is_last = (c == nc - 1).astype(F32)
    parts = []
    for j in range(pps):
        keys = jnp.concatenate([head_rows(kp[j], h) for h in range(n_heads)], axis=1)
        parts.append(jnp.dot(keys, qt_ref[...], preferred_element_type=F32))
    parts[-1] = parts[-1] + blpt_ref[...] * is_last
    s3 = jnp.concatenate(parts, axis=0).reshape(pps * page // SUBLANES, SUBLANES, -1)
    m_prev = m_ref[...]
    m_new = jnp.maximum(m_prev, _all_sublanes(jnp.max(s3, axis=0), jnp.maximum))
    alpha = jnp.exp2(m_prev - m_new)
    p3 = jnp.exp2(s3 - m_new[None])
    l_ref[...] = alpha * l_ref[...] + _all_sublanes(jnp.sum(p3, axis=0), jnp.add)
    m_ref[...] = m_new
    p = p3.reshape(pps * page, -1).T
    out = []
    for h in range(n_heads):
        ph = p[h * rows:(h + 1) * rows]
        pv = jnp.dot(ph[:, :page], head_rows(vp[0], h), preferred_element_type=F32)
        for j in range(1, pps):
            pv = pv + jnp.dot(ph[:, j * page:(j + 1) * page], head_rows(vp[j], h),
                              preferred_element_type=F32)
        out.append(pv)
    acc_ref[...] = _as_column(alpha) * acc_ref[...] + jnp.concatenate(out, axis=0)

    @pl.when(c == nc - 1)
    def _():
        s = _qk(q_ref[...], kn_ref[...]) + bn_ref[...]
        m_old = _as_column(m_ref[...])
        m_fin = jnp.maximum(m_old, jnp.max(s, axis=-1, keepdims=True))
        scale = jnp.exp2(m_old - m_fin)
        pn = jnp.exp2(s - m_fin)
        l_fin = scale * _as_column(l_ref[...]) + jnp.sum(pn, axis=-1, keepdims=True)
        acc = scale * acc_ref[...] + jnp.dot(pn, vn_ref[...], preferred_element_type=F32)
        lam = _lambda(lq1_ref[...], lk1_ref[...], lq2_ref[...], lk2_ref[...], lambda_init)
        o_all = acc / l_fin
        outs = []
        for h in range(n_heads):
            r0 = h * rows
            o = o_all[r0:r0 + dec_seq] - lam * o_all[r0 + dec_seq:r0 + rows]
            outs.append(_rms(o, sg_ref[...]) * (1.0 - lambda_init))
        o_ref[...] = gb_ref[...] * jnp.concatenate(outs, axis=1)


def _sample_attn(page_table, q2f, cache_k, cache_v, kn_rows, vn_rows, gb, bias_lp, bias_n, lams,
                 subln_g, *, n_seq, n_heads, lambda_init):
    _, n, d = q2f.shape
    dec_seq = n // n_seq
    e = d // n_heads
    n_pool, page = cache_k.shape[:2]
    n_pages = page_table.shape[1]
    pps = PAGES_PER_STEP
    nc = n_pages // pps
    hd = lams[0].shape[-1]
    rows = n_heads * 2 * dec_seq
    width = page * n_heads

    assert n_seq * nc >= PAGE_BUFFERS - 1
    whole = pl.BlockSpec(memory_space=pl.ANY)
    const2 = lambda b, c, pt: (0, 0)
    small = pl.BlockSpec((1, hd), const2)
    new_rows = pl.BlockSpec((LANES, e), lambda b, c, pt: (b, 0))
    seq = pl.BlockSpec((dec_seq, d), lambda b, c, pt: (b, 0))
    grid_spec = pltpu.PrefetchScalarGridSpec(
        num_scalar_prefetch=1,
        grid=(n_seq, nc),
        in_specs=[pl.BlockSpec((2, dec_seq, d), lambda b, c, pt: (0, b, 0)), whole, whole,
                  new_rows, new_rows, seq,
                  pl.BlockSpec((rows, page), const2), pl.BlockSpec((rows, LANES), const2),
                  small, small, small, small, pl.BlockSpec((1, e), const2)],
        out_specs=seq,
        scratch_shapes=[pltpu.VMEM((rows, e), F32), pltpu.VMEM((d, rows), F32),
                        pltpu.VMEM((page, rows), F32), pltpu.VMEM((SUBLANES, rows), F32),
                        pltpu.VMEM((SUBLANES, rows), F32), pltpu.VMEM((rows, e), F32),
                        pltpu.VMEM((PAGE_BUFFERS, pps, width, e), F32),
                        pltpu.VMEM((PAGE_BUFFERS, pps, width, e), F32),
                        pltpu.SemaphoreType.DMA((PAGE_BUFFERS, 2))],
    )
    ck = cache_k.reshape(n_pool, width, e)
    cv = cache_v.reshape(n_pool, width, e)
    return pl.pallas_call(
        functools.partial(_sample_attn_kernel, n_heads=n_heads, dec_seq=dec_seq, pps=pps,
                          lambda_init=lambda_init),
        grid_spec=grid_spec,
        out_shape=jax.ShapeDtypeStruct((n, d), F32),
        compiler_params=_params(2, vmem=SAMPLE_VMEM_LIMIT),
        name="sample_attn",
    )(page_table, q2f, ck, cv, kn_rows, vn_rows, gb, bias_lp, bias_n,
      *[x.reshape(1, hd) for x in lams], subln_g.reshape(1, e))


def _new_token_rows(x2d, n_seq, n_heads):
    n, d = x2d.shape
    e = d // n_heads
    per_seq = (n // n_seq) * n_heads
    x = x2d.reshape(n_seq, per_seq, e)
    x = jnp.pad(x, ((0, 0), (0, LANES - per_seq), (0, 0)))
    return x.reshape(n_seq * LANES, e)


def _gelu_tanh(x):
    return 0.5 * x * (1.0 + jnp.tanh(math.sqrt(2.0 / math.pi) * (x + 0.044715 * (x * x * x))))


def _ffn_kernel(x_ref, za_ref, zb_ref, spad_ref, wout_ref, nffn_ref, wup_ref,
                cw_ref, cb_ref, wdown_ref, nfin_ref, y_ref, tail_ref, carry_ref, act_ref,
                *, group_mode, final_norm):
    ti = pl.program_id(1)
    tm, d = x_ref.shape
    g = tm // SUBLANES
    d_ff = wdown_ref.shape[0]

    merged = (za_ref[...].astype(F32) + zb_ref[...].astype(F32)).astype(BF16)
    x2 = x_ref[...] + jnp.dot(merged, wout_ref[...], preferred_element_type=F32)
    h2 = _rms(x2, nffn_ref[...]).astype(BF16)

    if not group_mode:
        @pl.when(ti == 0)
        def _():
            carry_ref[...] = spad_ref[0]

    def up_conv(cols):
        up3 = jnp.dot(h2, wup_ref[:, cols], preferred_element_type=F32).reshape(g, SUBLANES, -1)
        if group_mode:
            prev3 = spad_ref[:, :, cols]
            tail_ref[:, :, cols] = up3
        else:
            first = carry_ref[:, cols][None]
            prev3 = jnp.concatenate([first, up3[:g - 1]], axis=0) if g > 1 else first
            carry_ref[:, cols] = up3[g - 1]
            tail_ref[0, :, cols] = up3[g - 1]
        return _causal_conv(up3, prev3, cw_ref[:, cols], cb_ref[:, cols]).reshape(tm, -1)

    for c0 in range(0, d_ff, FFN_CHUNK):
        gate = up_conv(slice(c0, c0 + FFN_CHUNK))
        val = up_conv(slice(d_ff + c0, d_ff + c0 + FFN_CHUNK))
        act_ref[:, c0:c0 + FFN_CHUNK] = (_gelu_tanh(gate) * val).astype(BF16)
    x3 = x2 + jnp.dot(act_ref[...], wdown_ref[...], preferred_element_type=F32)
    y_ref[...] = _rms(x3, nfin_ref[...]) if final_norm else x3


def _ffn(x2d, za, zb, spad, wout_bf, nffn, wup_bf, cw, cb, wdown_bf, nfin,
         *, n_seq, group_mode, final_norm):
    n, d = x2d.shape
    f2 = wup_bf.shape[1]
    if group_mode:
        tm, grid = n, (1, 1)
        row = pl.BlockSpec((n, d), lambda b, t: (0, 0))
        seq = pl.BlockSpec((n_seq, SUBLANES, f2), lambda b, t: (0, 0, 0))
    else:
        t_len = n // n_seq
        tm = min(FFN_ROW_TILE, t_len)
        nt = t_len // tm
        grid = (n_seq, nt)
        row = pl.BlockSpec((tm, d), lambda b, t: (b * nt + t, 0))
        seq = pl.BlockSpec((1, SUBLANES, f2), lambda b, t: (b, 0, 0))
    return pl.pallas_call(
        functools.partial(_ffn_kernel, group_mode=group_mode, final_norm=final_norm),
        grid=grid,
        in_specs=[row, row, row, seq, _resident(wout_bf.shape), _resident((1, d)),
                  _resident(wup_bf.shape), _resident(cw.shape), _resident((1, f2)),
                  _resident(wdown_bf.shape), _resident((1, d))],
        out_specs=[row, seq],
        out_shape=[jax.ShapeDtypeStruct((n, d), F32), jax.ShapeDtypeStruct((n_seq, SUBLANES, f2), F32)],
        scratch_shapes=[pltpu.VMEM((SUBLANES, f2), F32), pltpu.VMEM((tm, f2 // 2), BF16)],
        compiler_params=_params(2),
        name="merge_ffn",
    )(x2d, za, zb, spad, wout_bf, nffn.reshape(1, d), wup_bf, cw, cb.reshape(1, f2),
      wdown_bf, nfin.reshape(1, d))


def _pad_state(buf):
    b, w1, c = buf.shape
    return jnp.concatenate([jnp.zeros((b, SUBLANES - w1, c), buf.dtype), buf], axis=1)


def kernel(x_prompt, x_sample, cache_k, cache_v, page_table, state_rglru_h, state_rglru_conv, state_ffn_conv, w_in, conv_w, conv_b, rg_w_a, rg_b_a, rg_w_x, rg_b_x, rg_a_param, lambda_q1, lambda_k1, lambda_q2, lambda_k2, subln_g, rel_bias, w_out, norm_attn, norm_ffn, w_up, ffn_conv_w, ffn_conv_b, w_down, norm_final):
    bsz, t_len, d = x_prompt.shape
    dbsz, dec_seq, _ = x_sample.shape
    depth = w_in.shape[0]
    n_heads = cache_k.shape[3]
    kv_dim = cache_k.shape[4]
    head_dim = kv_dim // 2
    page = cache_k.shape[2]
    past = page_table.shape[1] * page
    cw_w = conv_w.shape[1]
    fw_w = ffn_conv_w.shape[1]
    assert w_in.shape[2] == 6 * d and n_heads * kv_dim == d and rg_a_param.shape[1] == d
    assert dec_seq == SUBLANES and t_len % ATTN_BLOCK == 0 and page_table.shape[1] % PAGES_PER_STEP == 0
    assert ATTN_BLOCK >= MAX_DISTANCE and page >= MAX_DISTANCE and dec_seq * n_heads <= LANES
    assert kv_dim == LANES and n_heads == SUBLANES

    bias_p, bias_lp, bias_n = _bias_tiles(rel_bias, n_heads, ATTN_BLOCK, page, dec_seq)

    xp = x_prompt.reshape(bsz * t_len, d)
    xs = x_sample.reshape(dbsz * dec_seq, d)
    outs = [[] for _ in range(10)]
    for l in range(depth):
        lambda_init = 0.8 - 0.6 * math.exp(-0.3 * l)
        last = l == depth - 1
        w_in_bf = w_in[l].astype(BF16)
        wa_bf = rg_w_a[l].astype(BF16)
        wx_bf = rg_w_x[l].astype(BF16)
        wout_bf = w_out[l].astype(BF16)
        wup_bf = w_up[l].astype(BF16)
        wdown_bf = w_down[l].astype(BF16)
        lams = (lambda_q1[l], lambda_k1[l], lambda_q2[l], lambda_k2[l])

        def project(x2d, spad, h0, n_seq, group_mode, pos_base):
            h0b = jnp.broadcast_to(h0[:, None, :], (n_seq, SUBLANES, d))
            return _inproj(x2d, norm_attn[l], w_in_bf, spad, h0b, conv_w[l], conv_b[l], wa_bf,
                           rg_b_a[l], wx_bf, rg_b_x[l], rg_a_param[l], head_dim=head_dim,
                           n_seq=n_seq, group_mode=group_mode, pos_base=pos_base)

        def channel_mix(x2d, za, zb, spad, n_seq, group_mode):
            return _ffn(x2d, za, zb, spad, wout_bf, norm_ffn[l], wup_bf, ffn_conv_w[l],
                        ffn_conv_b[l], wdown_bf, norm_final, n_seq=n_seq, group_mode=group_mode,
                        final_norm=last)

        q2, k, kb, v, vt, gb, za, hp, xtail = project(
            xp, jnp.zeros((bsz, SUBLANES, d), F32), jnp.zeros((bsz, d), F32), bsz, False, 0)
        zb = _prompt_attn(q2, kb, vt, gb, bias_p, lams, subln_g[l], n_seq=bsz, n_heads=n_heads,
                          lambda_init=lambda_init)
        xp, fp = channel_mix(xp, za, zb, jnp.zeros((bsz, SUBLANES, ffn_conv_w.shape[2]), F32),
                             bsz, False)
        outs[0].append(k.reshape(bsz, t_len, n_heads, kv_dim))
        outs[1].append(v.reshape(bsz, t_len, n_heads, kv_dim))
        outs[4].append(hp[:, SUBLANES - 1])
        outs[6].append(xtail[:, SUBLANES - (cw_w - 1):])
        outs[8].append(fp[:, SUBLANES - (fw_w - 1):])

        q2, k, _, v, _, gb, za, hs, xtail = project(
            xs, _pad_state(state_rglru_conv[l]), state_rglru_h[l], dbsz, True, past)
        zb = _sample_attn(page_table, q2.astype(F32), cache_k[l], cache_v[l],
                          _new_token_rows(k, dbsz, n_heads), _new_token_rows(v, dbsz, n_heads),
                          gb, bias_lp, bias_n, lams, subln_g[l], n_seq=dbsz, n_heads=n_heads,
                          lambda_init=lambda_init)
        xs, fs = channel_mix(xs, za, zb, _pad_state(state_ffn_conv[l]), dbsz, True)
        outs[2].append(k.reshape(dbsz, dec_seq, n_heads, kv_dim))
        outs[3].append(v.reshape(dbsz, dec_seq, n_heads, kv_dim))
        outs[5].append(hs[:, SUBLANES - 1])
        outs[7].append(xtail[:, SUBLANES - (cw_w - 1):])
        outs[9].append(fs[:, SUBLANES - (fw_w - 1):])

    return (xp.reshape(bsz, t_len, d), xs.reshape(dbsz, dec_seq, d),
            jnp.stack(outs[0]), jnp.stack(outs[1]), jnp.stack(outs[2]), jnp.stack(outs[3]),
            jnp.stack(outs[4]), jnp.stack(outs[5]), jnp.stack(outs[6]), jnp.stack(outs[7]),
            jnp.stack(outs[8]), jnp.stack(outs[9]))
```
